```python
import math
import jax
import jax.numpy as jnp
from jax import lax
import numpy as np

D_MODEL = 1024
BATCH = 2
SEQ = 8192
DEPTH = 1

EPS = 1e-6
D_MIX = D_MODEL

GLA_HEADS = 4
GLA_DK = 64
GLA_DV = 128
GLA_WIDTH = GLA_HEADS * GLA_DV
GLA_QK_WIDTH = GLA_HEADS * GLA_DK
GLA_GATE_RANK = 16
GLA_GATE_NORMALIZER = 16.0
GLA_CHUNK = 64

SSD_HEADS = 8
SSD_HEAD_DIM = 64
SSD_WIDTH = SSD_HEADS * SSD_HEAD_DIM
SSD_GROUPS = 2
SSD_STATE = 64
SSD_CONV = 4
SSD_CHUNK = 128
SSD_BC_WIDTH = SSD_GROUPS * SSD_STATE
SSD_CONV_DIM = SSD_WIDTH + 2 * SSD_BC_WIDTH

IN_SIZES = (GLA_QK_WIDTH, GLA_QK_WIDTH, GLA_WIDTH, GLA_GATE_RANK, GLA_WIDTH,
            SSD_WIDTH, SSD_CONV_DIM, SSD_HEADS)
IN_TOTAL = (2 * GLA_QK_WIDTH + 2 * GLA_WIDTH + GLA_GATE_RANK
            + SSD_WIDTH + SSD_CONV_DIM + SSD_HEADS)

PEER_HEADS = 8
PEER_KEYS = 128
PEER_EXPERTS = PEER_KEYS * PEER_KEYS
PEER_TOPK = 16
PEER_DQ = 256
PEER_HALF = PEER_DQ // 2
PEER_BLOCK = 128

kernel_name = "hybrid_gla_ssd_peer_layer"


def rmsnorm(x, w):
    xf = x.astype(jnp.float32)
    xf = xf * lax.rsqrt(jnp.mean(xf * xf, axis=-1, keepdims=True) + EPS)
    return xf * w.astype(jnp.float32)


def split_cols(proj):
    out, start = [], 0
    for size in IN_SIZES:
        out.append(proj[..., start:start + size])
        start += size
    return out


def gla_mixer(q, k, v, gate_lr, g_out, w_gate2, b_gate, norm_w):
    f32 = jnp.float32
    b, l, _ = q.shape
    nc, C, H = l // GLA_CHUNK, GLA_CHUNK, GLA_HEADS
    q = q.astype(f32).reshape(b, nc, C, H, GLA_DK) * (GLA_DK ** -0.5)
    k = k.astype(f32).reshape(b, nc, C, H, GLA_DK)
    v = v.astype(f32).reshape(b, nc, C, H, GLA_DV)
    log_a = jax.nn.log_sigmoid((gate_lr.astype(f32) @ w_gate2.astype(f32)
                                + b_gate.astype(f32))) / GLA_GATE_NORMALIZER
    log_a = log_a.reshape(b, nc, C, H, GLA_DK)
    G = jnp.cumsum(log_a, axis=2)
    G_last = G[:, :, -1:]
    q_in = q * jnp.exp(G)
    k_in = k * jnp.exp(-G)
    k_end = k * jnp.exp(G_last - G)
    causal = jnp.tril(jnp.ones((C, C), dtype=bool))
    att = jnp.einsum('bcihd,bcjhd->bchij', q_in, k_in)
    att = jnp.where(causal, att, 0.0)
    o_intra = jnp.einsum('bchij,bcjhv->bcihv', att, v)
    chunk_state = jnp.einsum('bcjhd,bcjhv->bchdv', k_end, v)
    chunk_decay = jnp.exp(G_last[:, :, 0])

    def step(S, inp):
        dec, st = inp
        return dec[..., None] * S + st, S

    S0 = jnp.zeros((b, H, GLA_DK, GLA_DV), f32)
    _, S_prev = lax.scan(step, S0, (jnp.moveaxis(chunk_decay, 1, 0),
                                    jnp.moveaxis(chunk_state, 1, 0)))
    S_prev = jnp.moveaxis(S_prev, 0, 1)
    o_inter = jnp.einsum('bcihd,bchdv->bcihv', q_in, S_prev)
    o = (o_intra + o_inter).reshape(b, l, H, GLA_DV)
    o = rmsnorm(o, norm_w)
    return o.reshape(b, l, GLA_WIDTH) * jax.nn.silu(g_out.astype(f32))


def causal_depthwise_conv(x, w, bias):
    c = x.shape[-1]
    out = lax.conv_general_dilated(
        x, w.astype(x.dtype)[:, None, :], window_strides=(1,),
        padding=[(SSD_CONV - 1, 0)], dimension_numbers=('NWC', 'WIO', 'NWC'),
        feature_group_count=c)
    return out + bias.astype(x.dtype)


def ssd_mixer(z, xbc, dt_raw, conv_w, conv_b, dt_bias, a_log, d_skip, norm_w):
    f32 = jnp.float32
    b, l, _ = z.shape
    nc, C = l // SSD_CHUNK, SSD_CHUNK
    G, J, P, N = SSD_GROUPS, SSD_HEADS // SSD_GROUPS, SSD_HEAD_DIM, SSD_STATE
    xbc = jax.nn.silu(causal_depthwise_conv(xbc.astype(f32), conv_w, conv_b))
    xs = xbc[..., :SSD_WIDTH]
    Bm = xbc[..., SSD_WIDTH:SSD_WIDTH + SSD_BC_WIDTH].reshape(b, nc, C, G, N)
    Cm = xbc[..., SSD_WIDTH + SSD_BC_WIDTH:].reshape(b, nc, C, G, N)
    X = xs.reshape(b, nc, C, G, J, P)
    dt = jax.nn.softplus(dt_raw.astype(f32) + dt_bias.astype(f32)).reshape(b, nc, C, G, J)
    A = -jnp.exp(a_log.astype(f32)).reshape(G, J)
    a = dt * A
    a_cum = jnp.cumsum(a, axis=2)
    seg = a_cum[:, :, :, None] - a_cum[:, :, None]
    causal = jnp.tril(jnp.ones((C, C), dtype=bool))[:, :, None, None]
    Lmat = jnp.exp(jnp.where(causal, seg, -jnp.inf))
    Xdt = X * dt[..., None]
    scores = jnp.einsum('bclgn,bcsgn->bclsg', Cm, Bm)
    y_diag = jnp.einsum('bclsg,bclsgj,bcsgjp->bclgjp', scores, Lmat, Xdt)
    decay_to_end = jnp.exp(a_cum[:, :, -1:] - a_cum)
    states = jnp.einsum('bclgn,bclgj,bclgjp->bcgjpn', Bm, decay_to_end, Xdt)
    chunk_decay = jnp.exp(a_cum[:, :, -1])

    def step(S, inp):
        dec, st = inp
        return dec[..., None, None] * S + st, S

    S0 = jnp.zeros((b, G, J, P, N), f32)
    _, S_prev = lax.scan(step, S0, (jnp.moveaxis(chunk_decay, 1, 0),
                                    jnp.moveaxis(states, 1, 0)))
    S_prev = jnp.moveaxis(S_prev, 0, 1)
    y_off = jnp.einsum('bclgn,bcgjpn,bclgj->bclgjp', Cm, S_prev, jnp.exp(a_cum))
    y = y_diag + y_off + X * d_skip.astype(f32).reshape(G, J)[..., None]
    y = y.reshape(b, l, SSD_WIDTH) * jax.nn.silu(z.astype(f32))
    y = y.reshape(b, l, G, SSD_WIDTH // G)
    y = y * lax.rsqrt(jnp.mean(y * y, axis=-1, keepdims=True) + EPS)
    return y.reshape(b, l, SSD_WIDTH) * norm_w.astype(f32)


def peer_ffn(xn, w_query, sub_keys, u, v):
    f32 = jnp.float32
    b, l, d = xn.shape
    T = b * l
    H, K = PEER_HEADS, PEER_TOPK
    xt = xn.reshape(T, d).astype(u.dtype)
    q = (xt @ w_query).reshape(T, H, 2, PEER_HALF)
    s = jnp.einsum('thpd,hpkd->thpk', q, sub_keys).astype(f32)
    top_s, top_i = lax.top_k(s, K)
    cand_s = (top_s[:, :, 0, :, None] + top_s[:, :, 1, None, :]).reshape(T, H, K * K)
    cand_i = (top_i[:, :, 0, :, None] * PEER_KEYS + top_i[:, :, 1, None, :]).reshape(T, H, K * K)
    best_s, best_pos = lax.top_k(cand_s, K)
    expert_idx = jnp.take_along_axis(cand_i, best_pos, axis=-1)
    gate = jax.nn.softmax(best_s, axis=-1)
    nb = T // PEER_BLOCK
    xb = xt.reshape(nb, PEER_BLOCK, d)
    ib = expert_idx.reshape(nb, PEER_BLOCK, H * K)
    gb = gate.reshape(nb, PEER_BLOCK, H * K)

    def block(args):
        x_blk, i_blk, g_blk = args
        act = jnp.einsum('td,ted->te', x_blk, u[i_blk]).astype(f32)
        hid = jax.nn.gelu(act, approximate=False) * g_blk
        return jnp.einsum('te,ted->td', hid.astype(v.dtype), v[i_blk])

    out = lax.map(block, (xb, ib, gb))
    return out.reshape(b, l, d)


def setup_inputs(seed: int = 0) -> dict:
    key = jax.random.key(seed)
    ks = jax.random.split(key, 24)
    f32 = jnp.float32
    nrm = lambda k, shape, scale: jax.random.normal(k, shape, f32) * scale
    dt0 = jnp.exp(jax.random.uniform(ks[10], (DEPTH, SSD_HEADS), f32,
                                     math.log(1e-3), math.log(1e-1)))
    return {
        "x": jax.random.normal(ks[0], (BATCH, SEQ, D_MODEL), f32),
        "norm_mix_w": 1.0 + nrm(ks[1], (DEPTH, D_MODEL), 0.02),
        "w_in": nrm(ks[2], (DEPTH, D_MODEL, IN_TOTAL), D_MODEL ** -0.5),
        "gla_w_gate2": nrm(ks[3], (DEPTH, GLA_GATE_RANK, GLA_QK_WIDTH), GLA_GATE_RANK ** -0.5),
        "gla_b_gate": nrm(ks[4], (DEPTH, GLA_QK_WIDTH), 0.1),
        "gla_norm_w": 1.0 + nrm(ks[5], (DEPTH, GLA_DV), 0.02),
        "ssd_conv_w": nrm(ks[6], (DEPTH, SSD_CONV, SSD_CONV_DIM), SSD_CONV ** -0.5),
        "ssd_conv_b": nrm(ks[7], (DEPTH, SSD_CONV_DIM), 0.02),
        "ssd_dt_bias": dt0 + jnp.log(-jnp.expm1(-dt0)),
        "ssd_a_log": jnp.log(jax.random.uniform(ks[8], (DEPTH, SSD_HEADS), f32, 1.0, 16.0)),
        "ssd_d": 1.0 + nrm(ks[9], (DEPTH, SSD_HEADS), 0.1),
        "ssd_norm_w": 1.0 + nrm(ks[11], (DEPTH, SSD_WIDTH), 0.02),
        "w_out": nrm(ks[12], (DEPTH, D_MIX, D_MODEL), D_MIX ** -0.5),
        "norm_ffn_w": 1.0 + nrm(ks[13], (DEPTH, D_MODEL), 0.02),
        "peer_w_query": nrm(ks[14], (DEPTH, D_MODEL, PEER_HEADS * PEER_DQ), D_MODEL ** -0.5),
        "peer_sub_keys": nrm(ks[15], (DEPTH, PEER_HEADS, 2, PEER_KEYS, PEER_HALF), PEER_HALF ** -0.5),
        "peer_u": nrm(ks[16], (DEPTH, PEER_EXPERTS, D_MODEL), D_MODEL ** -0.5),
        "peer_v": nrm(ks[17], (DEPTH, PEER_EXPERTS, D_MODEL), PEER_HEADS ** -0.5),
        "norm_final_w": 1.0 + nrm(ks[18], (D_MODEL,), 0.02),
    }


def reference(x, norm_mix_w, w_in, gla_w_gate2, gla_b_gate, gla_norm_w,
              ssd_conv_w, ssd_conv_b, ssd_dt_bias, ssd_a_log, ssd_d, ssd_norm_w,
              w_out, norm_ffn_w, peer_w_query, peer_sub_keys, peer_u, peer_v,
              norm_final_w):
    h = x
    for i in range(DEPTH):
        n = rmsnorm(h, norm_mix_w[i]).astype(w_in.dtype)
        proj = n @ w_in[i]
        q, k, v, gate_lr, g_out, z, xbc, dt_raw = split_cols(proj)
        o_gla = gla_mixer(q, k, v, gate_lr, g_out, gla_w_gate2[i], gla_b_gate[i], gla_norm_w[i])
        o_ssd = ssd_mixer(z, xbc, dt_raw, ssd_conv_w[i], ssd_conv_b[i], ssd_dt_bias[i],
                          ssd_a_log[i], ssd_d[i], ssd_norm_w[i])
        mixed = jnp.concatenate([o_gla, o_ssd], axis=-1).astype(w_out.dtype)
        h = h + (mixed @ w_out[i]).astype(h.dtype)
        n2 = rmsnorm(h, norm_ffn_w[i])
        h = h + peer_ffn(n2, peer_w_query[i], peer_sub_keys[i], peer_u[i], peer_v[i]).astype(h.dtype)
    return rmsnorm(h, norm_final_w).astype(x.dtype)
```

```python
import functools

import jax
import jax.numpy as jnp
from jax import lax
from jax.experimental import pallas as pl
from jax.experimental.pallas import tpu as pltpu

F32 = jnp.float32
BF16 = jnp.bfloat16
HIGHEST = lax.Precision.HIGHEST

EPS = 1e-6
D_MODEL = 1024

GLA_HEADS = 4
GLA_DK = 64
GLA_DV = 128
GLA_QK = GLA_HEADS * GLA_DK
GLA_V = GLA_HEADS * GLA_DV
GLA_RANK = 16
GLA_NORMALIZER = 16.0
GLA_CHUNK = 64

SSD_HEADS = 8
SSD_P = 64
SSD_WIDTH = SSD_HEADS * SSD_P
SSD_GROUPS = 2
SSD_N = 64
SSD_CONV = 4
SSD_CHUNK = 128
SSD_BC = SSD_GROUPS * SSD_N
SSD_CONV_DIM = SSD_WIDTH + 2 * SSD_BC

PEER_HEADS = 8
PEER_KEYS = 128
PEER_TOPK = 16
PEER_HALF = 128
PEER_ENTRIES = PEER_HEADS * PEER_TOPK

LANES = 128
SUBLANES = 8
VMEM_LIMIT = 56 * 1024 * 1024

_IN_SIZES = (GLA_QK, GLA_QK, GLA_V, GLA_RANK, GLA_V, SSD_WIDTH, SSD_CONV_DIM, SSD_HEADS)
_SLAB_WIDTHS = (GLA_QK, GLA_QK, GLA_V, GLA_V, SSD_WIDTH, SSD_CONV_DIM, LANES, LANES)


def _dot(a, b, dims=((1,), (0,)), precision=None):
    return lax.dot_general(a, b, (dims, ((), ())), precision=precision,
                           preferred_element_type=F32)


def _dot_nt(a, b, precision=None):
    return _dot(a, b, ((1,), (1,)), precision)


def _dot_tn(a, b, precision=None):
    return _dot(a, b, ((0,), (0,)), precision)


def _silu(x):
    return x * (1.0 / (1.0 + jnp.exp(-x)))


def _softplus(x):
    return jnp.maximum(x, 0.0) + jnp.log1p(jnp.exp(-jnp.abs(x)))


def _rms_scale(x):
    return lax.rsqrt(jnp.mean(x * x, axis=-1, keepdims=True) + EPS)


def _in_proj_kernel(x_ref, nw_ref, w_ref, *out_refs):
    x = x_ref[...]
    n = (x * _rms_scale(x) * nw_ref[...]).astype(BF16)
    start = 0
    for o_ref, width in zip(out_refs, _SLAB_WIDTHS):
        o_ref[...] = _dot(n, w_ref[:, start:start + width])
        start += width


def _in_proj(x2, norm_w, w_slabs, block_t):
    t = x2.shape[0]
    total = sum(_SLAB_WIDTHS)
    out_shape = [jax.ShapeDtypeStruct((t, w), F32) for w in _SLAB_WIDTHS]
    out_specs = [pl.BlockSpec((block_t, w), lambda i: (i, 0)) for w in _SLAB_WIDTHS]
    return pl.pallas_call(
        _in_proj_kernel,
        grid=(t // block_t,),
        in_specs=[pl.BlockSpec((block_t, D_MODEL), lambda i: (i, 0)),
                  pl.BlockSpec((1, D_MODEL), lambda i: (0, 0)),
                  pl.BlockSpec((D_MODEL, total), lambda i: (0, 0))],
        out_specs=out_specs,
        out_shape=out_shape,
        compiler_params=pltpu.CompilerParams(
            dimension_semantics=("parallel",), vmem_limit_bytes=VMEM_LIMIT),
        name="in_proj",
    )(x2, norm_w, w_slabs)


def _gla_kernel(q_ref, k_ref, v_ref, glr_ref, gout_ref, w2_ref, bg_ref, nw_ref,
                o_ref, st_ref, *, n_chunks):
    c_len = GLA_CHUNK

    @pl.when(pl.program_id(1) == 0)
    def _():
        st_ref[...] = jnp.zeros_like(st_ref)

    row = lax.broadcasted_iota(jnp.int32, (c_len, c_len), 0)
    col = lax.broadcasted_iota(jnp.int32, (c_len, c_len), 1)
    causal = row >= col
    tri = causal.astype(F32)
    head_of_lane = lax.broadcasted_iota(jnp.int32, (c_len, GLA_QK), 1) // GLA_DK
    block_diag = (lax.broadcasted_iota(jnp.int32, (GLA_V, GLA_QK), 0) // GLA_DV
                  == lax.broadcasted_iota(jnp.int32, (GLA_V, GLA_QK), 1) // GLA_DK)
    w2 = w2_ref[...]
    bg = bg_ref[...]
    nw = nw_ref[...]

    def chunk(c, carry):
        r0 = pl.multiple_of(c * c_len, c_len)
        rows = pl.ds(r0, c_len)
        q = q_ref[rows, :]
        k = k_ref[rows, :]
        v = v_ref[rows, :].astype(BF16)
        gate = _dot(glr_ref[rows, :], w2, precision=HIGHEST) + bg
        log_a = -_softplus(-gate) * (1.0 / GLA_NORMALIZER)
        g_cum = _dot(tri, log_a, precision=HIGHEST)
        g_last = g_cum[c_len - 1:c_len, :]
        q_in = (q * (GLA_DK ** -0.5) * jnp.exp(g_cum)).astype(BF16)
        k_in = k * jnp.exp(-g_cum)
        k_end = (k * jnp.exp(g_last - g_cum)).astype(BF16)

        state_t = st_ref[...]
        o_inter = _dot_nt(q_in, state_t.astype(BF16))
        new_state = _dot_tn(v, k_end)
        st_ref[...] = state_t * jnp.exp(g_last) + jnp.where(block_diag, new_state, 0.0)

        for h in range(GLA_HEADS):
            k_h = jnp.where(head_of_lane == h, k_in, 0.0).astype(BF16)
            att = jnp.where(causal, _dot_nt(q_in, k_h), 0.0)
            lanes = slice(h * GLA_DV, (h + 1) * GLA_DV)
            o_h = _dot(att.astype(BF16), v[:, lanes]) + o_inter[:, lanes]
            o_h = o_h * _rms_scale(o_h) * nw
            o_ref[rows, lanes] = (o_h * _silu(gout_ref[rows, lanes])).astype(o_ref.dtype)
        return carry

    lax.fori_loop(0, n_chunks, chunk, 0)


def _gla(q, k, v, glr, gout, w2p, bg, nw, batch, seq, block_l):
    t = batch * seq
    nblk = seq // block_l
    tok = lambda w: pl.BlockSpec((block_l, w), lambda b, j: (b * nblk + j, 0))
    const = lambda shape: pl.BlockSpec(shape, lambda b, j: (0, 0))
    return pl.pallas_call(
        functools.partial(_gla_kernel, n_chunks=block_l // GLA_CHUNK),
        grid=(batch, nblk),
        in_specs=[tok(GLA_QK), tok(GLA_QK), tok(GLA_V), tok(LANES), tok(GLA_V),
                  const((LANES, GLA_QK)), const((1, GLA_QK)), const((1, GLA_DV))],
        out_specs=tok(GLA_V),
        out_shape=jax.ShapeDtypeStruct((t, GLA_V), BF16),
        scratch_shapes=[pltpu.VMEM((GLA_V, GLA_QK), F32)],
        compiler_params=pltpu.CompilerParams(
            dimension_semantics=("parallel", "arbitrary"), vmem_limit_bytes=VMEM_LIMIT),
        name="gla",
    )(q, k, v, glr, gout, w2p, bg, nw)


def _ssd_kernel(z_ref, xbc_ref, dt_ref, cw_ref, cb_ref, dtb_ref, alog_ref, dskip_ref, nw_ref,
                o_ref, xs_ref, st_ref, *, block_l):
    c_len = SSD_CHUNK
    halo = SUBLANES

    @pl.when(pl.program_id(1) == 0)
    def _():
        xs_ref[0:halo, :] = jnp.zeros((halo, SSD_CONV_DIM), F32)
        st_ref[...] = jnp.zeros_like(st_ref)

    xs_ref[halo:halo + block_l, :] = xbc_ref[...]

    row = lax.broadcasted_iota(jnp.int32, (c_len, c_len), 0)
    col = lax.broadcasted_iota(jnp.int32, (c_len, c_len), 1)
    causal = row >= col
    tri = causal.astype(F32)
    first_half = col < SSD_N
    cw = cw_ref[...]
    cb = cb_ref[...]
    a_neg = -jnp.exp(alog_ref[...])
    dtb = dtb_ref[...]

    def chunk(c, carry):
        r0 = pl.multiple_of(c * c_len, c_len)
        rows = pl.ds(r0, c_len)
        window = xs_ref[pl.ds(r0, c_len + halo), :]
        conv = cb
        for tap in range(SSD_CONV):
            shift = halo - (SSD_CONV - 1) + tap
            conv = conv + cw[tap:tap + 1, :] * window[shift:shift + c_len, :]
        xc = _silu(conv)
        b_m = xc[:, SSD_WIDTH:SSD_WIDTH + SSD_BC]
        c_m = xc[:, SSD_WIDTH + SSD_BC:]

        dt = _softplus(dt_ref[rows, :] + dtb)
        a_cum = _dot(tri, dt * a_neg, precision=HIGHEST)
        a_cum_t = a_cum.T

        scores = []
        c_g = []
        for g in range(SSD_GROUPS):
            in_group = first_half if g == 0 else jnp.logical_not(first_half)
            c_g.append(jnp.where(in_group, c_m, 0.0).astype(BF16))
            scores.append(_dot_nt(c_g[g], b_m.astype(BF16)))

        for m in range(SSD_HEADS // 2):
            g = (2 * m) // (SSD_HEADS // SSD_GROUPS)
            lanes = slice(m * LANES, (m + 1) * LANES)
            x_pair = xc[:, lanes]
            halves = []
            dts = []
            for par in range(2):
                h = 2 * m + par
                dts.append(jnp.broadcast_to(dt[:, h:h + 1], (c_len, LANES)))
            xdt = (x_pair * jnp.where(first_half, dts[0], dts[1])).astype(BF16)
            for par in range(2):
                h = 2 * m + par
                a_col = jnp.broadcast_to(a_cum[:, h:h + 1], (c_len, c_len))
                a_row = jnp.broadcast_to(a_cum_t[h:h + 1, :], (c_len, c_len))
                a_end = a_col[c_len - 1:c_len, :]
                decay = jnp.exp(jnp.where(causal, a_col - a_row, -jnp.inf))
                y_diag = _dot((scores[g] * decay).astype(BF16), xdt)
                state = st_ref[h]
                y_off = _dot(c_g[g], state.astype(BF16)) * jnp.exp(a_col)
                halves.append(y_diag + y_off)
                b_dec = (b_m * jnp.exp(a_end - a_col)).astype(BF16)
                st_ref[h] = state * jnp.exp(a_end) + _dot_tn(b_dec, xdt)
            y = jnp.where(first_half, halves[0], halves[1]) + x_pair * dskip_ref[:, lanes]
            y = y * _silu(z_ref[rows, lanes])
            xs_pair_sq = jnp.sum(y * y, axis=-1, keepdims=True)
            if m % 2 == 0:
                y_prev, sq_prev = y, xs_pair_sq
            else:
                scale = lax.rsqrt((sq_prev + xs_pair_sq) * (1.0 / (2 * LANES)) + EPS)
                lo = slice((m - 1) * LANES, m * LANES)
                o_ref[rows, lo] = (y_prev * scale * nw_ref[:, lo]).astype(o_ref.dtype)
                o_ref[rows, lanes] = (y * scale * nw_ref[:, lanes]).astype(o_ref.dtype)
        return carry

    lax.fori_loop(0, block_l // c_len, chunk, 0)
    xs_ref[0:halo, :] = xs_ref[block_l:block_l + halo, :]


def _ssd(z, xbc, dt, cw, cb, dtb, alog, dskip, nw, batch, seq, block_l):
    t = batch * seq
    nblk = seq // block_l
    tok = lambda w: pl.BlockSpec((block_l, w), lambda b, j: (b * nblk + j, 0))
    const = lambda shape: pl.BlockSpec(shape, lambda b, j: (0, 0))
    return pl.pallas_call(
        functools.partial(_ssd_kernel, block_l=block_l),
        grid=(batch, nblk),
        in_specs=[tok(SSD_WIDTH), tok(SSD_CONV_DIM), tok(LANES),
                  const((SSD_CONV, SSD_CONV_DIM)), const((1, SSD_CONV_DIM)),
                  const((1, LANES)), const((1, LANES)), const((1, SSD_WIDTH)),
                  const((1, SSD_WIDTH))],
        out_specs=tok(SSD_WIDTH),
        out_shape=jax.ShapeDtypeStruct((t, SSD_WIDTH), BF16),
        scratch_shapes=[pltpu.VMEM((block_l + SUBLANES, SSD_CONV_DIM), F32),
                        pltpu.VMEM((SSD_HEADS, SSD_BC, LANES), F32)],
        compiler_params=pltpu.CompilerParams(
            dimension_semantics=("parallel", "arbitrary"), vmem_limit_bytes=VMEM_LIMIT),
        name="ssd",
    )(z, xbc, dt, cw, cb, dtb, alog, dskip, nw)


def _out_query_kernel(og_ref, os_ref, x_ref, wo_ref, nw_ref, wq_ref, keys_ref,
                      h_ref, n_ref, s_ref):
    mixed = _dot(og_ref[...], wo_ref[0:GLA_V, :]) + _dot(os_ref[...], wo_ref[GLA_V:, :])
    h = x_ref[...] + mixed
    h_ref[...] = h
    n = (h * _rms_scale(h) * nw_ref[...]).astype(BF16)
    n_ref[...] = n
    for hp in range(2 * PEER_HEADS):
        lanes = slice(hp * PEER_HALF, (hp + 1) * PEER_HALF)
        q_hp = _dot(n, wq_ref[:, lanes]).astype(BF16)
        s_ref[hp] = _dot_nt(keys_ref[hp], q_hp)


def _out_query(o_gla, o_ssd, x2, w_out, norm_w, w_query, keys, block_t):
    t = x2.shape[0]
    n_hp = 2 * PEER_HEADS
    return pl.pallas_call(
        _out_query_kernel,
        grid=(t // block_t,),
        in_specs=[pl.BlockSpec((block_t, GLA_V), lambda i: (i, 0)),
                  pl.BlockSpec((block_t, SSD_WIDTH), lambda i: (i, 0)),
                  pl.BlockSpec((block_t, D_MODEL), lambda i: (i, 0)),
                  pl.BlockSpec((D_MODEL, D_MODEL), lambda i: (0, 0)),
                  pl.BlockSpec((1, D_MODEL), lambda i: (0, 0)),
                  pl.BlockSpec((D_MODEL, n_hp * PEER_HALF), lambda i: (0, 0)),
                  pl.BlockSpec((n_hp, PEER_KEYS, PEER_HALF), lambda i: (0, 0, 0))],
        out_specs=[pl.BlockSpec((block_t, D_MODEL), lambda i: (i, 0)),
                   pl.BlockSpec((block_t, D_MODEL), lambda i: (i, 0)),
                   pl.BlockSpec((n_hp, PEER_KEYS, block_t), lambda i: (0, 0, i))],
        out_shape=[jax.ShapeDtypeStruct((t, D_MODEL), F32),
                   jax.ShapeDtypeStruct((t, D_MODEL), BF16),
                   jax.ShapeDtypeStruct((n_hp, PEER_KEYS, t), F32)],
        compiler_params=pltpu.CompilerParams(
            dimension_semantics=("parallel",), vmem_limit_bytes=VMEM_LIMIT),
        name="out_query",
    )(o_gla, o_ssd, x2, w_out, norm_w, w_query, keys)


def _extract_top(values, ids, count):
    lanes = values.shape[1]
    slot = lax.broadcasted_iota(jnp.int32, (count, lanes), 0)
    top_v = jnp.zeros((count, lanes), F32)
    top_i = jnp.zeros((count, lanes), F32)
    for r in range(count):
        m = jnp.max(values, axis=0, keepdims=True)
        sel = jnp.min(jnp.where(values == m, ids, jnp.inf), axis=0, keepdims=True)
        values = jnp.where(ids == sel, -jnp.inf, values)
        top_v = jnp.where(slot == r, m, top_v)
        top_i = jnp.where(slot == r, sel, top_i)
    return top_v, top_i


def _gather_rows(table, index):
    row = lax.broadcasted_iota(jnp.int32, table.shape, 0).astype(F32)
    return jnp.sum(jnp.where(row == index, table, 0.0), axis=0, keepdims=True)


def _topk_kernel(s_ref, i_ref, j_ref, g_ref, ei_ref, ej_ref, eg_ref):
    k = PEER_TOPK
    lanes = LANES
    key_id = lax.broadcasted_iota(jnp.int32, (PEER_KEYS, lanes), 0).astype(F32)
    sub = lax.broadcasted_iota(jnp.int32, (SUBLANES, lanes), 0).astype(F32)
    slot = lax.broadcasted_iota(jnp.int32, (k, lanes), 0)

    def head(h, carry):
        s1, i1 = _extract_top(s_ref[2 * h], key_id, k)
        s2, i2 = _extract_top(s_ref[2 * h + 1], key_id, k)
        pieces, flats = [], []
        for b0 in (0, SUBLANES):
            pieces.append(s1[0:1, :] + s2[b0:b0 + SUBLANES, :])
            flats.append(sub + float(b0))
        for a in range(1, SUBLANES):
            limit = k // (a + 1)
            pieces.append(jnp.where(sub < float(limit), s1[a:a + 1, :] + s2[0:SUBLANES, :], -jnp.inf))
            flats.append(sub + float(a * k))
        pieces.append(s1[SUBLANES:k, :] + s2[0:1, :])
        flats.append((sub + float(SUBLANES)) * float(k))
        cand = jnp.concatenate(pieces, axis=0)
        flat = jnp.concatenate(flats, axis=0)
        best_s, best_flat = _extract_top(cand, flat, k)

        ent_i = jnp.zeros((k, lanes), F32)
        ent_j = jnp.zeros((k, lanes), F32)
        for r in range(k):
            pos = best_flat[r:r + 1, :]
            a_idx = jnp.floor(pos * (1.0 / k))
            b_idx = pos - a_idx * float(k)
            ent_i = jnp.where(slot == r, _gather_rows(i1, a_idx), ent_i)
            ent_j = jnp.where(slot == r, _gather_rows(i2, b_idx), ent_j)
        e = jnp.exp(best_s - best_s[0:1, :])
        gate = e / jnp.sum(e, axis=0, keepdims=True)

        rows = pl.ds(pl.multiple_of(h * k, k), k)
        ei_ref[rows, :] = ent_i
        ej_ref[rows, :] = ent_j
        eg_ref[rows, :] = gate
        return carry

    lax.fori_loop(0, PEER_HEADS, head, 0)
    i_ref[...] = ei_ref[...].T
    j_ref[...] = ej_ref[...].T
    g_ref[...] = eg_ref[...].T


def _topk(scores_t):
    n_hp, n_keys, t = scores_t.shape
    ent = jax.ShapeDtypeStruct((t, PEER_ENTRIES), F32)
    ent_spec = pl.BlockSpec((LANES, PEER_ENTRIES), lambda i: (i, 0))
    return pl.pallas_call(
        _topk_kernel,
        grid=(t // LANES,),
        in_specs=[pl.BlockSpec((n_hp, n_keys, LANES), lambda i: (0, 0, i))],
        out_specs=[ent_spec, ent_spec, ent_spec],
        out_shape=[ent, ent, ent],
        scratch_shapes=[pltpu.VMEM((PEER_ENTRIES, LANES), F32)] * 3,
        compiler_params=pltpu.CompilerParams(
            dimension_semantics=("parallel",), vmem_limit_bytes=VMEM_LIMIT),
        name="topk",
    )(scores_t)


W_PITCH = PEER_KEYS + SUBLANES
PAIR = 2 * PEER_KEYS


def _gelu(x):
    return 0.5 * x * (1.0 + lax.erf(x * (2.0 ** -0.5)))


def _peer_kernel(n_ref, i_ref, j_ref, g_ref, u_ref, v_ref, h_ref, nw_ref, o_ref,
                 w_ref, hid_ref, acc_ref, *, block_t, block_e):
    e_step = pl.program_id(1)
    n_pairs = block_e // PAIR

    @pl.when(e_step == 0)
    def _():
        acc_ref[...] = jnp.zeros_like(acc_ref)
        grid_row = lax.broadcasted_iota(jnp.int32, (PEER_KEYS, PEER_ENTRIES), 0).astype(F32)

        def token(t, carry):
            i_row = i_ref[pl.ds(t, 1), :]
            j_row = j_ref[pl.ds(t, 1), :]
            g_row = g_ref[pl.ds(t, 1), :]
            a_t = jnp.where(grid_row == i_row, g_row, 0.0).astype(BF16)
            b_t = jnp.where(grid_row == j_row, 1.0, 0.0).astype(BF16)
            start = pl.multiple_of(t * W_PITCH, SUBLANES)
            w_ref[pl.ds(start, PEER_KEYS), :] = _dot_nt(a_t, b_t)
            return carry

        lax.fori_loop(0, block_t, token, 0)

    n = n_ref[...]
    for p in range(n_pairs):
        experts = slice(p * PAIR, (p + 1) * PAIR)
        act = _dot_nt(n, u_ref[experts, :])
        i0 = e_step * (block_e // PEER_KEYS) + 2 * p
        w_lo = w_ref[pl.ds(i0, block_t, stride=W_PITCH), :]
        w_hi = w_ref[pl.ds(i0 + 1, block_t, stride=W_PITCH), :]
        w_pair = jnp.concatenate([w_lo, w_hi], axis=-1)
        hid_ref[:, experts] = (_gelu(act) * w_pair).astype(BF16)
    acc_ref[...] += _dot(hid_ref[...], v_ref[...])

    @pl.when(e_step == pl.num_programs(1) - 1)
    def _():
        h = h_ref[...] + acc_ref[...]
        o_ref[...] = h * _rms_scale(h) * nw_ref[...]


def _peer(n2, ent_i, ent_j, ent_g, u, v, h1, norm_w, block_t, block_e):
    t = n2.shape[0]
    n_exp = u.shape[0]
    tok = lambda w: pl.BlockSpec((block_t, w), lambda i, e: (i, 0))
    return pl.pallas_call(
        functools.partial(_peer_kernel, block_t=block_t, block_e=block_e),
        grid=(t // block_t, n_exp // block_e),
        in_specs=[tok(D_MODEL), tok(PEER_ENTRIES), tok(PEER_ENTRIES), tok(PEER_ENTRIES),
                  pl.BlockSpec((block_e, D_MODEL), lambda i, e: (e, 0)),
                  pl.BlockSpec((block_e, D_MODEL), lambda i, e: (e, 0)),
                  tok(D_MODEL),
                  pl.BlockSpec((1, D_MODEL), lambda i, e: (0, 0))],
        out_specs=tok(D_MODEL),
        out_shape=jax.ShapeDtypeStruct((t, D_MODEL), F32),
        scratch_shapes=[pltpu.VMEM((block_t * W_PITCH, PEER_KEYS), F32),
                        pltpu.VMEM((block_t, block_e), BF16),
                        pltpu.VMEM((block_t, D_MODEL), F32)],
        compiler_params=pltpu.CompilerParams(
            dimension_semantics=("parallel", "arbitrary"), vmem_limit_bytes=VMEM_LIMIT),
        name="peer",
    )(n2, ent_i, ent_j, ent_g, u, v, h1, norm_w)


def _pad_cols(w, width):
    return jnp.pad(w, ((0, 0), (0, width - w.shape[1])))


def _in_proj_slabs(w_in):
    parts, start = [], 0
    for size in _IN_SIZES:
        parts.append(w_in[:, start:start + size])
        start += size
    q, k, v, gate_lr, g_out, z, xbc, dt = parts
    return jnp.concatenate(
        [q, k, v, g_out, z, xbc, _pad_cols(gate_lr, LANES), _pad_cols(dt, LANES)],
        axis=1).astype(BF16)


def _layer(x2, batch, seq, p):
    row = lambda a: a.reshape(1, -1).astype(F32)
    q, k, v, g_out, z, xbc, gate_lr, dt = _in_proj(
        x2, row(p["norm_mix_w"]), _in_proj_slabs(p["w_in"]), block_t=512)

    w2p = jnp.pad(p["gla_w_gate2"].astype(F32), ((0, LANES - GLA_RANK), (0, 0)))
    o_gla = _gla(q, k, v, gate_lr, g_out, w2p, row(p["gla_b_gate"]), row(p["gla_norm_w"]),
                 batch, seq, block_l=512)

    pad_heads = lambda a: _pad_cols(row(a), LANES)
    d_skip = jnp.repeat(p["ssd_d"].astype(F32), SSD_P).reshape(1, SSD_WIDTH)
    o_ssd = _ssd(z, xbc, dt, p["ssd_conv_w"].astype(F32), row(p["ssd_conv_b"]),
                 pad_heads(p["ssd_dt_bias"]), pad_heads(p["ssd_a_log"]), d_skip,
                 row(p["ssd_norm_w"]), batch, seq, block_l=512)

    keys = p["peer_sub_keys"].reshape(2 * PEER_HEADS, PEER_KEYS, PEER_HALF).astype(BF16)
    h1, n2, scores_t = _out_query(
        o_gla, o_ssd, x2, p["w_out"].astype(BF16), row(p["norm_ffn_w"]),
        p["peer_w_query"].astype(BF16), keys, block_t=256)

    ent_i, ent_j, ent_g = _topk(scores_t)
    return n2, ent_i, ent_j, ent_g, h1


def kernel(x, norm_mix_w, w_in, gla_w_gate2, gla_b_gate, gla_norm_w, ssd_conv_w, ssd_conv_b,
           ssd_dt_bias, ssd_a_log, ssd_d, ssd_norm_w, w_out, norm_ffn_w, peer_w_query,
           peer_sub_keys, peer_u, peer_v, norm_final_w):
    batch, seq, d = x.shape
    assert w_in.shape[0] == 1, "single-layer trunk"
    params = dict(norm_mix_w=norm_mix_w, w_in=w_in, gla_w_gate2=gla_w_gate2, gla_b_gate=gla_b_gate,
                  gla_norm_w=gla_norm_w, ssd_conv_w=ssd_conv_w, ssd_conv_b=ssd_conv_b,
                  ssd_dt_bias=ssd_dt_bias, ssd_a_log=ssd_a_log, ssd_d=ssd_d, ssd_norm_w=ssd_norm_w,
                  w_out=w_out, norm_ffn_w=norm_ffn_w, peer_w_query=peer_w_query,
                  peer_sub_keys=peer_sub_keys)
    p = {name: value[0] for name, value in params.items()}
    x2 = x.reshape(batch * seq, d).astype(F32)
    n2, ent_i, ent_j, ent_g, h1 = _layer(x2, batch, seq, p)
    y = _peer(n2, ent_i, ent_j, ent_g, peer_u[0].astype(BF16), peer_v[0].astype(BF16), h1,
              norm_final_w.reshape(1, d).astype(F32), block_t=256, block_e=2048)
    return y.reshape(batch, seq, d).astype(x.dtype)
```

```python
import functools

import jax
import jax.numpy as jnp
from jax import lax
from jax.experimental import pallas as pl
from jax.experimental.pallas import tpu as pltpu

F32 = jnp.float32
BF16 = jnp.bfloat16
HIGHEST = lax.Precision.HIGHEST

EPS = 1e-6
D_MODEL = 1024

GLA_HEADS = 4
GLA_DK = 64
GLA_DV = 128
GLA_QK = GLA_HEADS * GLA_DK
GLA_V = GLA_HEADS * GLA_DV
GLA_RANK = 16
GLA_NORMALIZER = 16.0
GLA_CHUNK = 64

SSD_HEADS = 8
SSD_P = 64
SSD_WIDTH = SSD_HEADS * SSD_P
SSD_GROUPS = 2
SSD_N = 64
SSD_CONV = 4
SSD_CHUNK = 128
SSD_BC = SSD_GROUPS * SSD_N
SSD_CONV_DIM = SSD_WIDTH + 2 * SSD_BC

PEER_HEADS = 8
PEER_KEYS = 128
PEER_TOPK = 16
PEER_HALF = 128
PEER_ENTRIES = PEER_HEADS * PEER_TOPK

LANES = 128
SUBLANES = 8
VMEM_LIMIT = 56 * 1024 * 1024

_IN_SIZES = (GLA_QK, GLA_QK, GLA_V, GLA_RANK, GLA_V, SSD_WIDTH, SSD_CONV_DIM, SSD_HEADS)
_SLAB_WIDTHS = (GLA_QK, GLA_QK, GLA_V, GLA_V, SSD_WIDTH, SSD_CONV_DIM, LANES, LANES)


def _dot(a, b, dims=((1,), (0,)), precision=None):
    return lax.dot_general(a, b, (dims, ((), ())), precision=precision,
                           preferred_element_type=F32)


def _dot_nt(a, b, precision=None):
    return _dot(a, b, ((1,), (1,)), precision)


def _dot_tn(a, b, precision=None):
    return _dot(a, b, ((0,), (0,)), precision)


def _silu(x):
    return x * (1.0 / (1.0 + jnp.exp(-x)))


def _softplus(x):
    return jnp.maximum(x, 0.0) + jnp.log1p(jnp.exp(-jnp.abs(x)))


def _rms_scale(x):
    return lax.rsqrt(jnp.mean(x * x, axis=-1, keepdims=True) + EPS)


def _in_proj_kernel(x_ref, nw_ref, w_ref, *out_refs):
    x = x_ref[...]
    n = (x * _rms_scale(x) * nw_ref[...]).astype(BF16)
    start = 0
    for o_ref, width in zip(out_refs, _SLAB_WIDTHS):
        o_ref[...] = _dot(n, w_ref[:, start:start + width])
        start += width


def _in_proj(x2, norm_w, w_slabs, block_t):
    t = x2.shape[0]
    total = sum(_SLAB_WIDTHS)
    out_shape = [jax.ShapeDtypeStruct((t, w), F32) for w in _SLAB_WIDTHS]
    out_specs = [pl.BlockSpec((block_t, w), lambda i: (i, 0)) for w in _SLAB_WIDTHS]
    return pl.pallas_call(
        _in_proj_kernel,
        grid=(t // block_t,),
        in_specs=[pl.BlockSpec((block_t, D_MODEL), lambda i: (i, 0)),
                  pl.BlockSpec((1, D_MODEL), lambda i: (0, 0)),
                  pl.BlockSpec((D_MODEL, total), lambda i: (0, 0))],
        out_specs=out_specs,
        out_shape=out_shape,
        compiler_params=pltpu.CompilerParams(
            dimension_semantics=("parallel",), vmem_limit_bytes=VMEM_LIMIT),
        name="in_proj",
    )(x2, norm_w, w_slabs)


def _gla_kernel(q_ref, k_ref, v_ref, glr_ref, gout_ref, w2_ref, bg_ref, nw_ref,
                o_ref, st_ref, *, n_chunks):
    c_len = GLA_CHUNK

    @pl.when(pl.program_id(1) == 0)
    def _():
        st_ref[...] = jnp.zeros_like(st_ref)

    row = lax.broadcasted_iota(jnp.int32, (c_len, c_len), 0)
    col = lax.broadcasted_iota(jnp.int32, (c_len, c_len), 1)
    causal = row >= col
    tri = causal.astype(F32)
    head_of_lane = lax.broadcasted_iota(jnp.int32, (c_len, GLA_QK), 1) // GLA_DK
    block_diag = (lax.broadcasted_iota(jnp.int32, (GLA_V, GLA_QK), 0) // GLA_DV
                  == lax.broadcasted_iota(jnp.int32, (GLA_V, GLA_QK), 1) // GLA_DK)
    w2 = w2_ref[...]
    bg = bg_ref[...]
    nw = nw_ref[...]

    def chunk(c, carry):
        r0 = pl.multiple_of(c * c_len, c_len)
        rows = pl.ds(r0, c_len)
        q = q_ref[rows, :]
        k = k_ref[rows, :]
        v = v_ref[rows, :].astype(BF16)
        gate = _dot(glr_ref[rows, :], w2, precision=HIGHEST) + bg
        log_a = -_softplus(-gate) * (1.0 / GLA_NORMALIZER)
        g_cum = _dot(tri, log_a, precision=HIGHEST)
        g_last = g_cum[c_len - 1:c_len, :]
        q_in = (q * (GLA_DK ** -0.5) * jnp.exp(g_cum)).astype(BF16)
        k_in = k * jnp.exp(-g_cum)
        k_end = (k * jnp.exp(g_last - g_cum)).astype(BF16)

        state_t = st_ref[...]
        o_inter = _dot_nt(q_in, state_t.astype(BF16))
        new_state = _dot_tn(v, k_end)
        st_ref[...] = state_t * jnp.exp(g_last) + jnp.where(block_diag, new_state, 0.0)

        for h in range(GLA_HEADS):
            k_h = jnp.where(head_of_lane == h, k_in, 0.0).astype(BF16)
            att = jnp.where(causal, _dot_nt(q_in, k_h), 0.0)
            lanes = slice(h * GLA_DV, (h + 1) * GLA_DV)
            o_h = _dot(att.astype(BF16), v[:, lanes]) + o_inter[:, lanes]
            o_h = o_h * _rms_scale(o_h) * nw
            o_ref[rows, lanes] = (o_h * _silu(gout_ref[rows, lanes])).astype(o_ref.dtype)
        return carry

    lax.fori_loop(0, n_chunks, chunk, 0)


def _gla(q, k, v, glr, gout, w2p, bg, nw, batch, seq, block_l):
    t = batch * seq
    nblk = seq // block_l
    tok = lambda w: pl.BlockSpec((block_l, w), lambda b, j: (b * nblk + j, 0))
    const = lambda shape: pl.BlockSpec(shape, lambda b, j: (0, 0))
    return pl.pallas_call(
        functools.partial(_gla_kernel, n_chunks=block_l // GLA_CHUNK),
        grid=(batch, nblk),
        in_specs=[tok(GLA_QK), tok(GLA_QK), tok(GLA_V), tok(LANES), tok(GLA_V),
                  const((LANES, GLA_QK)), const((1, GLA_QK)), const((1, GLA_DV))],
        out_specs=tok(GLA_V),
        out_shape=jax.ShapeDtypeStruct((t, GLA_V), BF16),
        scratch_shapes=[pltpu.VMEM((GLA_V, GLA_QK), F32)],
        compiler_params=pltpu.CompilerParams(
            dimension_semantics=("parallel", "arbitrary"), vmem_limit_bytes=VMEM_LIMIT),
        name="gla",
    )(q, k, v, glr, gout, w2p, bg, nw)


def _ssd_kernel(z_ref, xbc_ref, dt_ref, cw_ref, cb_ref, dtb_ref, alog_ref, dskip_ref, nw_ref,
                o_ref, xs_ref, st_ref, *, block_l):
    c_len = SSD_CHUNK
    halo = SUBLANES

    @pl.when(pl.program_id(1) == 0)
    def _():
        xs_ref[0:halo, :] = jnp.zeros((halo, SSD_CONV_DIM), F32)
        st_ref[...] = jnp.zeros_like(st_ref)

    xs_ref[halo:halo + block_l, :] = xbc_ref[...]

    row = lax.broadcasted_iota(jnp.int32, (c_len, c_len), 0)
    col = lax.broadcasted_iota(jnp.int32, (c_len, c_len), 1)
    causal = row >= col
    tri = causal.astype(F32)
    first_half = col < SSD_N
    cw = cw_ref[...]
    cb = cb_ref[...]
    a_neg = -jnp.exp(alog_ref[...])
    dtb = dtb_ref[...]

    def chunk(c, carry):
        r0 = pl.multiple_of(c * c_len, c_len)
        rows = pl.ds(r0, c_len)
        window = xs_ref[pl.ds(r0, c_len + halo), :]
        conv = cb
        for tap in range(SSD_CONV):
            shift = halo - (SSD_CONV - 1) + tap
            conv = conv + cw[tap:tap + 1, :] * window[shift:shift + c_len, :]
        xc = _silu(conv)
        b_m = xc[:, SSD_WIDTH:SSD_WIDTH + SSD_BC]
        c_m = xc[:, SSD_WIDTH + SSD_BC:]

        dt = _softplus(dt_ref[rows, :] + dtb)
        a_cum = _dot(tri, dt * a_neg, precision=HIGHEST)
        a_cum_t = a_cum.T

        scores = []
        c_g = []
        for g in range(SSD_GROUPS):
            in_group = first_half if g == 0 else jnp.logical_not(first_half)
            c_g.append(jnp.where(in_group, c_m, 0.0).astype(BF16))
            scores.append(_dot_nt(c_g[g], b_m.astype(BF16)))

        for m in range(SSD_HEADS // 2):
            g = (2 * m) // (SSD_HEADS // SSD_GROUPS)
            lanes = slice(m * LANES, (m + 1) * LANES)
            x_pair = xc[:, lanes]
            halves = []
            dts = []
            for par in range(2):
                h = 2 * m + par
                dts.append(jnp.broadcast_to(dt[:, h:h + 1], (c_len, LANES)))
            xdt = (x_pair * jnp.where(first_half, dts[0], dts[1])).astype(BF16)
            for par in range(2):
                h = 2 * m + par
                a_col = jnp.broadcast_to(a_cum[:, h:h + 1], (c_len, c_len))
                a_row = jnp.broadcast_to(a_cum_t[h:h + 1, :], (c_len, c_len))
                a_end = a_col[c_len - 1:c_len, :]
                decay = jnp.exp(jnp.where(causal, a_col - a_row, -jnp.inf))
                y_diag = _dot((scores[g] * decay).astype(BF16), xdt)
                state = st_ref[h]
                y_off = _dot(c_g[g], state.astype(BF16)) * jnp.exp(a_col)
                halves.append(y_diag + y_off)
                b_dec = (b_m * jnp.exp(a_end - a_col)).astype(BF16)
                st_ref[h] = state * jnp.exp(a_end) + _dot_tn(b_dec, xdt)
            y = jnp.where(first_half, halves[0], halves[1]) + x_pair * dskip_ref[:, lanes]
            y = y * _silu(z_ref[rows, lanes])
            xs_pair_sq = jnp.sum(y * y, axis=-1, keepdims=True)
            if m % 2 == 0:
                y_prev, sq_prev = y, xs_pair_sq
            else:
                scale = lax.rsqrt((sq_prev + xs_pair_sq) * (1.0 / (2 * LANES)) + EPS)
                lo = slice((m - 1) * LANES, m * LANES)
                o_ref[rows, lo] = (y_prev * scale * nw_ref[:, lo]).astype(o_ref.dtype)
                o_ref[rows, lanes] = (y * scale * nw_ref[:, lanes]).astype(o_ref.dtype)
        return carry

    lax.fori_loop(0, block_l // c_len, chunk, 0)
    xs_ref[0:halo, :] = xs_ref[block_l:block_l + halo, :]


def _ssd(z, xbc, dt, cw, cb, dtb, alog, dskip, nw, batch, seq, block_l):
    t = batch * seq
    nblk = seq // block_l
    tok = lambda w: pl.BlockSpec((block_l, w), lambda b, j: (b * nblk + j, 0))
    const = lambda shape: pl.BlockSpec(shape, lambda b, j: (0, 0))
    return pl.pallas_call(
        functools.partial(_ssd_kernel, block_l=block_l),
        grid=(batch, nblk),
        in_specs=[tok(SSD_WIDTH), tok(SSD_CONV_DIM), tok(LANES),
                  const((SSD_CONV, SSD_CONV_DIM)), const((1, SSD_CONV_DIM)),
                  const((1, LANES)), const((1, LANES)), const((1, SSD_WIDTH)),
                  const((1, SSD_WIDTH))],
        out_specs=tok(SSD_WIDTH),
        out_shape=jax.ShapeDtypeStruct((t, SSD_WIDTH), BF16),
        scratch_shapes=[pltpu.VMEM((block_l + SUBLANES, SSD_CONV_DIM), F32),
                        pltpu.VMEM((SSD_HEADS, SSD_BC, LANES), F32)],
        compiler_params=pltpu.CompilerParams(
            dimension_semantics=("parallel", "arbitrary"), vmem_limit_bytes=VMEM_LIMIT),
        name="ssd",
    )(z, xbc, dt, cw, cb, dtb, alog, dskip, nw)


def _out_query_kernel(og_ref, os_ref, x_ref, wo_ref, nw_ref, wq_ref, keys_ref,
                      h_ref, n_ref, s_ref):
    mixed = _dot(og_ref[...], wo_ref[0:GLA_V, :]) + _dot(os_ref[...], wo_ref[GLA_V:, :])
    h = x_ref[...] + mixed
    h_ref[...] = h
    n = (h * _rms_scale(h) * nw_ref[...]).astype(BF16)
    n_ref[...] = n
    for hp in range(2 * PEER_HEADS):
        lanes = slice(hp * PEER_HALF, (hp + 1) * PEER_HALF)
        q_hp = _dot(n, wq_ref[:, lanes]).astype(BF16)
        s_ref[hp] = _dot_nt(keys_ref[hp], q_hp)


def _out_query(o_gla, o_ssd, x2, w_out, norm_w, w_query, keys, block_t):
    t = x2.shape[0]
    n_hp = 2 * PEER_HEADS
    return pl.pallas_call(
        _out_query_kernel,
        grid=(t // block_t,),
        in_specs=[pl.BlockSpec((block_t, GLA_V), lambda i: (i, 0)),
                  pl.BlockSpec((block_t, SSD_WIDTH), lambda i: (i, 0)),
                  pl.BlockSpec((block_t, D_MODEL), lambda i: (i, 0)),
                  pl.BlockSpec((D_MODEL, D_MODEL), lambda i: (0, 0)),
                  pl.BlockSpec((1, D_MODEL), lambda i: (0, 0)),
                  pl.BlockSpec((D_MODEL, n_hp * PEER_HALF), lambda i: (0, 0)),
                  pl.BlockSpec((n_hp, PEER_KEYS, PEER_HALF), lambda i: (0, 0, 0))],
        out_specs=[pl.BlockSpec((block_t, D_MODEL), lambda i: (i, 0)),
                   pl.BlockSpec((block_t, D_MODEL), lambda i: (i, 0)),
                   pl.BlockSpec((n_hp, PEER_KEYS, block_t), lambda i: (0, 0, i))],
        out_shape=[jax.ShapeDtypeStruct((t, D_MODEL), F32),
                   jax.ShapeDtypeStruct((t, D_MODEL), BF16),
                   jax.ShapeDtypeStruct((n_hp, PEER_KEYS, t), F32)],
        compiler_params=pltpu.CompilerParams(
            dimension_semantics=("parallel",), vmem_limit_bytes=VMEM_LIMIT),
        name="out_query",
    )(o_gla, o_ssd, x2, w_out, norm_w, w_query, keys)


def _extract_top(values, ids, count):
    lanes = values.shape[1]
    slot = lax.broadcasted_iota(jnp.int32, (count, lanes), 0)
    top_v = jnp.zeros((count, lanes), F32)
    top_i = jnp.zeros((count, lanes), F32)
    for r in range(count):
        m = jnp.max(values, axis=0, keepdims=True)
        sel = jnp.min(jnp.where(values == m, ids, jnp.inf), axis=0, keepdims=True)
        values = jnp.where(ids == sel, -jnp.inf, values)
        top_v = jnp.where(slot == r, m, top_v)
        top_i = jnp.where(slot == r, sel, top_i)
    return top_v, top_i


def _gather_rows(table, index):
    row = lax.broadcasted_iota(jnp.int32, table.shape, 0).astype(F32)
    return jnp.sum(jnp.where(row == index, table, 0.0), axis=0, keepdims=True)


def _topk_kernel(s_ref, i_ref, j_ref, g_ref, ei_ref, ej_ref, eg_ref):
    k = PEER_TOPK
    lanes = LANES
    key_id = lax.broadcasted_iota(jnp.int32, (PEER_KEYS, lanes), 0).astype(F32)
    sub = lax.broadcasted_iota(jnp.int32, (SUBLANES, lanes), 0).astype(F32)
    slot = lax.broadcasted_iota(jnp.int32, (k, lanes), 0)

    def head(h, carry):
        s1, i1 = _extract_top(s_ref[2 * h], key_id, k)
        s2, i2 = _extract_top(s_ref[2 * h + 1], key_id, k)
        pieces, flats = [], []
        for b0 in (0, SUBLANES):
            pieces.append(s1[0:1, :] + s2[b0:b0 + SUBLANES, :])
            flats.append(sub + float(b0))
        for a in range(1, SUBLANES):
            limit = k // (a + 1)
            pieces.append(jnp.where(sub < float(limit), s1[a:a + 1, :] + s2[0:SUBLANES, :], -jnp.inf))
            flats.append(sub + float(a * k))
        pieces.append(s1[SUBLANES:k, :] + s2[0:1, :])
        flats.append((sub + float(SUBLANES)) * float(k))
        cand = jnp.concatenate(pieces, axis=0)
        flat = jnp.concatenate(flats, axis=0)
        best_s, best_flat = _extract_top(cand, flat, k)

        ent_i = jnp.zeros((k, lanes), F32)
        ent_j = jnp.zeros((k, lanes), F32)
        for r in range(k):
            pos = best_flat[r:r + 1, :]
            a_idx = jnp.floor(pos * (1.0 / k))
            b_idx = pos - a_idx * float(k)
            ent_i = jnp.where(slot == r, _gather_rows(i1, a_idx), ent_i)
            ent_j = jnp.where(slot == r, _gather_rows(i2, b_idx), ent_j)
        e = jnp.exp(best_s - best_s[0:1, :])
        gate = e / jnp.sum(e, axis=0, keepdims=True)

        rows = pl.ds(pl.multiple_of(h * k, k), k)
        ei_ref[rows, :] = ent_i
        ej_ref[rows, :] = ent_j
        eg_ref[rows, :] = gate
        return carry

    lax.fori_loop(0, PEER_HEADS, head, 0)
    i_ref[...] = ei_ref[...].T
    j_ref[...] = ej_ref[...].T
    g_ref[...] = eg_ref[...].T


def _topk(scores_t):
    n_hp, n_keys, t = scores_t.shape
    ent = jax.ShapeDtypeStruct((t, PEER_ENTRIES), F32)
    ent_spec = pl.BlockSpec((LANES, PEER_ENTRIES), lambda i: (i, 0))
    return pl.pallas_call(
        _topk_kernel,
        grid=(t // LANES,),
        in_specs=[pl.BlockSpec((n_hp, n_keys, LANES), lambda i: (0, 0, i))],
        out_specs=[ent_spec, ent_spec, ent_spec],
        out_shape=[ent, ent, ent],
        scratch_shapes=[pltpu.VMEM((PEER_ENTRIES, LANES), F32)] * 3,
        compiler_params=pltpu.CompilerParams(
            dimension_semantics=("parallel",), vmem_limit_bytes=VMEM_LIMIT),
        name="topk",
    )(scores_t)


W_PITCH = PEER_KEYS + SUBLANES
PAIR = 2 * PEER_KEYS
TOKEN_UNROLL = 16


def _gelu(x):
    return 0.5 * x * (1.0 + lax.erf(x * (2.0 ** -0.5)))


def _peer_kernel(n_ref, i_ref, j_ref, g_ref, u_ref, v_ref, h_ref, nw_ref, o_ref,
                 w_ref, hid_ref, acc_ref, *, block_t, block_e):
    e_step = pl.program_id(1)
    n_pairs = block_e // PAIR

    @pl.when(e_step == 0)
    def _():
        acc_ref[...] = jnp.zeros_like(acc_ref)
        grid_row = lax.broadcasted_iota(jnp.int32, (PEER_KEYS, PEER_ENTRIES), 0).astype(F32)

        def token(t, carry):
            i_row = i_ref[pl.ds(t, 1), :]
            j_row = j_ref[pl.ds(t, 1), :]
            g_row = g_ref[pl.ds(t, 1), :]
            a_t = jnp.where(grid_row == i_row, g_row, 0.0).astype(BF16)
            b_t = jnp.where(grid_row == j_row, 1.0, 0.0).astype(BF16)
            start = pl.multiple_of(t * W_PITCH, SUBLANES)
            w_ref[pl.ds(start, PEER_KEYS), :] = _dot_nt(a_t, b_t)
            return carry

        lax.fori_loop(0, block_t, token, 0, unroll=TOKEN_UNROLL)

    n = n_ref[...]
    for p in range(n_pairs):
        experts = slice(p * PAIR, (p + 1) * PAIR)
        act = _dot_nt(n, u_ref[experts, :])
        i0 = e_step * (block_e // PEER_KEYS) + 2 * p
        w_lo = w_ref[pl.ds(i0, block_t, stride=W_PITCH), :]
        w_hi = w_ref[pl.ds(i0 + 1, block_t, stride=W_PITCH), :]
        w_pair = jnp.concatenate([w_lo, w_hi], axis=-1)
        hid_ref[:, experts] = (_gelu(act) * w_pair).astype(BF16)
    acc_ref[...] += _dot(hid_ref[...], v_ref[...])

    @pl.when(e_step == pl.num_programs(1) - 1)
    def _():
        h = h_ref[...] + acc_ref[...]
        o_ref[...] = h * _rms_scale(h) * nw_ref[...]


def _peer(n2, ent_i, ent_j, ent_g, u, v, h1, norm_w, block_t, block_e):
    t = n2.shape[0]
    n_exp = u.shape[0]
    tok = lambda w: pl.BlockSpec((block_t, w), lambda i, e: (i, 0))
    return pl.pallas_call(
        functools.partial(_peer_kernel, block_t=block_t, block_e=block_e),
        grid=(t // block_t, n_exp // block_e),
        in_specs=[tok(D_MODEL), tok(PEER_ENTRIES), tok(PEER_ENTRIES), tok(PEER_ENTRIES),
                  pl.BlockSpec((block_e, D_MODEL), lambda i, e: (e, 0)),
                  pl.BlockSpec((block_e, D_MODEL), lambda i, e: (e, 0)),
                  tok(D_MODEL),
                  pl.BlockSpec((1, D_MODEL), lambda i, e: (0, 0))],
        out_specs=tok(D_MODEL),
        out_shape=jax.ShapeDtypeStruct((t, D_MODEL), F32),
        scratch_shapes=[pltpu.VMEM((block_t * W_PITCH, PEER_KEYS), F32),
                        pltpu.VMEM((block_t, block_e), BF16),
                        pltpu.VMEM((block_t, D_MODEL), F32)],
        compiler_params=pltpu.CompilerParams(
            dimension_semantics=("parallel", "arbitrary"), vmem_limit_bytes=VMEM_LIMIT),
        name="peer",
    )(n2, ent_i, ent_j, ent_g, u, v, h1, norm_w)


def _pad_cols(w, width):
    return jnp.pad(w, ((0, 0), (0, width - w.shape[1])))


def _in_proj_slabs(w_in):
    parts, start = [], 0
    for size in _IN_SIZES:
        parts.append(w_in[:, start:start + size])
        start += size
    q, k, v, gate_lr, g_out, z, xbc, dt = parts
    return jnp.concatenate(
        [q, k, v, g_out, z, xbc, _pad_cols(gate_lr, LANES), _pad_cols(dt, LANES)],
        axis=1).astype(BF16)


def _layer(x2, batch, seq, p):
    row = lambda a: a.reshape(1, -1).astype(F32)
    q, k, v, g_out, z, xbc, gate_lr, dt = _in_proj(
        x2, row(p["norm_mix_w"]), _in_proj_slabs(p["w_in"]), block_t=512)

    w2p = jnp.pad(p["gla_w_gate2"].astype(F32), ((0, LANES - GLA_RANK), (0, 0)))
    o_gla = _gla(q, k, v, gate_lr, g_out, w2p, row(p["gla_b_gate"]), row(p["gla_norm_w"]),
                 batch, seq, block_l=512)

    pad_heads = lambda a: _pad_cols(row(a), LANES)
    d_skip = jnp.repeat(p["ssd_d"].astype(F32), SSD_P).reshape(1, SSD_WIDTH)
    o_ssd = _ssd(z, xbc, dt, p["ssd_conv_w"].astype(F32), row(p["ssd_conv_b"]),
                 pad_heads(p["ssd_dt_bias"]), pad_heads(p["ssd_a_log"]), d_skip,
                 row(p["ssd_norm_w"]), batch, seq, block_l=512)

    keys = p["peer_sub_keys"].reshape(2 * PEER_HEADS, PEER_KEYS, PEER_HALF).astype(BF16)
    h1, n2, scores_t = _out_query(
        o_gla, o_ssd, x2, p["w_out"].astype(BF16), row(p["norm_ffn_w"]),
        p["peer_w_query"].astype(BF16), keys, block_t=256)

    ent_i, ent_j, ent_g = _topk(scores_t)
    return n2, ent_i, ent_j, ent_g, h1


def kernel(x, norm_mix_w, w_in, gla_w_gate2, gla_b_gate, gla_norm_w, ssd_conv_w, ssd_conv_b,
           ssd_dt_bias, ssd_a_log, ssd_d, ssd_norm_w, w_out, norm_ffn_w, peer_w_query,
           peer_sub_keys, peer_u, peer_v, norm_final_w):
    batch, seq, d = x.shape
    assert w_in.shape[0] == 1, "single-layer trunk"
    params = dict(norm_mix_w=norm_mix_w, w_in=w_in, gla_w_gate2=gla_w_gate2, gla_b_gate=gla_b_gate,
                  gla_norm_w=gla_norm_w, ssd_conv_w=ssd_conv_w, ssd_conv_b=ssd_conv_b,
                  ssd_dt_bias=ssd_dt_bias, ssd_a_log=ssd_a_log, ssd_d=ssd_d, ssd_norm_w=ssd_norm_w,
                  w_out=w_out, norm_ffn_w=norm_ffn_w, peer_w_query=peer_w_query,
                  peer_sub_keys=peer_sub_keys)
    p = {name: value[0] for name, value in params.items()}
    x2 = x.reshape(batch * seq, d).astype(F32)
    n2, ent_i, ent_j, ent_g, h1 = _layer(x2, batch, seq, p)
    y = _peer(n2, ent_i, ent_j, ent_g, peer_u[0].astype(BF16), peer_v[0].astype(BF16), h1,
              norm_final_w.reshape(1, d).astype(F32), block_t=256, block_e=2048)
    return y.reshape(batch, seq, d).astype(x.dtype)
```

```python
import functools

import jax
import jax.numpy as jnp
from jax import lax
from jax.experimental import pallas as pl
from jax.experimental.pallas import tpu as pltpu

F32 = jnp.float32
BF16 = jnp.bfloat16
HIGHEST = lax.Precision.HIGHEST

EPS = 1e-6
D_MODEL = 1024

GLA_HEADS = 4
GLA_DK = 64
GLA_DV = 128
GLA_QK = GLA_HEADS * GLA_DK
GLA_V = GLA_HEADS * GLA_DV
GLA_RANK = 16
GLA_NORMALIZER = 16.0
GLA_CHUNK = 64

SSD_HEADS = 8
SSD_P = 64
SSD_WIDTH = SSD_HEADS * SSD_P
SSD_GROUPS = 2
SSD_N = 64
SSD_CONV = 4
SSD_CHUNK = 128
SSD_BC = SSD_GROUPS * SSD_N
SSD_CONV_DIM = SSD_WIDTH + 2 * SSD_BC

PEER_HEADS = 8
PEER_KEYS = 128
PEER_TOPK = 16
PEER_HALF = 128
PEER_ENTRIES = PEER_HEADS * PEER_TOPK

LANES = 128
SUBLANES = 8
VMEM_LIMIT = 56 * 1024 * 1024

_IN_SIZES = (GLA_QK, GLA_QK, GLA_V, GLA_RANK, GLA_V, SSD_WIDTH, SSD_CONV_DIM, SSD_HEADS)
_SLAB_WIDTHS = (GLA_QK, GLA_QK, GLA_V, GLA_V, SSD_WIDTH, SSD_CONV_DIM, LANES, LANES)


def _dot(a, b, dims=((1,), (0,)), precision=None):
    return lax.dot_general(a, b, (dims, ((), ())), precision=precision,
                           preferred_element_type=F32)


def _dot_nt(a, b, precision=None):
    return _dot(a, b, ((1,), (1,)), precision)


def _dot_tn(a, b, precision=None):
    return _dot(a, b, ((0,), (0,)), precision)


def _silu(x):
    return x * (1.0 / (1.0 + jnp.exp(-x)))


def _softplus(x):
    return jnp.maximum(x, 0.0) + jnp.log1p(jnp.exp(-jnp.abs(x)))


def _rms_scale(x):
    return lax.rsqrt(jnp.mean(x * x, axis=-1, keepdims=True) + EPS)


def _in_proj_kernel(x_ref, nw_ref, w_ref, *out_refs):
    x = x_ref[...]
    n = (x * _rms_scale(x) * nw_ref[...]).astype(BF16)
    start = 0
    for o_ref, width in zip(out_refs, _SLAB_WIDTHS):
        o_ref[...] = _dot(n, w_ref[:, start:start + width])
        start += width


def _in_proj(x2, norm_w, w_slabs, block_t):
    t = x2.shape[0]
    total = sum(_SLAB_WIDTHS)
    out_shape = [jax.ShapeDtypeStruct((t, w), F32) for w in _SLAB_WIDTHS]
    out_specs = [pl.BlockSpec((block_t, w), lambda i: (i, 0)) for w in _SLAB_WIDTHS]
    return pl.pallas_call(
        _in_proj_kernel,
        grid=(t // block_t,),
        in_specs=[pl.BlockSpec((block_t, D_MODEL), lambda i: (i, 0)),
                  pl.BlockSpec((1, D_MODEL), lambda i: (0, 0)),
                  pl.BlockSpec((D_MODEL, total), lambda i: (0, 0))],
        out_specs=out_specs,
        out_shape=out_shape,
        compiler_params=pltpu.CompilerParams(
            dimension_semantics=("parallel",), vmem_limit_bytes=VMEM_LIMIT),
        name="in_proj",
    )(x2, norm_w, w_slabs)


def _gla_kernel(q_ref, k_ref, v_ref, glr_ref, gout_ref, w2_ref, bg_ref, nw_ref,
                o_ref, st_ref, *, n_chunks):
    c_len = GLA_CHUNK

    @pl.when(pl.program_id(1) == 0)
    def _():
        st_ref[...] = jnp.zeros_like(st_ref)

    def iota(shape, dim):
        return lax.broadcasted_iota(jnp.int32, shape, dim)

    block_l = n_chunks * c_len
    gate = _dot(glr_ref[...], w2_ref[...], precision=HIGHEST) + bg_ref[...]
    log_a = -_softplus(-gate) * (1.0 / GLA_NORMALIZER)
    tri = ((iota((block_l, block_l), 0) // c_len == iota((block_l, block_l), 1) // c_len)
           & (iota((block_l, block_l), 0) >= iota((block_l, block_l), 1))).astype(F32)
    g_cum = _dot(tri, log_a, precision=HIGHEST)
    g_end = [g_cum[(c + 1) * c_len - 1:(c + 1) * c_len, :] for c in range(n_chunks)]
    g_last = jnp.concatenate([jnp.broadcast_to(g, (c_len, GLA_QK)) for g in g_end], axis=0)
    k = k_ref[...]
    q_in = (q_ref[...] * (GLA_DK ** -0.5) * jnp.exp(g_cum)).astype(BF16)
    k_in = k * jnp.exp(-g_cum)
    k_end = (k * jnp.exp(g_last - g_cum)).astype(BF16)
    v = v_ref[...]

    k_bd_mask = iota((GLA_QK, GLA_QK), 0) // c_len == iota((GLA_QK, GLA_QK), 1) // GLA_DK
    v_bd_mask = iota((GLA_QK, GLA_V), 0) // c_len == iota((GLA_QK, GLA_V), 1) // GLA_DV
    causal = iota((c_len, GLA_QK), 0) >= iota((c_len, GLA_QK), 1) % c_len
    state_mask = iota((GLA_V, GLA_QK), 0) // GLA_DV == iota((GLA_V, GLA_QK), 1) // GLA_DK

    att = []
    for c in range(n_chunks):
        rows = slice(c * c_len, (c + 1) * c_len)
        k_bd = jnp.where(k_bd_mask, jnp.concatenate([k_in[rows]] * GLA_HEADS, axis=0), 0.0)
        scores = _dot_nt(q_in[rows], k_bd.astype(BF16))
        att.append(jnp.where(causal, scores, 0.0).astype(BF16))

    state_t = st_ref[...]
    nw = jnp.concatenate([nw_ref[...]] * GLA_HEADS, axis=1)
    for c in range(n_chunks):
        rows = slice(c * c_len, (c + 1) * c_len)
        v_c = v[rows]
        new_state = _dot_tn(v_c.astype(BF16), k_end[rows])
        o_inter = _dot_nt(q_in[rows], state_t.astype(BF16))
        state_t = state_t * jnp.exp(g_end[c]) + jnp.where(state_mask, new_state, 0.0)
        v_bd = jnp.where(v_bd_mask, jnp.concatenate([v_c] * GLA_HEADS, axis=0), 0.0)
        o = _dot(att[c], v_bd.astype(BF16)) + o_inter
        outs = []
        for h in range(GLA_HEADS):
            o_h = o[:, h * GLA_DV:(h + 1) * GLA_DV]
            outs.append(o_h * _rms_scale(o_h))
        o = jnp.concatenate(outs, axis=1) * nw
        o_ref[rows, :] = (o * _silu(gout_ref[rows, :])).astype(o_ref.dtype)
    st_ref[...] = state_t


def _gla(q, k, v, glr, gout, w2p, bg, nw, batch, seq, block_l):
    t = batch * seq
    nblk = seq // block_l
    tok = lambda w: pl.BlockSpec((block_l, w), lambda b, j: (b * nblk + j, 0))
    const = lambda shape: pl.BlockSpec(shape, lambda b, j: (0, 0))
    return pl.pallas_call(
        functools.partial(_gla_kernel, n_chunks=block_l // GLA_CHUNK),
        grid=(batch, nblk),
        in_specs=[tok(GLA_QK), tok(GLA_QK), tok(GLA_V), tok(LANES), tok(GLA_V),
                  const((LANES, GLA_QK)), const((1, GLA_QK)), const((1, GLA_DV))],
        out_specs=tok(GLA_V),
        out_shape=jax.ShapeDtypeStruct((t, GLA_V), BF16),
        scratch_shapes=[pltpu.VMEM((GLA_V, GLA_QK), F32)],
        compiler_params=pltpu.CompilerParams(
            dimension_semantics=("parallel", "arbitrary"), vmem_limit_bytes=VMEM_LIMIT),
        name="gla",
    )(q, k, v, glr, gout, w2p, bg, nw)


def _ssd_kernel(z_ref, xbc_ref, dt_ref, cw_ref, cb_ref, dtb_ref, alog_ref, dskip_ref, nw_ref,
                o_ref, xs_ref, st_ref, *, block_l):
    c_len = SSD_CHUNK
    halo = SUBLANES

    @pl.when(pl.program_id(1) == 0)
    def _():
        xs_ref[0:halo, :] = jnp.zeros((halo, SSD_CONV_DIM), F32)
        st_ref[...] = jnp.zeros_like(st_ref)

    xs_ref[halo:halo + block_l, :] = xbc_ref[...]

    row = lax.broadcasted_iota(jnp.int32, (c_len, c_len), 0)
    col = lax.broadcasted_iota(jnp.int32, (c_len, c_len), 1)
    causal = row >= col
    tri = causal.astype(F32)
    first_half = col < SSD_N
    cw = cw_ref[...]
    cb = cb_ref[...]
    a_neg = -jnp.exp(alog_ref[...])
    dtb = dtb_ref[...]

    def chunk(c, carry):
        r0 = pl.multiple_of(c * c_len, c_len)
        rows = pl.ds(r0, c_len)
        window = xs_ref[pl.ds(r0, c_len + halo), :]
        conv = cb
        for tap in range(SSD_CONV):
            shift = halo - (SSD_CONV - 1) + tap
            conv = conv + cw[tap:tap + 1, :] * window[shift:shift + c_len, :]
        xc = _silu(conv)
        b_m = xc[:, SSD_WIDTH:SSD_WIDTH + SSD_BC]
        c_m = xc[:, SSD_WIDTH + SSD_BC:]

        dt = _softplus(dt_ref[rows, :] + dtb)
        a_cum = _dot(tri, dt * a_neg, precision=HIGHEST)
        a_cum_t = a_cum.T

        scores = []
        c_g = []
        for g in range(SSD_GROUPS):
            in_group = first_half if g == 0 else jnp.logical_not(first_half)
            c_g.append(jnp.where(in_group, c_m, 0.0).astype(BF16))
            scores.append(_dot_nt(c_g[g], b_m.astype(BF16)))

        for m in range(SSD_HEADS // 2):
            g = (2 * m) // (SSD_HEADS // SSD_GROUPS)
            lanes = slice(m * LANES, (m + 1) * LANES)
            x_pair = xc[:, lanes]
            halves = []
            dts = []
            for par in range(2):
                h = 2 * m + par
                dts.append(jnp.broadcast_to(dt[:, h:h + 1], (c_len, LANES)))
            xdt = (x_pair * jnp.where(first_half, dts[0], dts[1])).astype(BF16)
            for par in range(2):
                h = 2 * m + par
                a_col = jnp.broadcast_to(a_cum[:, h:h + 1], (c_len, c_len))
                a_row = jnp.broadcast_to(a_cum_t[h:h + 1, :], (c_len, c_len))
                a_end = a_col[c_len - 1:c_len, :]
                decay = jnp.exp(jnp.where(causal, a_col - a_row, -jnp.inf))
                y_diag = _dot((scores[g] * decay).astype(BF16), xdt)
                state = st_ref[h]
                y_off = _dot(c_g[g], state.astype(BF16)) * jnp.exp(a_col)
                halves.append(y_diag + y_off)
                b_dec = (b_m * jnp.exp(a_end - a_col)).astype(BF16)
                st_ref[h] = state * jnp.exp(a_end) + _dot_tn(b_dec, xdt)
            y = jnp.where(first_half, halves[0], halves[1]) + x_pair * dskip_ref[:, lanes]
            y = y * _silu(z_ref[rows, lanes])
            xs_pair_sq = jnp.sum(y * y, axis=-1, keepdims=True)
            if m % 2 == 0:
                y_prev, sq_prev = y, xs_pair_sq
            else:
                scale = lax.rsqrt((sq_prev + xs_pair_sq) * (1.0 / (2 * LANES)) + EPS)
                lo = slice((m - 1) * LANES, m * LANES)
                o_ref[rows, lo] = (y_prev * scale * nw_ref[:, lo]).astype(o_ref.dtype)
                o_ref[rows, lanes] = (y * scale * nw_ref[:, lanes]).astype(o_ref.dtype)
        return carry

    lax.fori_loop(0, block_l // c_len, chunk, 0)
    xs_ref[0:halo, :] = xs_ref[block_l:block_l + halo, :]


def _ssd(z, xbc, dt, cw, cb, dtb, alog, dskip, nw, batch, seq, block_l):
    t = batch * seq
    nblk = seq // block_l
    tok = lambda w: pl.BlockSpec((block_l, w), lambda b, j: (b * nblk + j, 0))
    const = lambda shape: pl.BlockSpec(shape, lambda b, j: (0, 0))
    return pl.pallas_call(
        functools.partial(_ssd_kernel, block_l=block_l),
        grid=(batch, nblk),
        in_specs=[tok(SSD_WIDTH), tok(SSD_CONV_DIM), tok(LANES),
                  const((SSD_CONV, SSD_CONV_DIM)), const((1, SSD_CONV_DIM)),
                  const((1, LANES)), const((1, LANES)), const((1, SSD_WIDTH)),
                  const((1, SSD_WIDTH))],
        out_specs=tok(SSD_WIDTH),
        out_shape=jax.ShapeDtypeStruct((t, SSD_WIDTH), BF16),
        scratch_shapes=[pltpu.VMEM((block_l + SUBLANES, SSD_CONV_DIM), F32),
                        pltpu.VMEM((SSD_HEADS, SSD_BC, LANES), F32)],
        compiler_params=pltpu.CompilerParams(
            dimension_semantics=("parallel", "arbitrary"), vmem_limit_bytes=VMEM_LIMIT),
        name="ssd",
    )(z, xbc, dt, cw, cb, dtb, alog, dskip, nw)


def _out_query_kernel(og_ref, os_ref, x_ref, wo_ref, nw_ref, wq_ref, keys_ref,
                      h_ref, n_ref, s_ref, q_ref):
    mixed = _dot(og_ref[...], wo_ref[0:GLA_V, :]) + _dot(os_ref[...], wo_ref[GLA_V:, :])
    h = x_ref[...] + mixed
    h_ref[...] = h
    n = (h * _rms_scale(h) * nw_ref[...]).astype(BF16)
    n_ref[...] = n
    q_ref[...] = _dot(n, wq_ref[...]).astype(BF16)
    for hp in range(2 * PEER_HEADS):
        lanes = slice(hp * PEER_HALF, (hp + 1) * PEER_HALF)
        s_ref[hp] = _dot_nt(keys_ref[hp], q_ref[:, lanes])


def _out_query(o_gla, o_ssd, x2, w_out, norm_w, w_query, keys, block_t):
    t = x2.shape[0]
    n_hp = 2 * PEER_HEADS
    return pl.pallas_call(
        _out_query_kernel,
        grid=(t // block_t,),
        in_specs=[pl.BlockSpec((block_t, GLA_V), lambda i: (i, 0)),
                  pl.BlockSpec((block_t, SSD_WIDTH), lambda i: (i, 0)),
                  pl.BlockSpec((block_t, D_MODEL), lambda i: (i, 0)),
                  pl.BlockSpec((D_MODEL, D_MODEL), lambda i: (0, 0)),
                  pl.BlockSpec((1, D_MODEL), lambda i: (0, 0)),
                  pl.BlockSpec((D_MODEL, n_hp * PEER_HALF), lambda i: (0, 0)),
                  pl.BlockSpec((n_hp, PEER_KEYS, PEER_HALF), lambda i: (0, 0, 0))],
        out_specs=[pl.BlockSpec((block_t, D_MODEL), lambda i: (i, 0)),
                   pl.BlockSpec((block_t, D_MODEL), lambda i: (i, 0)),
                   pl.BlockSpec((n_hp, PEER_KEYS, block_t), lambda i: (0, 0, i))],
        out_shape=[jax.ShapeDtypeStruct((t, D_MODEL), F32),
                   jax.ShapeDtypeStruct((t, D_MODEL), BF16),
                   jax.ShapeDtypeStruct((n_hp, PEER_KEYS, t), F32)],
        scratch_shapes=[pltpu.VMEM((block_t, n_hp * PEER_HALF), BF16)],
        compiler_params=pltpu.CompilerParams(
            dimension_semantics=("parallel",), vmem_limit_bytes=VMEM_LIMIT),
        name="out_query",
    )(o_gla, o_ssd, x2, w_out, norm_w, w_query, keys)


def _extract_top(values, ids, count):
    lanes = values.shape[1]
    slot = lax.broadcasted_iota(jnp.int32, (count, lanes), 0)
    top_v = jnp.zeros((count, lanes), F32)
    top_i = jnp.zeros((count, lanes), F32)
    for r in range(count):
        m = jnp.max(values, axis=0, keepdims=True)
        sel = jnp.min(jnp.where(values == m, ids, jnp.inf), axis=0, keepdims=True)
        values = jnp.where(ids == sel, -jnp.inf, values)
        top_v = jnp.where(slot == r, m, top_v)
        top_i = jnp.where(slot == r, sel, top_i)
    return top_v, top_i


def _gather_rows(table, index):
    row = lax.broadcasted_iota(jnp.int32, table.shape, 0).astype(F32)
    return jnp.sum(jnp.where(row == index, table, 0.0), axis=0, keepdims=True)


def _topk_kernel(s_ref, i_ref, j_ref, g_ref, ei_ref, ej_ref, eg_ref):
    k = PEER_TOPK
    lanes = LANES
    key_id = lax.broadcasted_iota(jnp.int32, (PEER_KEYS, lanes), 0).astype(F32)
    sub = lax.broadcasted_iota(jnp.int32, (SUBLANES, lanes), 0).astype(F32)
    slot = lax.broadcasted_iota(jnp.int32, (k, lanes), 0)

    def head(h, carry):
        s1, i1 = _extract_top(s_ref[2 * h], key_id, k)
        s2, i2 = _extract_top(s_ref[2 * h + 1], key_id, k)
        pieces, flats = [], []
        for b0 in (0, SUBLANES):
            pieces.append(s1[0:1, :] + s2[b0:b0 + SUBLANES, :])
            flats.append(sub + float(b0))
        for a in range(1, SUBLANES):
            limit = k // (a + 1)
            pieces.append(jnp.where(sub < float(limit), s1[a:a + 1, :] + s2[0:SUBLANES, :], -jnp.inf))
            flats.append(sub + float(a * k))
        pieces.append(s1[SUBLANES:k, :] + s2[0:1, :])
        flats.append((sub + float(SUBLANES)) * float(k))
        cand = jnp.concatenate(pieces, axis=0)
        flat = jnp.concatenate(flats, axis=0)
        best_s, best_flat = _extract_top(cand, flat, k)

        ent_i = jnp.zeros((k, lanes), F32)
        ent_j = jnp.zeros((k, lanes), F32)
        for r in range(k):
            pos = best_flat[r:r + 1, :]
            a_idx = jnp.floor(pos * (1.0 / k))
            b_idx = pos - a_idx * float(k)
            ent_i = jnp.where(slot == r, _gather_rows(i1, a_idx), ent_i)
            ent_j = jnp.where(slot == r, _gather_rows(i2, b_idx), ent_j)
        e = jnp.exp(best_s - best_s[0:1, :])
        gate = e / jnp.sum(e, axis=0, keepdims=True)

        rows = pl.ds(pl.multiple_of(h * k, k), k)
        ei_ref[rows, :] = ent_i
        ej_ref[rows, :] = ent_j
        eg_ref[rows, :] = gate
        return carry

    lax.fori_loop(0, PEER_HEADS, head, 0, unroll=2)
    i_ref[...] = ei_ref[...].T
    j_ref[...] = ej_ref[...].T
    g_ref[...] = eg_ref[...].T


def _topk(scores_t):
    n_hp, n_keys, t = scores_t.shape
    ent = jax.ShapeDtypeStruct((t, PEER_ENTRIES), F32)
    ent_spec = pl.BlockSpec((LANES, PEER_ENTRIES), lambda i: (i, 0))
    return pl.pallas_call(
        _topk_kernel,
        grid=(t // LANES,),
        in_specs=[pl.BlockSpec((n_hp, n_keys, LANES), lambda i: (0, 0, i))],
        out_specs=[ent_spec, ent_spec, ent_spec],
        out_shape=[ent, ent, ent],
        scratch_shapes=[pltpu.VMEM((PEER_ENTRIES, LANES), F32)] * 3,
        compiler_params=pltpu.CompilerParams(
            dimension_semantics=("parallel",), vmem_limit_bytes=VMEM_LIMIT),
        name="topk",
    )(scores_t)


W_PITCH = PEER_KEYS + SUBLANES
PAIR = 2 * PEER_KEYS
TOKEN_UNROLL = 16


def _gelu(x):
    return 0.5 * x * (1.0 + lax.erf(x * (2.0 ** -0.5)))


def _peer_kernel(n_ref, i_ref, j_ref, g_ref, u_ref, v_ref, h_ref, nw_ref, o_ref,
                 w_ref, hid_ref, acc_ref, *, block_t, block_e):
    e_step = pl.program_id(1)
    n_pairs = block_e // PAIR

    half_t = block_t // 2

    @pl.when(e_step == 0)
    def _():
        acc_ref[...] = jnp.zeros_like(acc_ref)
        grid_row = lax.broadcasted_iota(jnp.int32, (PEER_KEYS, PEER_ENTRIES), 0).astype(F32)

        def gate_grid(t):
            i_row = i_ref[pl.ds(t, 1), :]
            j_row = j_ref[pl.ds(t, 1), :]
            g_row = g_ref[pl.ds(t, 1), :]
            a_t = jnp.where(grid_row == i_row, g_row, 0.0).astype(BF16)
            b_t = jnp.where(grid_row == j_row, 1.0, 0.0).astype(BF16)
            return _dot_nt(a_t, b_t)

        def token_pair(t, carry):
            packed = pltpu.pack_elementwise([gate_grid(t), gate_grid(t + half_t)],
                                            packed_dtype=BF16)
            start = pl.multiple_of(t * W_PITCH, SUBLANES)
            w_ref[pl.ds(start, PEER_KEYS), :] = lax.bitcast_convert_type(packed, jnp.uint32)
            return carry

        lax.fori_loop(0, half_t, token_pair, 0, unroll=TOKEN_UNROLL // 2)

    def gate_rows(i):
        words = w_ref[pl.ds(i, half_t, stride=W_PITCH), :]
        halves = [pltpu.unpack_elementwise(words, index=k, packed_dtype=BF16, unpacked_dtype=F32)
                  for k in range(2)]
        return jnp.concatenate(halves, axis=0)

    n = n_ref[...]
    for p in range(n_pairs):
        experts = slice(p * PAIR, (p + 1) * PAIR)
        act = _dot_nt(n, u_ref[experts, :])
        i0 = e_step * (block_e // PEER_KEYS) + 2 * p
        w_pair = jnp.concatenate([gate_rows(i0), gate_rows(i0 + 1)], axis=-1)
        hid_ref[:, experts] = (_gelu(act) * w_pair).astype(BF16)
    acc_ref[...] += _dot(hid_ref[...], v_ref[...])

    @pl.when(e_step == pl.num_programs(1) - 1)
    def _():
        h = h_ref[...] + acc_ref[...]
        o_ref[...] = h * _rms_scale(h) * nw_ref[...]


def _peer(n2, ent_i, ent_j, ent_g, u, v, h1, norm_w, block_t, block_e):
    t = n2.shape[0]
    n_exp = u.shape[0]
    tok = lambda w: pl.BlockSpec((block_t, w), lambda i, e: (i, 0))
    return pl.pallas_call(
        functools.partial(_peer_kernel, block_t=block_t, block_e=block_e),
        grid=(t // block_t, n_exp // block_e),
        in_specs=[tok(D_MODEL), tok(PEER_ENTRIES), tok(PEER_ENTRIES), tok(PEER_ENTRIES),
                  pl.BlockSpec((block_e, D_MODEL), lambda i, e: (e, 0)),
                  pl.BlockSpec((block_e, D_MODEL), lambda i, e: (e, 0)),
                  tok(D_MODEL),
                  pl.BlockSpec((1, D_MODEL), lambda i, e: (0, 0))],
        out_specs=tok(D_MODEL),
        out_shape=jax.ShapeDtypeStruct((t, D_MODEL), F32),
        scratch_shapes=[pltpu.VMEM((block_t // 2 * W_PITCH, PEER_KEYS), jnp.uint32),
                        pltpu.VMEM((block_t, block_e), BF16),
                        pltpu.VMEM((block_t, D_MODEL), F32)],
        compiler_params=pltpu.CompilerParams(
            dimension_semantics=("parallel", "arbitrary"), vmem_limit_bytes=VMEM_LIMIT),
        name="peer",
    )(n2, ent_i, ent_j, ent_g, u, v, h1, norm_w)


def _pad_cols(w, width):
    return jnp.pad(w, ((0, 0), (0, width - w.shape[1])))


def _in_proj_slabs(w_in):
    parts, start = [], 0
    for size in _IN_SIZES:
        parts.append(w_in[:, start:start + size])
        start += size
    q, k, v, gate_lr, g_out, z, xbc, dt = parts
    return jnp.concatenate(
        [q, k, v, g_out, z, xbc, _pad_cols(gate_lr, LANES), _pad_cols(dt, LANES)],
        axis=1).astype(BF16)


def _layer(x2, batch, seq, p):
    row = lambda a: a.reshape(1, -1).astype(F32)
    q, k, v, g_out, z, xbc, gate_lr, dt = _in_proj(
        x2, row(p["norm_mix_w"]), _in_proj_slabs(p["w_in"]), block_t=512)

    w2p = jnp.pad(p["gla_w_gate2"].astype(F32), ((0, LANES - GLA_RANK), (0, 0)))
    o_gla = _gla(q, k, v, gate_lr, g_out, w2p, row(p["gla_b_gate"]), row(p["gla_norm_w"]),
                 batch, seq, block_l=256)

    pad_heads = lambda a: _pad_cols(row(a), LANES)
    d_skip = jnp.repeat(p["ssd_d"].astype(F32), SSD_P).reshape(1, SSD_WIDTH)
    o_ssd = _ssd(z, xbc, dt, p["ssd_conv_w"].astype(F32), row(p["ssd_conv_b"]),
                 pad_heads(p["ssd_dt_bias"]), pad_heads(p["ssd_a_log"]), d_skip,
                 row(p["ssd_norm_w"]), batch, seq, block_l=512)

    keys = p["peer_sub_keys"].reshape(2 * PEER_HEADS, PEER_KEYS, PEER_HALF).astype(BF16)
    h1, n2, scores_t = _out_query(
        o_gla, o_ssd, x2, p["w_out"].astype(BF16), row(p["norm_ffn_w"]),
        p["peer_w_query"].astype(BF16), keys, block_t=512)

    ent_i, ent_j, ent_g = _topk(scores_t)
    return n2, ent_i, ent_j, ent_g, h1


def kernel(x, norm_mix_w, w_in, gla_w_gate2, gla_b_gate, gla_norm_w, ssd_conv_w, ssd_conv_b,
           ssd_dt_bias, ssd_a_log, ssd_d, ssd_norm_w, w_out, norm_ffn_w, peer_w_query,
           peer_sub_keys, peer_u, peer_v, norm_final_w):
    batch, seq, d = x.shape
    assert w_in.shape[0] == 1, "single-layer trunk"
    params = dict(norm_mix_w=norm_mix_w, w_in=w_in, gla_w_gate2=gla_w_gate2, gla_b_gate=gla_b_gate,
                  gla_norm_w=gla_norm_w, ssd_conv_w=ssd_conv_w, ssd_conv_b=ssd_conv_b,
                  ssd_dt_bias=ssd_dt_bias, ssd_a_log=ssd_a_log, ssd_d=ssd_d, ssd_norm_w=ssd_norm_w,
                  w_out=w_out, norm_ffn_w=norm_ffn_w, peer_w_query=peer_w_query,
                  peer_sub_keys=peer_sub_keys)
    p = {name: value[0] for name, value in params.items()}
    x2 = x.reshape(batch * seq, d).astype(F32)
    n2, ent_i, ent_j, ent_g, h1 = _layer(x2, batch, seq, p)
    y = _peer(n2, ent_i, ent_j, ent_g, peer_u[0].astype(BF16), peer_v[0].astype(BF16), h1,
              norm_final_w.reshape(1, d).astype(F32), block_t=512, block_e=2048)
    return y.reshape(batch, seq, d).astype(x.dtype)
```

```python
import functools

import jax
import jax.numpy as jnp
from jax import lax
from jax.experimental import pallas as pl
from jax.experimental.pallas import tpu as pltpu

F32 = jnp.float32
BF16 = jnp.bfloat16
HIGHEST = lax.Precision.HIGHEST

EPS = 1e-6
D_MODEL = 1024

GLA_HEADS = 4
GLA_DK = 64
GLA_DV = 128
GLA_QK = GLA_HEADS * GLA_DK
GLA_V = GLA_HEADS * GLA_DV
GLA_RANK = 16
GLA_NORMALIZER = 16.0
GLA_CHUNK = 64

SSD_HEADS = 8
SSD_P = 64
SSD_WIDTH = SSD_HEADS * SSD_P
SSD_GROUPS = 2
SSD_N = 64
SSD_CONV = 4
SSD_CHUNK = 128
SSD_BC = SSD_GROUPS * SSD_N
SSD_CONV_DIM = SSD_WIDTH + 2 * SSD_BC

PEER_HEADS = 8
PEER_KEYS = 128
PEER_TOPK = 16
PEER_HALF = 128
PEER_ENTRIES = PEER_HEADS * PEER_TOPK

LANES = 128
SUBLANES = 8
VMEM_LIMIT = 56 * 1024 * 1024

_IN_SIZES = (GLA_QK, GLA_QK, GLA_V, GLA_RANK, GLA_V, SSD_WIDTH, SSD_CONV_DIM, SSD_HEADS)
_SLAB_WIDTHS = (GLA_QK, GLA_QK, GLA_V, GLA_V, SSD_WIDTH, SSD_CONV_DIM, LANES, LANES)


def _dot(a, b, dims=((1,), (0,)), precision=None):
    return lax.dot_general(a, b, (dims, ((), ())), precision=precision,
                           preferred_element_type=F32)


def _dot_nt(a, b, precision=None):
    return _dot(a, b, ((1,), (1,)), precision)


def _dot_tn(a, b, precision=None):
    return _dot(a, b, ((0,), (0,)), precision)


def _silu(x):
    return x * (1.0 / (1.0 + jnp.exp(-x)))


def _softplus(x):
    return jnp.maximum(x, 0.0) + jnp.log1p(jnp.exp(-jnp.abs(x)))


def _rms_scale(x):
    return lax.rsqrt(jnp.mean(x * x, axis=-1, keepdims=True) + EPS)


def _in_proj_kernel(x_ref, nw_ref, w_ref, *out_refs):
    x = x_ref[...]
    n = (x * _rms_scale(x) * nw_ref[...]).astype(BF16)
    start = 0
    for o_ref, width in zip(out_refs, _SLAB_WIDTHS):
        o_ref[...] = _dot(n, w_ref[:, start:start + width])
        start += width


def _in_proj(x2, norm_w, w_slabs, block_t):
    t = x2.shape[0]
    total = sum(_SLAB_WIDTHS)
    out_shape = [jax.ShapeDtypeStruct((t, w), F32) for w in _SLAB_WIDTHS]
    out_specs = [pl.BlockSpec((block_t, w), lambda i: (i, 0)) for w in _SLAB_WIDTHS]
    return pl.pallas_call(
        _in_proj_kernel,
        grid=(t // block_t,),
        in_specs=[pl.BlockSpec((block_t, D_MODEL), lambda i: (i, 0)),
                  pl.BlockSpec((1, D_MODEL), lambda i: (0, 0)),
                  pl.BlockSpec((D_MODEL, total), lambda i: (0, 0))],
        out_specs=out_specs,
        out_shape=out_shape,
        compiler_params=pltpu.CompilerParams(
            dimension_semantics=("parallel",), vmem_limit_bytes=VMEM_LIMIT),
        name="in_proj",
    )(x2, norm_w, w_slabs)


def _gla_kernel(q_ref, k_ref, v_ref, glr_ref, gout_ref, w2_ref, bg_ref, nw_ref,
                o_ref, st_ref, *, n_chunks):
    c_len = GLA_CHUNK

    @pl.when(pl.program_id(1) == 0)
    def _():
        st_ref[...] = jnp.zeros_like(st_ref)

    def iota(shape, dim):
        return lax.broadcasted_iota(jnp.int32, shape, dim)

    block_l = n_chunks * c_len
    gate = _dot(glr_ref[...], w2_ref[...], precision=HIGHEST) + bg_ref[...]
    log_a = -_softplus(-gate) * (1.0 / GLA_NORMALIZER)
    tri = ((iota((block_l, block_l), 0) // c_len == iota((block_l, block_l), 1) // c_len)
           & (iota((block_l, block_l), 0) >= iota((block_l, block_l), 1))).astype(F32)
    g_cum = _dot(tri, log_a, precision=HIGHEST)
    g_end = [g_cum[(c + 1) * c_len - 1:(c + 1) * c_len, :] for c in range(n_chunks)]
    g_last = jnp.concatenate([jnp.broadcast_to(g, (c_len, GLA_QK)) for g in g_end], axis=0)
    k = k_ref[...]
    q_in = (q_ref[...] * (GLA_DK ** -0.5) * jnp.exp(g_cum)).astype(BF16)
    k_in = k * jnp.exp(-g_cum)
    k_end = (k * jnp.exp(g_last - g_cum)).astype(BF16)
    v = v_ref[...]

    k_bd_mask = iota((GLA_QK, GLA_QK), 0) // c_len == iota((GLA_QK, GLA_QK), 1) // GLA_DK
    v_bd_mask = iota((GLA_QK, GLA_V), 0) // c_len == iota((GLA_QK, GLA_V), 1) // GLA_DV
    causal = iota((c_len, GLA_QK), 0) >= iota((c_len, GLA_QK), 1) % c_len
    state_mask = iota((GLA_V, GLA_QK), 0) // GLA_DV == iota((GLA_V, GLA_QK), 1) // GLA_DK

    att = []
    for c in range(n_chunks):
        rows = slice(c * c_len, (c + 1) * c_len)
        k_bd = jnp.where(k_bd_mask, jnp.concatenate([k_in[rows]] * GLA_HEADS, axis=0), 0.0)
        scores = _dot_nt(q_in[rows], k_bd.astype(BF16))
        att.append(jnp.where(causal, scores, 0.0).astype(BF16))

    state_t = st_ref[...]
    nw = jnp.concatenate([nw_ref[...]] * GLA_HEADS, axis=1)
    for c in range(n_chunks):
        rows = slice(c * c_len, (c + 1) * c_len)
        v_c = v[rows]
        new_state = _dot_tn(v_c.astype(BF16), k_end[rows])
        o_inter = _dot_nt(q_in[rows], state_t.astype(BF16))
        state_t = state_t * jnp.exp(g_end[c]) + jnp.where(state_mask, new_state, 0.0)
        v_bd = jnp.where(v_bd_mask, jnp.concatenate([v_c] * GLA_HEADS, axis=0), 0.0)
        o = _dot(att[c], v_bd.astype(BF16)) + o_inter
        outs = []
        for h in range(GLA_HEADS):
            o_h = o[:, h * GLA_DV:(h + 1) * GLA_DV]
            outs.append(o_h * _rms_scale(o_h))
        o = jnp.concatenate(outs, axis=1) * nw
        o_ref[rows, :] = (o * _silu(gout_ref[rows, :])).astype(o_ref.dtype)
    st_ref[...] = state_t


def _gla(q, k, v, glr, gout, w2p, bg, nw, batch, seq, block_l):
    t = batch * seq
    nblk = seq // block_l
    tok = lambda w: pl.BlockSpec((block_l, w), lambda b, j: (b * nblk + j, 0))
    const = lambda shape: pl.BlockSpec(shape, lambda b, j: (0, 0))
    return pl.pallas_call(
        functools.partial(_gla_kernel, n_chunks=block_l // GLA_CHUNK),
        grid=(batch, nblk),
        in_specs=[tok(GLA_QK), tok(GLA_QK), tok(GLA_V), tok(LANES), tok(GLA_V),
                  const((LANES, GLA_QK)), const((1, GLA_QK)), const((1, GLA_DV))],
        out_specs=tok(GLA_V),
        out_shape=jax.ShapeDtypeStruct((t, GLA_V), BF16),
        scratch_shapes=[pltpu.VMEM((GLA_V, GLA_QK), F32)],
        compiler_params=pltpu.CompilerParams(
            dimension_semantics=("parallel", "arbitrary"), vmem_limit_bytes=VMEM_LIMIT),
        name="gla",
    )(q, k, v, glr, gout, w2p, bg, nw)


def _ssd_kernel(z_ref, xbc_ref, dt_ref, cw_ref, cb_ref, dtb_ref, alog_ref, dskip_ref, nw_ref,
                o_ref, xs_ref, st_ref, *, block_l):
    c_len = SSD_CHUNK
    halo = SUBLANES

    @pl.when(pl.program_id(1) == 0)
    def _():
        xs_ref[0:halo, :] = jnp.zeros((halo, SSD_CONV_DIM), F32)
        st_ref[...] = jnp.zeros_like(st_ref)

    xs_ref[halo:halo + block_l, :] = xbc_ref[...]

    row = lax.broadcasted_iota(jnp.int32, (c_len, c_len), 0)
    col = lax.broadcasted_iota(jnp.int32, (c_len, c_len), 1)
    causal = row >= col
    tri = causal.astype(F32)
    first_half = col < SSD_N
    cw = cw_ref[...]
    cb = cb_ref[...]
    a_neg = -jnp.exp(alog_ref[...])
    dtb = dtb_ref[...]

    def chunk(c, carry):
        r0 = pl.multiple_of(c * c_len, c_len)
        rows = pl.ds(r0, c_len)
        window = xs_ref[pl.ds(r0, c_len + halo), :]
        conv = cb
        for tap in range(SSD_CONV):
            shift = halo - (SSD_CONV - 1) + tap
            conv = conv + cw[tap:tap + 1, :] * window[shift:shift + c_len, :]
        xc = _silu(conv)
        b_m = xc[:, SSD_WIDTH:SSD_WIDTH + SSD_BC]
        c_m = xc[:, SSD_WIDTH + SSD_BC:]

        dt = _softplus(dt_ref[rows, :] + dtb)
        a_cum = _dot(tri, dt * a_neg, precision=HIGHEST)
        a_cum_t = a_cum.T

        scores = []
        c_g = []
        for g in range(SSD_GROUPS):
            in_group = first_half if g == 0 else jnp.logical_not(first_half)
            c_g.append(jnp.where(in_group, c_m, 0.0).astype(BF16))
            scores.append(_dot_nt(c_g[g], b_m.astype(BF16)))

        for m in range(SSD_HEADS // 2):
            g = (2 * m) // (SSD_HEADS // SSD_GROUPS)
            lanes = slice(m * LANES, (m + 1) * LANES)
            x_pair = xc[:, lanes]
            halves = []
            dts = []
            for par in range(2):
                h = 2 * m + par
                dts.append(jnp.broadcast_to(dt[:, h:h + 1], (c_len, LANES)))
            xdt = (x_pair * jnp.where(first_half, dts[0], dts[1])).astype(BF16)
            for par in range(2):
                h = 2 * m + par
                a_col = jnp.broadcast_to(a_cum[:, h:h + 1], (c_len, c_len))
                a_row = jnp.broadcast_to(a_cum_t[h:h + 1, :], (c_len, c_len))
                a_end = a_col[c_len - 1:c_len, :]
                decay = jnp.exp(jnp.where(causal, a_col - a_row, -jnp.inf))
                y_diag = _dot((scores[g] * decay).astype(BF16), xdt)
                state = st_ref[h]
                y_off = _dot(c_g[g], state.astype(BF16)) * jnp.exp(a_col)
                halves.append(y_diag + y_off)
                b_dec = (b_m * jnp.exp(a_end - a_col)).astype(BF16)
                st_ref[h] = state * jnp.exp(a_end) + _dot_tn(b_dec, xdt)
            y = jnp.where(first_half, halves[0], halves[1]) + x_pair * dskip_ref[:, lanes]
            y = y * _silu(z_ref[rows, lanes])
            xs_pair_sq = jnp.sum(y * y, axis=-1, keepdims=True)
            if m % 2 == 0:
                y_prev, sq_prev = y, xs_pair_sq
            else:
                scale = lax.rsqrt((sq_prev + xs_pair_sq) * (1.0 / (2 * LANES)) + EPS)
                lo = slice((m - 1) * LANES, m * LANES)
                o_ref[rows, lo] = (y_prev * scale * nw_ref[:, lo]).astype(o_ref.dtype)
                o_ref[rows, lanes] = (y * scale * nw_ref[:, lanes]).astype(o_ref.dtype)
        return carry

    lax.fori_loop(0, block_l // c_len, chunk, 0)
    xs_ref[0:halo, :] = xs_ref[block_l:block_l + halo, :]


def _ssd(z, xbc, dt, cw, cb, dtb, alog, dskip, nw, batch, seq, block_l):
    t = batch * seq
    nblk = seq // block_l
    tok = lambda w: pl.BlockSpec((block_l, w), lambda b, j: (b * nblk + j, 0))
    const = lambda shape: pl.BlockSpec(shape, lambda b, j: (0, 0))
    return pl.pallas_call(
        functools.partial(_ssd_kernel, block_l=block_l),
        grid=(batch, nblk),
        in_specs=[tok(SSD_WIDTH), tok(SSD_CONV_DIM), tok(LANES),
                  const((SSD_CONV, SSD_CONV_DIM)), const((1, SSD_CONV_DIM)),
                  const((1, LANES)), const((1, LANES)), const((1, SSD_WIDTH)),
                  const((1, SSD_WIDTH))],
        out_specs=tok(SSD_WIDTH),
        out_shape=jax.ShapeDtypeStruct((t, SSD_WIDTH), BF16),
        scratch_shapes=[pltpu.VMEM((block_l + SUBLANES, SSD_CONV_DIM), F32),
                        pltpu.VMEM((SSD_HEADS, SSD_BC, LANES), F32)],
        compiler_params=pltpu.CompilerParams(
            dimension_semantics=("parallel", "arbitrary"), vmem_limit_bytes=VMEM_LIMIT),
        name="ssd",
    )(z, xbc, dt, cw, cb, dtb, alog, dskip, nw)


def _out_query_kernel(og_ref, os_ref, x_ref, wo_ref, nw_ref, wq_ref, keys_ref,
                      h_ref, n_ref, s_ref, q_ref):
    mixed = _dot(og_ref[...], wo_ref[0:GLA_V, :]) + _dot(os_ref[...], wo_ref[GLA_V:, :])
    h = x_ref[...] + mixed
    h_ref[...] = h
    n = (h * _rms_scale(h) * nw_ref[...]).astype(BF16)
    n_ref[...] = n
    q_ref[...] = _dot(n, wq_ref[...]).astype(BF16)
    for hp in range(2 * PEER_HEADS):
        lanes = slice(hp * PEER_HALF, (hp + 1) * PEER_HALF)
        s_ref[hp] = _dot_nt(keys_ref[hp], q_ref[:, lanes])


def _out_query(o_gla, o_ssd, x2, w_out, norm_w, w_query, keys, block_t):
    t = x2.shape[0]
    n_hp = 2 * PEER_HEADS
    return pl.pallas_call(
        _out_query_kernel,
        grid=(t // block_t,),
        in_specs=[pl.BlockSpec((block_t, GLA_V), lambda i: (i, 0)),
                  pl.BlockSpec((block_t, SSD_WIDTH), lambda i: (i, 0)),
                  pl.BlockSpec((block_t, D_MODEL), lambda i: (i, 0)),
                  pl.BlockSpec((D_MODEL, D_MODEL), lambda i: (0, 0)),
                  pl.BlockSpec((1, D_MODEL), lambda i: (0, 0)),
                  pl.BlockSpec((D_MODEL, n_hp * PEER_HALF), lambda i: (0, 0)),
                  pl.BlockSpec((n_hp, PEER_KEYS, PEER_HALF), lambda i: (0, 0, 0))],
        out_specs=[pl.BlockSpec((block_t, D_MODEL), lambda i: (i, 0)),
                   pl.BlockSpec((block_t, D_MODEL), lambda i: (i, 0)),
                   pl.BlockSpec((n_hp, PEER_KEYS, block_t), lambda i: (0, 0, i))],
        out_shape=[jax.ShapeDtypeStruct((t, D_MODEL), F32),
                   jax.ShapeDtypeStruct((t, D_MODEL), BF16),
                   jax.ShapeDtypeStruct((n_hp, PEER_KEYS, t), F32)],
        scratch_shapes=[pltpu.VMEM((block_t, n_hp * PEER_HALF), BF16)],
        compiler_params=pltpu.CompilerParams(
            dimension_semantics=("parallel",), vmem_limit_bytes=VMEM_LIMIT),
        name="out_query",
    )(o_gla, o_ssd, x2, w_out, norm_w, w_query, keys)


def _extract_top(values, ids, count):
    lanes = values.shape[1]
    slot = lax.broadcasted_iota(jnp.int32, (count, lanes), 0)
    top_v = jnp.zeros((count, lanes), F32)
    top_i = jnp.zeros((count, lanes), F32)
    for r in range(count):
        m = jnp.max(values, axis=0, keepdims=True)
        sel = jnp.min(jnp.where(values == m, ids, jnp.inf), axis=0, keepdims=True)
        values = jnp.where(ids == sel, -jnp.inf, values)
        top_v = jnp.where(slot == r, m, top_v)
        top_i = jnp.where(slot == r, sel, top_i)
    return top_v, top_i, jnp.zeros((1, lanes), F32)


def _sorting_network(n):
    size = 1 << (n - 1).bit_length()
    pairs = []
    p = 1
    while p < size:
        k = p
        while k >= 1:
            for j in range(k % p, size - k, 2 * k):
                for i in range(min(k, size - j - k)):
                    if (i + j) // (2 * p) == (i + j + k) // (2 * p):
                        pairs.append((i + j, i + j + k))
            k //= 2
        p *= 2
    return [(a, b) for a, b in pairs if b < n]


def _pop_top(values, ids, count):
    rows, lanes = values.shape
    n = rows // SUBLANES
    vals = [values[v * SUBLANES:(v + 1) * SUBLANES] for v in range(n)]
    idl = [ids[v * SUBLANES:(v + 1) * SUBLANES] for v in range(n)]
    for a, b in _sorting_network(n):
        swap = vals[b] > vals[a]
        vals[a], vals[b] = jnp.where(swap, vals[b], vals[a]), jnp.where(swap, vals[a], vals[b])
        idl[a], idl[b] = jnp.where(swap, idl[b], idl[a]), jnp.where(swap, idl[a], idl[b])

    slot = lax.broadcasted_iota(jnp.int32, (count, lanes), 0)
    top_v = jnp.zeros((count, lanes), F32)
    top_i = jnp.zeros((count, lanes), F32)
    tie = jnp.zeros((1, lanes), F32)
    prev = None
    for r in range(count + 1):
        m = jnp.max(vals[0], axis=0, keepdims=True)
        if prev is not None:
            tie = jnp.where(m == prev, 1.0, tie)
        prev = m
        if r == count:
            break
        sel = jnp.min(jnp.where(vals[0] == m, idl[0], jnp.inf), axis=0, keepdims=True)
        top_v = jnp.where(slot == r, m, top_v)
        top_i = jnp.where(slot == r, sel, top_i)
        hit = idl[0] == sel
        depth = min(n - 1, count - r)
        for v in range(depth):
            vals[v] = jnp.where(hit, vals[v + 1], vals[v])
            idl[v] = jnp.where(hit, idl[v + 1], idl[v])
        vals[depth] = jnp.where(hit, -jnp.inf, vals[depth])
    return top_v, top_i, tie


def _gather_rows(table, index):
    row = lax.broadcasted_iota(jnp.int32, table.shape, 0).astype(F32)
    return jnp.sum(jnp.where(row == index, table, 0.0), axis=0, keepdims=True)


def _head_entries(s_ref, h, top_fn):
    k = PEER_TOPK
    lanes = LANES
    key_id = lax.broadcasted_iota(jnp.int32, (PEER_KEYS, lanes), 0).astype(F32)
    sub = lax.broadcasted_iota(jnp.int32, (SUBLANES, lanes), 0).astype(F32)
    slot = lax.broadcasted_iota(jnp.int32, (k, lanes), 0)

    s1, i1, tie1 = top_fn(s_ref[2 * h], key_id, k)
    s2, i2, tie2 = top_fn(s_ref[2 * h + 1], key_id, k)
    pieces, flats = [], []
    for b0 in (0, SUBLANES):
        pieces.append(s1[0:1, :] + s2[b0:b0 + SUBLANES, :])
        flats.append(sub + float(b0))
    for a in range(1, SUBLANES):
        limit = k // (a + 1)
        pieces.append(jnp.where(sub < float(limit), s1[a:a + 1, :] + s2[0:SUBLANES, :], -jnp.inf))
        flats.append(sub + float(a * k))
    pieces.append(s1[SUBLANES:k, :] + s2[0:1, :])
    flats.append((sub + float(SUBLANES)) * float(k))
    cand = jnp.concatenate(pieces, axis=0)
    flat = jnp.concatenate(flats, axis=0)
    best_s, best_flat, tie3 = top_fn(cand, flat, k)

    ent_i = jnp.zeros((k, lanes), F32)
    ent_j = jnp.zeros((k, lanes), F32)
    for r in range(k):
        pos = best_flat[r:r + 1, :]
        a_idx = jnp.floor(pos * (1.0 / k))
        b_idx = pos - a_idx * float(k)
        ent_i = jnp.where(slot == r, _gather_rows(i1, a_idx), ent_i)
        ent_j = jnp.where(slot == r, _gather_rows(i2, b_idx), ent_j)
    e = jnp.exp(best_s - best_s[0:1, :])
    gate = e / jnp.sum(e, axis=0, keepdims=True)
    return ent_i, ent_j, gate, jnp.maximum(jnp.maximum(tie1, tie2), tie3)


def _topk_kernel(s_ref, i_ref, j_ref, g_ref, ei_ref, ej_ref, eg_ref, tie_ref):
    k = PEER_TOPK

    def store(h, ent_i, ent_j, gate):
        rows = pl.ds(pl.multiple_of(h * k, k), k)
        ei_ref[rows, :] = ent_i
        ej_ref[rows, :] = ent_j
        eg_ref[rows, :] = gate

    def fast_head(h, carry):
        ent_i, ent_j, gate, tie = _head_entries(s_ref, h, _pop_top)
        store(h, ent_i, ent_j, gate)
        tie_ref[pl.ds(h, 1), :] = tie
        return carry

    def exact_head(h, carry):
        @pl.when(jnp.max(tie_ref[pl.ds(h, 1), :]) > 0.0)
        def _():
            ent_i, ent_j, gate, _ = _head_entries(s_ref, h, _extract_top)
            store(h, ent_i, ent_j, gate)
        return carry

    lax.fori_loop(0, PEER_HEADS, fast_head, 0, unroll=2)

    @pl.when(jnp.max(tie_ref[...]) > 0.0)
    def _():
        lax.fori_loop(0, PEER_HEADS, exact_head, 0)

    i_ref[...] = ei_ref[...].T
    j_ref[...] = ej_ref[...].T
    g_ref[...] = eg_ref[...].T


def _topk(scores_t):
    n_hp, n_keys, t = scores_t.shape
    ent = jax.ShapeDtypeStruct((t, PEER_ENTRIES), F32)
    ent_spec = pl.BlockSpec((LANES, PEER_ENTRIES), lambda i: (i, 0))
    return pl.pallas_call(
        _topk_kernel,
        grid=(t // LANES,),
        in_specs=[pl.BlockSpec((n_hp, n_keys, LANES), lambda i: (0, 0, i))],
        out_specs=[ent_spec, ent_spec, ent_spec],
        out_shape=[ent, ent, ent],
        scratch_shapes=[pltpu.VMEM((PEER_ENTRIES, LANES), F32)] * 3
        + [pltpu.VMEM((PEER_HEADS, LANES), F32)],
        compiler_params=pltpu.CompilerParams(
            dimension_semantics=("parallel",), vmem_limit_bytes=VMEM_LIMIT),
        name="topk",
    )(scores_t)


W_PITCH = PEER_KEYS + SUBLANES
PAIR = 2 * PEER_KEYS
TOKEN_UNROLL = 32


def _gelu(x):
    return 0.5 * x * (1.0 + lax.erf(x * (2.0 ** -0.5)))


def _peer_kernel(n_ref, i_ref, j_ref, g_ref, u_ref, v_ref, h_ref, nw_ref, o_ref,
                 w_ref, hid_ref, acc_ref, *, block_t, block_e):
    e_step = pl.program_id(1)
    n_pairs = block_e // PAIR

    half_t = block_t // 2

    @pl.when(e_step == 0)
    def _():
        acc_ref[...] = jnp.zeros_like(acc_ref)
        grid_row = lax.broadcasted_iota(jnp.int32, (PEER_KEYS, PEER_ENTRIES), 0).astype(F32)

        def gate_grid(t):
            i_row = i_ref[pl.ds(t, 1), :]
            j_row = j_ref[pl.ds(t, 1), :]
            g_row = g_ref[pl.ds(t, 1), :]
            a_t = jnp.where(grid_row == i_row, g_row, 0.0).astype(BF16)
            b_t = jnp.where(grid_row == j_row, 1.0, 0.0).astype(BF16)
            return _dot_nt(a_t, b_t)

        def token_pair(t, carry):
            packed = pltpu.pack_elementwise([gate_grid(t), gate_grid(t + half_t)],
                                            packed_dtype=BF16)
            start = pl.multiple_of(t * W_PITCH, SUBLANES)
            w_ref[pl.ds(start, PEER_KEYS), :] = lax.bitcast_convert_type(packed, jnp.uint32)
            return carry

        lax.fori_loop(0, half_t, token_pair, 0, unroll=TOKEN_UNROLL // 2)

    def gate_rows(i):
        words = w_ref[pl.ds(i, half_t, stride=W_PITCH), :]
        halves = [pltpu.unpack_elementwise(words, index=k, packed_dtype=BF16, unpacked_dtype=F32)
                  for k in range(2)]
        return jnp.concatenate(halves, axis=0)

    n = n_ref[...]
    for p in range(n_pairs):
        experts = slice(p * PAIR, (p + 1) * PAIR)
        act = _dot_nt(n, u_ref[experts, :])
        i0 = e_step * (block_e // PEER_KEYS) + 2 * p
        w_pair = jnp.concatenate([gate_rows(i0), gate_rows(i0 + 1)], axis=-1)
        hid_ref[:, experts] = (_gelu(act) * w_pair).astype(BF16)
    acc_ref[...] += _dot(hid_ref[...], v_ref[...])

    @pl.when(e_step == pl.num_programs(1) - 1)
    def _():
        h = h_ref[...] + acc_ref[...]
        o_ref[...] = h * _rms_scale(h) * nw_ref[...]


def _peer(n2, ent_i, ent_j, ent_g, u, v, h1, norm_w, block_t, block_e):
    t = n2.shape[0]
    n_exp = u.shape[0]
    tok = lambda w: pl.BlockSpec((block_t, w), lambda i, e: (i, 0))
    return pl.pallas_call(
        functools.partial(_peer_kernel, block_t=block_t, block_e=block_e),
        grid=(t // block_t, n_exp // block_e),
        in_specs=[tok(D_MODEL), tok(PEER_ENTRIES), tok(PEER_ENTRIES), tok(PEER_ENTRIES),
                  pl.BlockSpec((block_e, D_MODEL), lambda i, e: (e, 0)),
                  pl.BlockSpec((block_e, D_MODEL), lambda i, e: (e, 0)),
                  tok(D_MODEL),
                  pl.BlockSpec((1, D_MODEL), lambda i, e: (0, 0))],
        out_specs=tok(D_MODEL),
        out_shape=jax.ShapeDtypeStruct((t, D_MODEL), F32),
        scratch_shapes=[pltpu.VMEM((block_t // 2 * W_PITCH, PEER_KEYS), jnp.uint32),
                        pltpu.VMEM((block_t, block_e), BF16),
                        pltpu.VMEM((block_t, D_MODEL), F32)],
        compiler_params=pltpu.CompilerParams(
            dimension_semantics=("parallel", "arbitrary"), vmem_limit_bytes=VMEM_LIMIT),
        name="peer",
    )(n2, ent_i, ent_j, ent_g, u, v, h1, norm_w)


def _pad_cols(w, width):
    return jnp.pad(w, ((0, 0), (0, width - w.shape[1])))


def _in_proj_slabs(w_in):
    parts, start = [], 0
    for size in _IN_SIZES:
        parts.append(w_in[:, start:start + size])
        start += size
    q, k, v, gate_lr, g_out, z, xbc, dt = parts
    return jnp.concatenate(
        [q, k, v, g_out, z, xbc, _pad_cols(gate_lr, LANES), _pad_cols(dt, LANES)],
        axis=1).astype(BF16)


def _layer(x2, batch, seq, p):
    row = lambda a: a.reshape(1, -1).astype(F32)
    q, k, v, g_out, z, xbc, gate_lr, dt = _in_proj(
        x2, row(p["norm_mix_w"]), _in_proj_slabs(p["w_in"]), block_t=512)

    w2p = jnp.pad(p["gla_w_gate2"].astype(F32), ((0, LANES - GLA_RANK), (0, 0)))
    o_gla = _gla(q, k, v, gate_lr, g_out, w2p, row(p["gla_b_gate"]), row(p["gla_norm_w"]),
                 batch, seq, block_l=256)

    pad_heads = lambda a: _pad_cols(row(a), LANES)
    d_skip = jnp.repeat(p["ssd_d"].astype(F32), SSD_P).reshape(1, SSD_WIDTH)
    o_ssd = _ssd(z, xbc, dt, p["ssd_conv_w"].astype(F32), row(p["ssd_conv_b"]),
                 pad_heads(p["ssd_dt_bias"]), pad_heads(p["ssd_a_log"]), d_skip,
                 row(p["ssd_norm_w"]), batch, seq, block_l=512)

    keys = p["peer_sub_keys"].reshape(2 * PEER_HEADS, PEER_KEYS, PEER_HALF).astype(BF16)
    h1, n2, scores_t = _out_query(
        o_gla, o_ssd, x2, p["w_out"].astype(BF16), row(p["norm_ffn_w"]),
        p["peer_w_query"].astype(BF16), keys, block_t=512)

    ent_i, ent_j, ent_g = _topk(scores_t)
    return n2, ent_i, ent_j, ent_g, h1


def kernel(x, norm_mix_w, w_in, gla_w_gate2, gla_b_gate, gla_norm_w, ssd_conv_w, ssd_conv_b,
           ssd_dt_bias, ssd_a_log, ssd_d, ssd_norm_w, w_out, norm_ffn_w, peer_w_query,
           peer_sub_keys, peer_u, peer_v, norm_final_w):
    batch, seq, d = x.shape
    assert w_in.shape[0] == 1, "single-layer trunk"
    params = dict(norm_mix_w=norm_mix_w, w_in=w_in, gla_w_gate2=gla_w_gate2, gla_b_gate=gla_b_gate,
                  gla_norm_w=gla_norm_w, ssd_conv_w=ssd_conv_w, ssd_conv_b=ssd_conv_b,
                  ssd_dt_bias=ssd_dt_bias, ssd_a_log=ssd_a_log, ssd_d=ssd_d, ssd_norm_w=ssd_norm_w,
                  w_out=w_out, norm_ffn_w=norm_ffn_w, peer_w_query=peer_w_query,
                  peer_sub_keys=peer_sub_keys)
    p = {name: value[0] for name, value in params.items()}
    x2 = x.reshape(batch * seq, d).astype(F32)
    n2, ent_i, ent_j, ent_g, h1 = _layer(x2, batch, seq, p)
    y = _peer(n2, ent_i, ent_j, ent_g, peer_u[0].astype(BF16), peer_v[0].astype(BF16), h1,
              norm_final_w.reshape(1, d).astype(F32), block_t=512, block_e=2048)
    return y.reshape(batch, seq, d).astype(x.dtype)
```

```python
import functools

import jax
import jax.numpy as jnp
from jax import lax
from jax.experimental import pallas as pl
from jax.experimental.pallas import tpu as pltpu

F32 = jnp.float32
BF16 = jnp.bfloat16
HIGHEST = lax.Precision.HIGHEST

EPS = 1e-6
D_MODEL = 1024

GLA_HEADS = 4
GLA_DK = 64
GLA_DV = 128
GLA_QK = GLA_HEADS * GLA_DK
GLA_V = GLA_HEADS * GLA_DV
GLA_RANK = 16
GLA_NORMALIZER = 16.0
GLA_CHUNK = 64

SSD_HEADS = 8
SSD_P = 64
SSD_WIDTH = SSD_HEADS * SSD_P
SSD_GROUPS = 2
SSD_N = 64
SSD_CONV = 4
SSD_CHUNK = 128
SSD_BC = SSD_GROUPS * SSD_N
SSD_CONV_DIM = SSD_WIDTH + 2 * SSD_BC

PEER_HEADS = 8
PEER_KEYS = 128
PEER_TOPK = 16
PEER_HALF = 128
PEER_ENTRIES = PEER_HEADS * PEER_TOPK
PEER_BLOCK_T = 512
PEER_BLOCK_E = 2048
PEER_GRID_ROWS = PEER_BLOCK_E // PEER_KEYS

LANES = 128
SUBLANES = 8
VMEM_LIMIT = 56 * 1024 * 1024

_IN_SIZES = (GLA_QK, GLA_QK, GLA_V, GLA_RANK, GLA_V, SSD_WIDTH, SSD_CONV_DIM, SSD_HEADS)
_SLAB_WIDTHS = (GLA_QK, GLA_QK, GLA_V, GLA_V, SSD_WIDTH, SSD_CONV_DIM, LANES, LANES)


def _dot(a, b, dims=((1,), (0,)), precision=None):
    return lax.dot_general(a, b, (dims, ((), ())), precision=precision,
                           preferred_element_type=F32)


def _dot_nt(a, b, precision=None):
    return _dot(a, b, ((1,), (1,)), precision)


def _dot_tn(a, b, precision=None):
    return _dot(a, b, ((0,), (0,)), precision)


def _silu(x):
    return x * (1.0 / (1.0 + jnp.exp(-x)))


def _softplus(x):
    return jnp.maximum(x, 0.0) + jnp.log1p(jnp.exp(-jnp.abs(x)))


def _rms_scale(x):
    return lax.rsqrt(jnp.mean(x * x, axis=-1, keepdims=True) + EPS)


def _in_proj_kernel(x_ref, nw_ref, w_ref, *out_refs):
    x = x_ref[...]
    n = (x * _rms_scale(x) * nw_ref[...]).astype(BF16)
    start = 0
    for o_ref, width in zip(out_refs, _SLAB_WIDTHS):
        o_ref[...] = _dot(n, w_ref[:, start:start + width])
        start += width


def _in_proj(x2, norm_w, w_slabs, block_t):
    t = x2.shape[0]
    total = sum(_SLAB_WIDTHS)
    out_shape = [jax.ShapeDtypeStruct((t, w), F32) for w in _SLAB_WIDTHS]
    out_specs = [pl.BlockSpec((block_t, w), lambda i: (i, 0)) for w in _SLAB_WIDTHS]
    return pl.pallas_call(
        _in_proj_kernel,
        grid=(t // block_t,),
        in_specs=[pl.BlockSpec((block_t, D_MODEL), lambda i: (i, 0)),
                  pl.BlockSpec((1, D_MODEL), lambda i: (0, 0)),
                  pl.BlockSpec((D_MODEL, total), lambda i: (0, 0))],
        out_specs=out_specs,
        out_shape=out_shape,
        compiler_params=pltpu.CompilerParams(
            dimension_semantics=("parallel",), vmem_limit_bytes=VMEM_LIMIT),
        name="in_proj",
    )(x2, norm_w, w_slabs)


def _gla_kernel(q_ref, k_ref, v_ref, glr_ref, gout_ref, w2_ref, bg_ref, nw_ref,
                o_ref, st_ref, *, n_chunks):
    c_len = GLA_CHUNK

    @pl.when(pl.program_id(1) == 0)
    def _():
        st_ref[...] = jnp.zeros_like(st_ref)

    def iota(shape, dim):
        return lax.broadcasted_iota(jnp.int32, shape, dim)

    block_l = n_chunks * c_len
    gate = _dot(glr_ref[...], w2_ref[...], precision=HIGHEST) + bg_ref[...]
    log_a = -_softplus(-gate) * (1.0 / GLA_NORMALIZER)
    tri = ((iota((block_l, block_l), 0) // c_len == iota((block_l, block_l), 1) // c_len)
           & (iota((block_l, block_l), 0) >= iota((block_l, block_l), 1))).astype(F32)
    g_cum = _dot(tri, log_a, precision=HIGHEST)
    g_end = [g_cum[(c + 1) * c_len - 1:(c + 1) * c_len, :] for c in range(n_chunks)]
    g_last = jnp.concatenate([jnp.broadcast_to(g, (c_len, GLA_QK)) for g in g_end], axis=0)
    k = k_ref[...]
    q_in = (q_ref[...] * (GLA_DK ** -0.5) * jnp.exp(g_cum)).astype(BF16)
    k_in = k * jnp.exp(-g_cum)
    k_end = (k * jnp.exp(g_last - g_cum)).astype(BF16)
    v = v_ref[...]

    k_bd_mask = iota((GLA_QK, GLA_QK), 0) // c_len == iota((GLA_QK, GLA_QK), 1) // GLA_DK
    v_bd_mask = iota((GLA_QK, GLA_V), 0) // c_len == iota((GLA_QK, GLA_V), 1) // GLA_DV
    causal = iota((c_len, GLA_QK), 0) >= iota((c_len, GLA_QK), 1) % c_len
    state_mask = iota((GLA_V, GLA_QK), 0) // GLA_DV == iota((GLA_V, GLA_QK), 1) // GLA_DK

    att = []
    for c in range(n_chunks):
        rows = slice(c * c_len, (c + 1) * c_len)
        k_bd = jnp.where(k_bd_mask, jnp.concatenate([k_in[rows]] * GLA_HEADS, axis=0), 0.0)
        scores = _dot_nt(q_in[rows], k_bd.astype(BF16))
        att.append(jnp.where(causal, scores, 0.0).astype(BF16))

    state_t = st_ref[...]
    nw = jnp.concatenate([nw_ref[...]] * GLA_HEADS, axis=1)
    for c in range(n_chunks):
        rows = slice(c * c_len, (c + 1) * c_len)
        v_c = v[rows]
        new_state = _dot_tn(v_c.astype(BF16), k_end[rows])
        o_inter = _dot_nt(q_in[rows], state_t.astype(BF16))
        state_t = state_t * jnp.exp(g_end[c]) + jnp.where(state_mask, new_state, 0.0)
        v_bd = jnp.where(v_bd_mask, jnp.concatenate([v_c] * GLA_HEADS, axis=0), 0.0)
        o = _dot(att[c], v_bd.astype(BF16)) + o_inter
        outs = []
        for h in range(GLA_HEADS):
            o_h = o[:, h * GLA_DV:(h + 1) * GLA_DV]
            outs.append(o_h * _rms_scale(o_h))
        o = jnp.concatenate(outs, axis=1) * nw
        o_ref[rows, :] = (o * _silu(gout_ref[rows, :])).astype(o_ref.dtype)
    st_ref[...] = state_t


def _gla(q, k, v, glr, gout, w2p, bg, nw, batch, seq, block_l):
    t = batch * seq
    nblk = seq // block_l
    tok = lambda w: pl.BlockSpec((block_l, w), lambda b, j: (b * nblk + j, 0))
    const = lambda shape: pl.BlockSpec(shape, lambda b, j: (0, 0))
    return pl.pallas_call(
        functools.partial(_gla_kernel, n_chunks=block_l // GLA_CHUNK),
        grid=(batch, nblk),
        in_specs=[tok(GLA_QK), tok(GLA_QK), tok(GLA_V), tok(LANES), tok(GLA_V),
                  const((LANES, GLA_QK)), const((1, GLA_QK)), const((1, GLA_DV))],
        out_specs=tok(GLA_V),
        out_shape=jax.ShapeDtypeStruct((t, GLA_V), BF16),
        scratch_shapes=[pltpu.VMEM((GLA_V, GLA_QK), F32)],
        compiler_params=pltpu.CompilerParams(
            dimension_semantics=("parallel", "arbitrary"), vmem_limit_bytes=VMEM_LIMIT),
        name="gla",
    )(q, k, v, glr, gout, w2p, bg, nw)


def _ssd_kernel(z_ref, xbc_ref, dt_ref, cw_ref, cb_ref, dtb_ref, alog_ref, dskip_ref, nw_ref,
                o_ref, xs_ref, st_ref, *, block_l):
    c_len = SSD_CHUNK
    halo = SUBLANES

    @pl.when(pl.program_id(1) == 0)
    def _():
        xs_ref[0:halo, :] = jnp.zeros((halo, SSD_CONV_DIM), F32)
        st_ref[...] = jnp.zeros_like(st_ref)

    xs_ref[halo:halo + block_l, :] = xbc_ref[...]

    row = lax.broadcasted_iota(jnp.int32, (c_len, c_len), 0)
    col = lax.broadcasted_iota(jnp.int32, (c_len, c_len), 1)
    causal = row >= col
    tri = causal.astype(F32)
    first_half = col < SSD_N
    cw = cw_ref[...]
    cb = cb_ref[...]
    a_neg = -jnp.exp(alog_ref[...])
    dtb = dtb_ref[...]

    def chunk(c, carry):
        r0 = pl.multiple_of(c * c_len, c_len)
        rows = pl.ds(r0, c_len)
        window = xs_ref[pl.ds(r0, c_len + halo), :]
        conv = cb
        for tap in range(SSD_CONV):
            shift = halo - (SSD_CONV - 1) + tap
            conv = conv + cw[tap:tap + 1, :] * window[shift:shift + c_len, :]
        xc = _silu(conv)
        b_m = xc[:, SSD_WIDTH:SSD_WIDTH + SSD_BC]
        c_m = xc[:, SSD_WIDTH + SSD_BC:]

        dt = _softplus(dt_ref[rows, :] + dtb)
        a_cum = _dot(tri, dt * a_neg, precision=HIGHEST)
        a_cum_t = a_cum.T

        scores = []
        c_g = []
        for g in range(SSD_GROUPS):
            in_group = first_half if g == 0 else jnp.logical_not(first_half)
            c_g.append(jnp.where(in_group, c_m, 0.0).astype(BF16))
            scores.append(_dot_nt(c_g[g], b_m.astype(BF16)))

        for m in range(SSD_HEADS // 2):
            g = (2 * m) // (SSD_HEADS // SSD_GROUPS)
            lanes = slice(m * LANES, (m + 1) * LANES)
            x_pair = xc[:, lanes]
            halves = []
            dts = []
            for par in range(2):
                h = 2 * m + par
                dts.append(jnp.broadcast_to(dt[:, h:h + 1], (c_len, LANES)))
            xdt = (x_pair * jnp.where(first_half, dts[0], dts[1])).astype(BF16)
            for par in range(2):
                h = 2 * m + par
                a_col = jnp.broadcast_to(a_cum[:, h:h + 1], (c_len, c_len))
                a_row = jnp.broadcast_to(a_cum_t[h:h + 1, :], (c_len, c_len))
                a_end = a_col[c_len - 1:c_len, :]
                decay = jnp.exp(jnp.where(causal, a_col - a_row, -jnp.inf))
                y_diag = _dot((scores[g] * decay).astype(BF16), xdt)
                state = st_ref[h]
                y_off = _dot(c_g[g], state.astype(BF16)) * jnp.exp(a_col)
                halves.append(y_diag + y_off)
                b_dec = (b_m * jnp.exp(a_end - a_col)).astype(BF16)
                st_ref[h] = state * jnp.exp(a_end) + _dot_tn(b_dec, xdt)
            y = jnp.where(first_half, halves[0], halves[1]) + x_pair * dskip_ref[:, lanes]
            y = y * _silu(z_ref[rows, lanes])
            xs_pair_sq = jnp.sum(y * y, axis=-1, keepdims=True)
            if m % 2 == 0:
                y_prev, sq_prev = y, xs_pair_sq
            else:
                scale = lax.rsqrt((sq_prev + xs_pair_sq) * (1.0 / (2 * LANES)) + EPS)
                lo = slice((m - 1) * LANES, m * LANES)
                o_ref[rows, lo] = (y_prev * scale * nw_ref[:, lo]).astype(o_ref.dtype)
                o_ref[rows, lanes] = (y * scale * nw_ref[:, lanes]).astype(o_ref.dtype)
        return carry

    lax.fori_loop(0, block_l // c_len, chunk, 0)
    xs_ref[0:halo, :] = xs_ref[block_l:block_l + halo, :]


def _ssd(z, xbc, dt, cw, cb, dtb, alog, dskip, nw, batch, seq, block_l):
    t = batch * seq
    nblk = seq // block_l
    tok = lambda w: pl.BlockSpec((block_l, w), lambda b, j: (b * nblk + j, 0))
    const = lambda shape: pl.BlockSpec(shape, lambda b, j: (0, 0))
    return pl.pallas_call(
        functools.partial(_ssd_kernel, block_l=block_l),
        grid=(batch, nblk),
        in_specs=[tok(SSD_WIDTH), tok(SSD_CONV_DIM), tok(LANES),
                  const((SSD_CONV, SSD_CONV_DIM)), const((1, SSD_CONV_DIM)),
                  const((1, LANES)), const((1, LANES)), const((1, SSD_WIDTH)),
                  const((1, SSD_WIDTH))],
        out_specs=tok(SSD_WIDTH),
        out_shape=jax.ShapeDtypeStruct((t, SSD_WIDTH), BF16),
        scratch_shapes=[pltpu.VMEM((block_l + SUBLANES, SSD_CONV_DIM), F32),
                        pltpu.VMEM((SSD_HEADS, SSD_BC, LANES), F32)],
        compiler_params=pltpu.CompilerParams(
            dimension_semantics=("parallel", "arbitrary"), vmem_limit_bytes=VMEM_LIMIT),
        name="ssd",
    )(z, xbc, dt, cw, cb, dtb, alog, dskip, nw)


def _out_query_kernel(og_ref, os_ref, x_ref, wo_ref, nw_ref, wq_ref, keys_ref,
                      h_ref, n_ref, s_ref, q_ref):
    mixed = _dot(og_ref[...], wo_ref[0:GLA_V, :]) + _dot(os_ref[...], wo_ref[GLA_V:, :])
    h = x_ref[...] + mixed
    h_ref[...] = h
    n = (h * _rms_scale(h) * nw_ref[...]).astype(BF16)
    n_ref[...] = n
    q_ref[...] = _dot(n, wq_ref[...]).astype(BF16)
    for hp in range(2 * PEER_HEADS):
        lanes = slice(hp * PEER_HALF, (hp + 1) * PEER_HALF)
        s_ref[hp] = _dot_nt(keys_ref[hp], q_ref[:, lanes])


def _out_query(o_gla, o_ssd, x2, w_out, norm_w, w_query, keys, block_t):
    t = x2.shape[0]
    n_hp = 2 * PEER_HEADS
    return pl.pallas_call(
        _out_query_kernel,
        grid=(t // block_t,),
        in_specs=[pl.BlockSpec((block_t, GLA_V), lambda i: (i, 0)),
                  pl.BlockSpec((block_t, SSD_WIDTH), lambda i: (i, 0)),
                  pl.BlockSpec((block_t, D_MODEL), lambda i: (i, 0)),
                  pl.BlockSpec((D_MODEL, D_MODEL), lambda i: (0, 0)),
                  pl.BlockSpec((1, D_MODEL), lambda i: (0, 0)),
                  pl.BlockSpec((D_MODEL, n_hp * PEER_HALF), lambda i: (0, 0)),
                  pl.BlockSpec((n_hp, PEER_KEYS, PEER_HALF), lambda i: (0, 0, 0))],
        out_specs=[pl.BlockSpec((block_t, D_MODEL), lambda i: (i, 0)),
                   pl.BlockSpec((block_t, D_MODEL), lambda i: (i, 0)),
                   pl.BlockSpec((n_hp, PEER_KEYS, block_t), lambda i: (0, 0, i))],
        out_shape=[jax.ShapeDtypeStruct((t, D_MODEL), F32),
                   jax.ShapeDtypeStruct((t, D_MODEL), BF16),
                   jax.ShapeDtypeStruct((n_hp, PEER_KEYS, t), F32)],
        scratch_shapes=[pltpu.VMEM((block_t, n_hp * PEER_HALF), BF16)],
        compiler_params=pltpu.CompilerParams(
            dimension_semantics=("parallel",), vmem_limit_bytes=VMEM_LIMIT),
        name="out_query",
    )(o_gla, o_ssd, x2, w_out, norm_w, w_query, keys)


def _extract_top(values, ids, count):
    lanes = values.shape[1]
    slot = lax.broadcasted_iota(jnp.int32, (count, lanes), 0)
    top_v = jnp.zeros((count, lanes), F32)
    top_i = jnp.zeros((count, lanes), F32)
    for r in range(count):
        m = jnp.max(values, axis=0, keepdims=True)
        sel = jnp.min(jnp.where(values == m, ids, jnp.inf), axis=0, keepdims=True)
        values = jnp.where(ids == sel, -jnp.inf, values)
        top_v = jnp.where(slot == r, m, top_v)
        top_i = jnp.where(slot == r, sel, top_i)
    return top_v, top_i, jnp.zeros((1, lanes), F32)


def _sorting_network(n):
    size = 1 << (n - 1).bit_length()
    pairs = []
    p = 1
    while p < size:
        k = p
        while k >= 1:
            for j in range(k % p, size - k, 2 * k):
                for i in range(min(k, size - j - k)):
                    if (i + j) // (2 * p) == (i + j + k) // (2 * p):
                        pairs.append((i + j, i + j + k))
            k //= 2
        p *= 2
    return [(a, b) for a, b in pairs if b < n]


def _pop_top(values, ids, count):
    rows, lanes = values.shape
    n = rows // SUBLANES
    vals = [values[v * SUBLANES:(v + 1) * SUBLANES] for v in range(n)]
    idl = [ids[v * SUBLANES:(v + 1) * SUBLANES] for v in range(n)]
    for a, b in _sorting_network(n):
        swap = vals[b] > vals[a]
        vals[a], vals[b] = jnp.where(swap, vals[b], vals[a]), jnp.where(swap, vals[a], vals[b])
        idl[a], idl[b] = jnp.where(swap, idl[b], idl[a]), jnp.where(swap, idl[a], idl[b])

    slot = lax.broadcasted_iota(jnp.int32, (count, lanes), 0)
    top_v = jnp.zeros((count, lanes), F32)
    top_i = jnp.zeros((count, lanes), F32)
    tie = jnp.zeros((1, lanes), F32)
    prev = None
    for r in range(count + 1):
        m = jnp.max(vals[0], axis=0, keepdims=True)
        if prev is not None:
            tie = jnp.where(m == prev, 1.0, tie)
        prev = m
        if r == count:
            break
        sel = jnp.min(jnp.where(vals[0] == m, idl[0], jnp.inf), axis=0, keepdims=True)
        top_v = jnp.where(slot == r, m, top_v)
        top_i = jnp.where(slot == r, sel, top_i)
        hit = idl[0] == sel
        depth = min(n - 1, count - r)
        for v in range(depth):
            vals[v] = jnp.where(hit, vals[v + 1], vals[v])
            idl[v] = jnp.where(hit, idl[v + 1], idl[v])
        vals[depth] = jnp.where(hit, -jnp.inf, vals[depth])
    return top_v, top_i, tie


def _gather_rows(table, index):
    row = lax.broadcasted_iota(jnp.int32, table.shape, 0).astype(F32)
    return jnp.sum(jnp.where(row == index, table, 0.0), axis=0, keepdims=True)


def _head_entries(s_ref, h, top_fn):
    k = PEER_TOPK
    lanes = LANES
    key_id = lax.broadcasted_iota(jnp.int32, (PEER_KEYS, lanes), 0).astype(F32)
    sub = lax.broadcasted_iota(jnp.int32, (SUBLANES, lanes), 0).astype(F32)
    slot = lax.broadcasted_iota(jnp.int32, (k, lanes), 0)

    s1, i1, tie1 = top_fn(s_ref[2 * h], key_id, k)
    s2, i2, tie2 = top_fn(s_ref[2 * h + 1], key_id, k)
    pieces, flats = [], []
    for b0 in (0, SUBLANES):
        pieces.append(s1[0:1, :] + s2[b0:b0 + SUBLANES, :])
        flats.append(sub + float(b0))
    for a in range(1, SUBLANES):
        limit = k // (a + 1)
        pieces.append(jnp.where(sub < float(limit), s1[a:a + 1, :] + s2[0:SUBLANES, :], -jnp.inf))
        flats.append(sub + float(a * k))
    pieces.append(s1[SUBLANES:k, :] + s2[0:1, :])
    flats.append((sub + float(SUBLANES)) * float(k))
    cand = jnp.concatenate(pieces, axis=0)
    flat = jnp.concatenate(flats, axis=0)
    best_s, best_flat, tie3 = top_fn(cand, flat, k)

    ent_i = jnp.zeros((k, lanes), F32)
    ent_j = jnp.zeros((k, lanes), F32)
    for r in range(k):
        pos = best_flat[r:r + 1, :]
        a_idx = jnp.floor(pos * (1.0 / k))
        b_idx = pos - a_idx * float(k)
        ent_i = jnp.where(slot == r, _gather_rows(i1, a_idx), ent_i)
        ent_j = jnp.where(slot == r, _gather_rows(i2, b_idx), ent_j)
    e = jnp.exp(best_s - best_s[0:1, :])
    gate = e / jnp.sum(e, axis=0, keepdims=True)
    return ent_i, ent_j, gate, jnp.maximum(jnp.maximum(tie1, tie2), tie3)


def _topk_kernel(s_ref, w_ref, ei_ref, ej_ref, eg_ref, tie_ref, pi_ref, pj_ref, pg_ref):
    k = PEER_TOPK
    half_tile = LANES // 2

    @pl.when(pl.program_id(0) == 0)
    def _():
        pi_ref[...] = jnp.zeros_like(pi_ref)
        pj_ref[...] = jnp.zeros_like(pj_ref)
        pg_ref[...] = jnp.zeros_like(pg_ref)

    grid_row = lax.broadcasted_iota(jnp.int32, (PEER_KEYS, PEER_ENTRIES), 0).astype(F32)

    def gate_grid(t):
        i_row = pi_ref[pl.ds(t, 1), :]
        j_row = pj_ref[pl.ds(t, 1), :]
        g_row = pg_ref[pl.ds(t, 1), :]
        a_t = jnp.where(grid_row == i_row, g_row, 0.0).astype(BF16)
        b_t = jnp.where(grid_row == j_row, 1.0, 0.0).astype(BF16)
        return _dot_nt(a_t, b_t)

    def store(h, ent_i, ent_j, gate):
        rows = pl.ds(pl.multiple_of(h * k, k), k)
        ei_ref[rows, :] = ent_i
        ej_ref[rows, :] = ent_j
        eg_ref[rows, :] = gate

    def fast_head(h, carry):
        ent_i, ent_j, gate, tie = _head_entries(s_ref, h, _pop_top)
        store(h, ent_i, ent_j, gate)
        tie_ref[pl.ds(h, 1), :] = tie
        for n in range(half_tile // PEER_HEADS):
            q = h * (half_tile // PEER_HEADS) + n
            packed = pltpu.pack_elementwise([gate_grid(q), gate_grid(q + half_tile)],
                                            packed_dtype=BF16)
            words = lax.bitcast_convert_type(packed, jnp.uint32)
            rows = pl.ds(pl.multiple_of(q * PEER_GRID_ROWS, PEER_GRID_ROWS), PEER_GRID_ROWS)
            for e in range(PEER_KEYS // PEER_GRID_ROWS):
                w_ref[e, rows, :] = words[e * PEER_GRID_ROWS:(e + 1) * PEER_GRID_ROWS, :]
        return carry

    def exact_head(h, carry):
        @pl.when(jnp.max(tie_ref[pl.ds(h, 1), :]) > 0.0)
        def _():
            ent_i, ent_j, gate, _ = _head_entries(s_ref, h, _extract_top)
            store(h, ent_i, ent_j, gate)
        return carry

    lax.fori_loop(0, PEER_HEADS, fast_head, 0, unroll=2)

    @pl.when(jnp.max(tie_ref[...]) > 0.0)
    def _():
        lax.fori_loop(0, PEER_HEADS, exact_head, 0)

    pi_ref[...] = ei_ref[...].T
    pj_ref[...] = ej_ref[...].T
    pg_ref[...] = eg_ref[...].T


def _topk(scores_t):
    n_hp, n_keys, t = scores_t.shape
    n_tiles = t // LANES
    tiles_per_block = PEER_BLOCK_T // LANES
    e_steps = PEER_KEYS // PEER_GRID_ROWS
    tile_rows = LANES // 2 * PEER_GRID_ROWS

    def out_index(g):
        tile = jnp.maximum(g - 1, 0)
        return (tile // tiles_per_block, 0, tile % tiles_per_block, 0)

    return pl.pallas_call(
        _topk_kernel,
        grid=(n_tiles + 1,),
        in_specs=[pl.BlockSpec((n_hp, n_keys, LANES),
                               lambda g: (0, 0, jnp.minimum(g, n_tiles - 1)))],
        out_specs=pl.BlockSpec((None, e_steps, tile_rows, PEER_KEYS), out_index),
        out_shape=jax.ShapeDtypeStruct(
            (t // PEER_BLOCK_T, e_steps, tiles_per_block * tile_rows, PEER_KEYS), jnp.uint32),
        scratch_shapes=[pltpu.VMEM((PEER_ENTRIES, LANES), F32)] * 3
        + [pltpu.VMEM((PEER_HEADS, LANES), F32)]
        + [pltpu.VMEM((LANES, PEER_ENTRIES), F32)] * 3,
        compiler_params=pltpu.CompilerParams(
            dimension_semantics=("arbitrary",), vmem_limit_bytes=VMEM_LIMIT),
        name="topk",
    )(scores_t)


PAIR = 2 * PEER_KEYS


def _gelu(x):
    return 0.5 * x * (1.0 + lax.erf(x * (2.0 ** -0.5)))


def _peer_kernel(n_ref, w_ref, u_ref, v_ref, h_ref, nw_ref, o_ref, hid_ref, acc_ref,
                 *, block_t, block_e):
    e_step = pl.program_id(1)
    n_pairs = block_e // PAIR
    grid_rows = block_e // PEER_KEYS
    half_tile = LANES // 2

    @pl.when(e_step == 0)
    def _():
        acc_ref[...] = jnp.zeros_like(acc_ref)

    def gate_rows(i):
        words = w_ref[pl.ds(i, block_t // 2, stride=grid_rows), :]
        lo, hi = [pltpu.unpack_elementwise(words, index=k, packed_dtype=BF16, unpacked_dtype=F32)
                  for k in range(2)]
        pieces = []
        for tile in range(block_t // LANES):
            rows = slice(tile * half_tile, (tile + 1) * half_tile)
            pieces += [lo[rows], hi[rows]]
        return jnp.concatenate(pieces, axis=0)

    n = n_ref[...]
    for p in range(n_pairs):
        experts = slice(p * PAIR, (p + 1) * PAIR)
        act = _dot_nt(n, u_ref[experts, :])
        w_pair = jnp.concatenate([gate_rows(2 * p), gate_rows(2 * p + 1)], axis=-1)
        hid_ref[:, experts] = (_gelu(act) * w_pair).astype(BF16)
    acc_ref[...] += _dot(hid_ref[...], v_ref[...])

    @pl.when(e_step == pl.num_programs(1) - 1)
    def _():
        h = h_ref[...] + acc_ref[...]
        o_ref[...] = h * _rms_scale(h) * nw_ref[...]


def _peer(n2, w_packed, u, v, h1, norm_w, block_t, block_e):
    t = n2.shape[0]
    n_exp = u.shape[0]
    tok = lambda w: pl.BlockSpec((block_t, w), lambda i, e: (i, 0))
    return pl.pallas_call(
        functools.partial(_peer_kernel, block_t=block_t, block_e=block_e),
        grid=(t // block_t, n_exp // block_e),
        in_specs=[tok(D_MODEL),
                  pl.BlockSpec((None, None, block_t // 2 * (block_e // PEER_KEYS), PEER_KEYS),
                               lambda i, e: (i, e, 0, 0)),
                  pl.BlockSpec((block_e, D_MODEL), lambda i, e: (e, 0)),
                  pl.BlockSpec((block_e, D_MODEL), lambda i, e: (e, 0)),
                  tok(D_MODEL),
                  pl.BlockSpec((1, D_MODEL), lambda i, e: (0, 0))],
        out_specs=tok(D_MODEL),
        out_shape=jax.ShapeDtypeStruct((t, D_MODEL), F32),
        scratch_shapes=[pltpu.VMEM((block_t, block_e), BF16),
                        pltpu.VMEM((block_t, D_MODEL), F32)],
        compiler_params=pltpu.CompilerParams(
            dimension_semantics=("parallel", "arbitrary"), vmem_limit_bytes=VMEM_LIMIT),
        name="peer",
    )(n2, w_packed, u, v, h1, norm_w)


def _pad_cols(w, width):
    return jnp.pad(w, ((0, 0), (0, width - w.shape[1])))


def _in_proj_slabs(w_in):
    parts, start = [], 0
    for size in _IN_SIZES:
        parts.append(w_in[:, start:start + size])
        start += size
    q, k, v, gate_lr, g_out, z, xbc, dt = parts
    return jnp.concatenate(
        [q, k, v, g_out, z, xbc, _pad_cols(gate_lr, LANES), _pad_cols(dt, LANES)],
        axis=1).astype(BF16)


def _layer(x2, batch, seq, p):
    row = lambda a: a.reshape(1, -1).astype(F32)
    q, k, v, g_out, z, xbc, gate_lr, dt = _in_proj(
        x2, row(p["norm_mix_w"]), _in_proj_slabs(p["w_in"]), block_t=512)

    w2p = jnp.pad(p["gla_w_gate2"].astype(F32), ((0, LANES - GLA_RANK), (0, 0)))
    o_gla = _gla(q, k, v, gate_lr, g_out, w2p, row(p["gla_b_gate"]), row(p["gla_norm_w"]),
                 batch, seq, block_l=256)

    pad_heads = lambda a: _pad_cols(row(a), LANES)
    d_skip = jnp.repeat(p["ssd_d"].astype(F32), SSD_P).reshape(1, SSD_WIDTH)
    o_ssd = _ssd(z, xbc, dt, p["ssd_conv_w"].astype(F32), row(p["ssd_conv_b"]),
                 pad_heads(p["ssd_dt_bias"]), pad_heads(p["ssd_a_log"]), d_skip,
                 row(p["ssd_norm_w"]), batch, seq, block_l=512)

    keys = p["peer_sub_keys"].reshape(2 * PEER_HEADS, PEER_KEYS, PEER_HALF).astype(BF16)
    h1, n2, scores_t = _out_query(
        o_gla, o_ssd, x2, p["w_out"].astype(BF16), row(p["norm_ffn_w"]),
        p["peer_w_query"].astype(BF16), keys, block_t=512)

    return n2, _topk(scores_t), h1


def kernel(x, norm_mix_w, w_in, gla_w_gate2, gla_b_gate, gla_norm_w, ssd_conv_w, ssd_conv_b,
           ssd_dt_bias, ssd_a_log, ssd_d, ssd_norm_w, w_out, norm_ffn_w, peer_w_query,
           peer_sub_keys, peer_u, peer_v, norm_final_w):
    batch, seq, d = x.shape
    assert w_in.shape[0] == 1, "single-layer trunk"
    params = dict(norm_mix_w=norm_mix_w, w_in=w_in, gla_w_gate2=gla_w_gate2, gla_b_gate=gla_b_gate,
                  gla_norm_w=gla_norm_w, ssd_conv_w=ssd_conv_w, ssd_conv_b=ssd_conv_b,
                  ssd_dt_bias=ssd_dt_bias, ssd_a_log=ssd_a_log, ssd_d=ssd_d, ssd_norm_w=ssd_norm_w,
                  w_out=w_out, norm_ffn_w=norm_ffn_w, peer_w_query=peer_w_query,
                  peer_sub_keys=peer_sub_keys)
    p = {name: value[0] for name, value in params.items()}
    x2 = x.reshape(batch * seq, d).astype(F32)
    n2, w_packed, h1 = _layer(x2, batch, seq, p)
    y = _peer(n2, w_packed, peer_u[0].astype(BF16), peer_v[0].astype(BF16), h1,
              norm_final_w.reshape(1, d).astype(F32), block_t=PEER_BLOCK_T, block_e=PEER_BLOCK_E)
    return y.reshape(batch, seq, d).astype(x.dtype)
```

```python
import functools

import jax
import jax.numpy as jnp
from jax import lax
from jax.experimental import pallas as pl
from jax.experimental.pallas import tpu as pltpu

F32 = jnp.float32
BF16 = jnp.bfloat16
HIGHEST = lax.Precision.HIGHEST

EPS = 1e-6
D_MODEL = 1024

GLA_HEADS = 4
GLA_DK = 64
GLA_DV = 128
GLA_QK = GLA_HEADS * GLA_DK
GLA_V = GLA_HEADS * GLA_DV
GLA_RANK = 16
GLA_NORMALIZER = 16.0
GLA_CHUNK = 64

SSD_HEADS = 8
SSD_P = 64
SSD_WIDTH = SSD_HEADS * SSD_P
SSD_GROUPS = 2
SSD_N = 64
SSD_CONV = 4
SSD_CHUNK = 128
SSD_BC = SSD_GROUPS * SSD_N
SSD_CONV_DIM = SSD_WIDTH + 2 * SSD_BC

PEER_HEADS = 8
PEER_KEYS = 128
PEER_TOPK = 16
PEER_HALF = 128
PEER_ENTRIES = PEER_HEADS * PEER_TOPK
PEER_BLOCK_T = 512
PEER_BLOCK_E = 2048
PEER_GRID_ROWS = PEER_BLOCK_E // PEER_KEYS

LANES = 128
SUBLANES = 8
VMEM_LIMIT = 56 * 1024 * 1024

_IN_SIZES = (GLA_QK, GLA_QK, GLA_V, GLA_RANK, GLA_V, SSD_WIDTH, SSD_CONV_DIM, SSD_HEADS)
_SLAB_WIDTHS = (GLA_QK, GLA_QK, GLA_V, GLA_V, SSD_WIDTH, SSD_CONV_DIM, LANES, LANES)


def _dot(a, b, dims=((1,), (0,)), precision=None):
    return lax.dot_general(a, b, (dims, ((), ())), precision=precision,
                           preferred_element_type=F32)


def _dot_nt(a, b, precision=None):
    return _dot(a, b, ((1,), (1,)), precision)


def _dot_tn(a, b, precision=None):
    return _dot(a, b, ((0,), (0,)), precision)


def _silu(x):
    return x * (1.0 / (1.0 + jnp.exp(-x)))


def _softplus(x):
    return jnp.maximum(x, 0.0) + jnp.log(1.0 + jnp.exp(-jnp.abs(x)))


def _rms_scale(x):
    return lax.rsqrt(jnp.mean(x * x, axis=-1, keepdims=True) + EPS)


def _in_proj_kernel(x_ref, nw_ref, w_ref, *out_refs):
    x = x_ref[...]
    n = (x * _rms_scale(x) * nw_ref[...]).astype(BF16)
    start = 0
    for o_ref, width in zip(out_refs, _SLAB_WIDTHS):
        o_ref[...] = _dot(n, w_ref[:, start:start + width])
        start += width


def _in_proj(x2, norm_w, w_slabs, block_t):
    t = x2.shape[0]
    total = sum(_SLAB_WIDTHS)
    out_shape = [jax.ShapeDtypeStruct((t, w), F32) for w in _SLAB_WIDTHS]
    out_specs = [pl.BlockSpec((block_t, w), lambda i: (i, 0)) for w in _SLAB_WIDTHS]
    return pl.pallas_call(
        _in_proj_kernel,
        grid=(t // block_t,),
        in_specs=[pl.BlockSpec((block_t, D_MODEL), lambda i: (i, 0)),
                  pl.BlockSpec((1, D_MODEL), lambda i: (0, 0)),
                  pl.BlockSpec((D_MODEL, total), lambda i: (0, 0))],
        out_specs=out_specs,
        out_shape=out_shape,
        compiler_params=pltpu.CompilerParams(
            dimension_semantics=("parallel",), vmem_limit_bytes=VMEM_LIMIT),
        name="in_proj",
    )(x2, norm_w, w_slabs)


def _gla_kernel(q_ref, k_ref, v_ref, glr_ref, gout_ref, w2_ref, bg_ref, nw_ref, table_ref,
                o_ref, table_bf_ref, st_ref, *, n_chunks):
    c_len = GLA_CHUNK
    table_bf_ref[...] = table_ref[...].astype(BF16)

    @pl.when(pl.program_id(1) == 0)
    def _():
        st_ref[...] = jnp.zeros_like(st_ref)

    def iota(shape, dim):
        return lax.broadcasted_iota(jnp.int32, shape, dim)

    block_l = n_chunks * c_len
    gate = _dot(glr_ref[...], w2_ref[...], precision=HIGHEST) + bg_ref[...]
    log_a = -_softplus(-gate) * (1.0 / GLA_NORMALIZER)
    tri = ((iota((block_l, block_l), 0) // c_len == iota((block_l, block_l), 1) // c_len)
           & (iota((block_l, block_l), 0) >= iota((block_l, block_l), 1))).astype(F32)
    g_cum = _dot(tri, log_a, precision=HIGHEST)
    g_end = [g_cum[(c + 1) * c_len - 1:(c + 1) * c_len, :] for c in range(n_chunks)]
    g_last = jnp.concatenate([jnp.broadcast_to(g, (c_len, GLA_QK)) for g in g_end], axis=0)
    k = k_ref[...]
    q_in = (q_ref[...] * (GLA_DK ** -0.5) * jnp.exp(g_cum)).astype(BF16)
    k_in = k * jnp.exp(-g_cum)
    k_end = (k * jnp.exp(g_last - g_cum)).astype(BF16)
    v = v_ref[...]

    k_bd_mask = iota((GLA_QK, GLA_QK), 0) // c_len == iota((GLA_QK, GLA_QK), 1) // GLA_DK
    v_bd_mask = iota((GLA_QK, GLA_V), 0) // c_len == iota((GLA_QK, GLA_V), 1) // GLA_DV
    causal = iota((c_len, GLA_QK), 0) >= iota((c_len, GLA_QK), 1) % c_len
    state_mask = iota((GLA_V, GLA_QK), 0) // GLA_DV == iota((GLA_V, GLA_QK), 1) // GLA_DK

    att = []
    for c in range(n_chunks):
        rows = slice(c * c_len, (c + 1) * c_len)
        k_bd = jnp.where(k_bd_mask, jnp.concatenate([k_in[rows]] * GLA_HEADS, axis=0), 0.0)
        scores = _dot_nt(q_in[rows], k_bd.astype(BF16))
        att.append(jnp.where(causal, scores, 0.0).astype(BF16))

    state_t = st_ref[...]
    nw = jnp.concatenate([nw_ref[...]] * GLA_HEADS, axis=1)
    for c in range(n_chunks):
        rows = slice(c * c_len, (c + 1) * c_len)
        v_c = v[rows]
        new_state = _dot_tn(v_c.astype(BF16), k_end[rows])
        o_inter = _dot_nt(q_in[rows], state_t.astype(BF16))
        state_t = state_t * jnp.exp(g_end[c]) + jnp.where(state_mask, new_state, 0.0)
        v_bd = jnp.where(v_bd_mask, jnp.concatenate([v_c] * GLA_HEADS, axis=0), 0.0)
        o = _dot(att[c], v_bd.astype(BF16)) + o_inter
        outs = []
        for h in range(GLA_HEADS):
            o_h = o[:, h * GLA_DV:(h + 1) * GLA_DV]
            outs.append(o_h * _rms_scale(o_h))
        o = jnp.concatenate(outs, axis=1) * nw
        o_ref[rows, :] = (o * _silu(gout_ref[rows, :])).astype(o_ref.dtype)
    st_ref[...] = state_t


def _table_slab_spec(table, batch, nblk):
    rows, width = table.shape
    slab = rows // (batch * nblk)
    assert slab * batch * nblk == rows and slab % (2 * SUBLANES) == 0
    return pl.BlockSpec((slab, width), lambda b, j: (b * nblk + j, 0))


def _gla(q, k, v, glr, gout, w2p, bg, nw, table, batch, seq, block_l):
    t = batch * seq
    nblk = seq // block_l
    tok = lambda w: pl.BlockSpec((block_l, w), lambda b, j: (b * nblk + j, 0))
    const = lambda shape: pl.BlockSpec(shape, lambda b, j: (0, 0))
    table_spec = _table_slab_spec(table, batch, nblk)
    return pl.pallas_call(
        functools.partial(_gla_kernel, n_chunks=block_l // GLA_CHUNK),
        grid=(batch, nblk),
        in_specs=[tok(GLA_QK), tok(GLA_QK), tok(GLA_V), tok(LANES), tok(GLA_V),
                  const((LANES, GLA_QK)), const((1, GLA_QK)), const((1, GLA_DV)), table_spec],
        out_specs=[tok(GLA_V), table_spec],
        out_shape=[jax.ShapeDtypeStruct((t, GLA_V), BF16),
                   jax.ShapeDtypeStruct(table.shape, BF16)],
        scratch_shapes=[pltpu.VMEM((GLA_V, GLA_QK), F32)],
        compiler_params=pltpu.CompilerParams(
            dimension_semantics=("parallel", "arbitrary"), vmem_limit_bytes=VMEM_LIMIT),
        name="gla",
    )(q, k, v, glr, gout, w2p, bg, nw, table)


def _ssd_kernel(z_ref, xbc_ref, dt_ref, cw_ref, cb_ref, dtb_ref, alog_ref, dskip_ref, nw_ref,
                table_ref, o_ref, table_bf_ref, xs_ref, st_ref, *, block_l):
    c_len = SSD_CHUNK
    halo = SUBLANES
    table_bf_ref[...] = table_ref[...].astype(BF16)

    @pl.when(pl.program_id(1) == 0)
    def _():
        xs_ref[0:halo, :] = jnp.zeros((halo, SSD_CONV_DIM), F32)
        st_ref[...] = jnp.zeros_like(st_ref)

    xs_ref[halo:halo + block_l, :] = xbc_ref[...]

    row = lax.broadcasted_iota(jnp.int32, (c_len, c_len), 0)
    col = lax.broadcasted_iota(jnp.int32, (c_len, c_len), 1)
    causal = row >= col
    upper = (row <= col).astype(F32)
    first_half = col < SSD_N
    cw = cw_ref[...]
    cb = cb_ref[...]
    a_neg = -jnp.exp(alog_ref[...])
    dtb = dtb_ref[...]
    head_pad = jnp.zeros((LANES - SSD_HEADS, c_len), F32)

    def chunk(c, carry):
        r0 = pl.multiple_of(c * c_len, c_len)
        rows = pl.ds(r0, c_len)
        window = xs_ref[pl.ds(r0, c_len + halo), :]
        conv = cb
        for tap in range(SSD_CONV):
            shift = halo - (SSD_CONV - 1) + tap
            conv = conv + cw[tap:tap + 1, :] * window[shift:shift + c_len, :]
        xc = _silu(conv)
        b_m = xc[:, SSD_WIDTH:SSD_WIDTH + SSD_BC]
        c_m = xc[:, SSD_WIDTH + SSD_BC:]

        dt_t = _softplus(dt_ref[rows, :].T[0:SSD_HEADS, :] + dtb)
        a_cum_t = _dot(dt_t * a_neg, upper, precision=HIGHEST)
        dt = jnp.concatenate([dt_t, head_pad], axis=0).T
        a_cum = jnp.concatenate([a_cum_t, head_pad], axis=0).T

        scores = []
        c_g = []
        for g in range(SSD_GROUPS):
            in_group = first_half if g == 0 else jnp.logical_not(first_half)
            c_g.append(jnp.where(in_group, c_m, 0.0).astype(BF16))
            scores.append(_dot_nt(c_g[g], b_m.astype(BF16)))

        for m in range(SSD_HEADS // 2):
            g = (2 * m) // (SSD_HEADS // SSD_GROUPS)
            lanes = slice(m * LANES, (m + 1) * LANES)
            x_pair = xc[:, lanes]
            halves = []
            dts = []
            for par in range(2):
                h = 2 * m + par
                dts.append(jnp.broadcast_to(dt[:, h:h + 1], (c_len, LANES)))
            xdt = (x_pair * jnp.where(first_half, dts[0], dts[1])).astype(BF16)
            for par in range(2):
                h = 2 * m + par
                a_col = jnp.broadcast_to(a_cum[:, h:h + 1], (c_len, c_len))
                a_row = jnp.broadcast_to(a_cum_t[h:h + 1, :], (c_len, c_len))
                a_end = a_col[c_len - 1:c_len, :]
                decay = jnp.exp(jnp.where(causal, a_col - a_row, -jnp.inf))
                y_diag = _dot((scores[g] * decay).astype(BF16), xdt)
                state = st_ref[h]
                y_off = _dot(c_g[g], state.astype(BF16)) * jnp.exp(a_col)
                halves.append(y_diag + y_off)
                b_dec = (b_m * jnp.exp(a_end - a_col)).astype(BF16)
                st_ref[h] = state * jnp.exp(a_end) + _dot_tn(b_dec, xdt)
            y = jnp.where(first_half, halves[0], halves[1]) + x_pair * dskip_ref[:, lanes]
            y = y * _silu(z_ref[rows, lanes])
            xs_pair_sq = jnp.sum(y * y, axis=-1, keepdims=True)
            if m % 2 == 0:
                y_prev, sq_prev = y, xs_pair_sq
            else:
                scale = lax.rsqrt((sq_prev + xs_pair_sq) * (1.0 / (2 * LANES)) + EPS)
                lo = slice((m - 1) * LANES, m * LANES)
                o_ref[rows, lo] = (y_prev * scale * nw_ref[:, lo]).astype(o_ref.dtype)
                o_ref[rows, lanes] = (y * scale * nw_ref[:, lanes]).astype(o_ref.dtype)
        return carry

    lax.fori_loop(0, block_l // c_len, chunk, 0)
    xs_ref[0:halo, :] = xs_ref[block_l:block_l + halo, :]


def _ssd(z, xbc, dt, cw, cb, dtb, alog, dskip, nw, table, batch, seq, block_l):
    t = batch * seq
    nblk = seq // block_l
    tok = lambda w: pl.BlockSpec((block_l, w), lambda b, j: (b * nblk + j, 0))
    const = lambda shape: pl.BlockSpec(shape, lambda b, j: (0, 0))
    table_spec = _table_slab_spec(table, batch, nblk)
    return pl.pallas_call(
        functools.partial(_ssd_kernel, block_l=block_l),
        grid=(batch, nblk),
        in_specs=[tok(SSD_WIDTH), tok(SSD_CONV_DIM), tok(LANES),
                  const((SSD_CONV, SSD_CONV_DIM)), const((1, SSD_CONV_DIM)),
                  const((SSD_HEADS, LANES)), const((SSD_HEADS, LANES)), const((1, SSD_WIDTH)),
                  const((1, SSD_WIDTH)), table_spec],
        out_specs=[tok(SSD_WIDTH), table_spec],
        out_shape=[jax.ShapeDtypeStruct((t, SSD_WIDTH), BF16),
                   jax.ShapeDtypeStruct(table.shape, BF16)],
        scratch_shapes=[pltpu.VMEM((block_l + SUBLANES, SSD_CONV_DIM), F32),
                        pltpu.VMEM((SSD_HEADS, SSD_BC, LANES), F32)],
        compiler_params=pltpu.CompilerParams(
            dimension_semantics=("parallel", "arbitrary"), vmem_limit_bytes=VMEM_LIMIT),
        name="ssd",
    )(z, xbc, dt, cw, cb, dtb, alog, dskip, nw, table)


def _out_query_kernel(og_ref, os_ref, x_ref, wo_ref, nw_ref, wq_ref, keys_ref,
                      h_ref, n_ref, s_ref, q_ref):
    mixed = _dot(og_ref[...], wo_ref[0:GLA_V, :]) + _dot(os_ref[...], wo_ref[GLA_V:, :])
    h = x_ref[...] + mixed
    h_ref[...] = h
    n = (h * _rms_scale(h) * nw_ref[...]).astype(BF16)
    n_ref[...] = n
    q_ref[...] = _dot(n, wq_ref[...]).astype(BF16)
    for hp in range(2 * PEER_HEADS):
        lanes = slice(hp * PEER_HALF, (hp + 1) * PEER_HALF)
        s_ref[hp] = _dot_nt(keys_ref[hp], q_ref[:, lanes])


def _out_query(o_gla, o_ssd, x2, w_out, norm_w, w_query, keys, block_t):
    t = x2.shape[0]
    n_hp = 2 * PEER_HEADS
    return pl.pallas_call(
        _out_query_kernel,
        grid=(t // block_t,),
        in_specs=[pl.BlockSpec((block_t, GLA_V), lambda i: (i, 0)),
                  pl.BlockSpec((block_t, SSD_WIDTH), lambda i: (i, 0)),
                  pl.BlockSpec((block_t, D_MODEL), lambda i: (i, 0)),
                  pl.BlockSpec((D_MODEL, D_MODEL), lambda i: (0, 0)),
                  pl.BlockSpec((1, D_MODEL), lambda i: (0, 0)),
                  pl.BlockSpec((D_MODEL, n_hp * PEER_HALF), lambda i: (0, 0)),
                  pl.BlockSpec((n_hp, PEER_KEYS, PEER_HALF), lambda i: (0, 0, 0))],
        out_specs=[pl.BlockSpec((block_t, D_MODEL), lambda i: (i, 0)),
                   pl.BlockSpec((block_t, D_MODEL), lambda i: (i, 0)),
                   pl.BlockSpec((n_hp, PEER_KEYS, block_t), lambda i: (0, 0, i))],
        out_shape=[jax.ShapeDtypeStruct((t, D_MODEL), F32),
                   jax.ShapeDtypeStruct((t, D_MODEL), BF16),
                   jax.ShapeDtypeStruct((n_hp, PEER_KEYS, t), F32)],
        scratch_shapes=[pltpu.VMEM((block_t, n_hp * PEER_HALF), BF16)],
        compiler_params=pltpu.CompilerParams(
            dimension_semantics=("parallel",), vmem_limit_bytes=VMEM_LIMIT),
        name="out_query",
    )(o_gla, o_ssd, x2, w_out, norm_w, w_query, keys)


def _extract_top(values, ids, count):
    lanes = values.shape[1]
    slot = lax.broadcasted_iota(jnp.int32, (count, lanes), 0)
    top_v = jnp.zeros((count, lanes), F32)
    top_i = jnp.zeros((count, lanes), F32)
    for r in range(count):
        m = jnp.max(values, axis=0, keepdims=True)
        sel = jnp.min(jnp.where(values == m, ids, jnp.inf), axis=0, keepdims=True)
        values = jnp.where(ids == sel, -jnp.inf, values)
        top_v = jnp.where(slot == r, m, top_v)
        top_i = jnp.where(slot == r, sel, top_i)
    return top_v, top_i, jnp.zeros((1, lanes), F32)


def _sorting_network(n):
    size = 1 << (n - 1).bit_length()
    pairs = []
    p = 1
    while p < size:
        k = p
        while k >= 1:
            for j in range(k % p, size - k, 2 * k):
                for i in range(min(k, size - j - k)):
                    if (i + j) // (2 * p) == (i + j + k) // (2 * p):
                        pairs.append((i + j, i + j + k))
            k //= 2
        p *= 2
    return [(a, b) for a, b in pairs if b < n]


def _pop_top(values, ids, count):
    rows, lanes = values.shape
    n = rows // SUBLANES
    vals = [values[v * SUBLANES:(v + 1) * SUBLANES] for v in range(n)]
    idl = [ids[v * SUBLANES:(v + 1) * SUBLANES] for v in range(n)]
    for a, b in _sorting_network(n):
        swap = vals[b] > vals[a]
        vals[a], vals[b] = jnp.where(swap, vals[b], vals[a]), jnp.where(swap, vals[a], vals[b])
        idl[a], idl[b] = jnp.where(swap, idl[b], idl[a]), jnp.where(swap, idl[a], idl[b])

    slot = lax.broadcasted_iota(jnp.int32, (count, lanes), 0)
    top_v = jnp.zeros((count, lanes), F32)
    top_i = jnp.zeros((count, lanes), F32)
    tie = jnp.zeros((1, lanes), F32)
    prev = None
    for r in range(count + 1):
        m = jnp.max(vals[0], axis=0, keepdims=True)
        if prev is not None:
            tie = jnp.where(m == prev, 1.0, tie)
        prev = m
        if r == count:
            break
        sel = jnp.min(jnp.where(vals[0] == m, idl[0], jnp.inf), axis=0, keepdims=True)
        top_v = jnp.where(slot == r, m, top_v)
        top_i = jnp.where(slot == r, sel, top_i)
        hit = idl[0] == sel
        depth = min(n - 1, count - r)
        for v in range(depth):
            vals[v] = jnp.where(hit, vals[v + 1], vals[v])
            idl[v] = jnp.where(hit, idl[v + 1], idl[v])
        vals[depth] = jnp.where(hit, -jnp.inf, vals[depth])
    return top_v, top_i, tie


def _gather_rows(table, index):
    row = lax.broadcasted_iota(jnp.int32, table.shape, 0).astype(F32)
    return jnp.sum(jnp.where(row == index, table, 0.0), axis=0, keepdims=True)


def _head_entries(s_ref, h, top_fn):
    k = PEER_TOPK
    lanes = LANES
    key_id = lax.broadcasted_iota(jnp.int32, (PEER_KEYS, lanes), 0).astype(F32)
    sub = lax.broadcasted_iota(jnp.int32, (SUBLANES, lanes), 0).astype(F32)
    slot = lax.broadcasted_iota(jnp.int32, (k, lanes), 0)

    s1, i1, tie1 = top_fn(s_ref[2 * h], key_id, k)
    s2, i2, tie2 = top_fn(s_ref[2 * h + 1], key_id, k)
    pieces, flats = [], []
    for b0 in (0, SUBLANES):
        pieces.append(s1[0:1, :] + s2[b0:b0 + SUBLANES, :])
        flats.append(sub + float(b0))
    for a in range(1, SUBLANES):
        limit = k // (a + 1)
        pieces.append(jnp.where(sub < float(limit), s1[a:a + 1, :] + s2[0:SUBLANES, :], -jnp.inf))
        flats.append(sub + float(a * k))
    pieces.append(s1[SUBLANES:k, :] + s2[0:1, :])
    flats.append((sub + float(SUBLANES)) * float(k))
    cand = jnp.concatenate(pieces, axis=0)
    flat = jnp.concatenate(flats, axis=0)
    best_s, best_flat, tie3 = top_fn(cand, flat, k)

    ent_i = jnp.zeros((k, lanes), F32)
    ent_j = jnp.zeros((k, lanes), F32)
    for r in range(k):
        pos = best_flat[r:r + 1, :]
        a_idx = jnp.floor(pos * (1.0 / k))
        b_idx = pos - a_idx * float(k)
        ent_i = jnp.where(slot == r, _gather_rows(i1, a_idx), ent_i)
        ent_j = jnp.where(slot == r, _gather_rows(i2, b_idx), ent_j)
    e = jnp.exp(best_s - best_s[0:1, :])
    gate = e / jnp.sum(e, axis=0, keepdims=True)
    return ent_i, ent_j, gate, jnp.maximum(jnp.maximum(tie1, tie2), tie3)


def _topk_kernel(s_ref, w_ref, ei_ref, ej_ref, eg_ref, tie_ref, pi_ref, pj_ref, pg_ref):
    k = PEER_TOPK
    half_tile = LANES // 2

    @pl.when(pl.program_id(0) == 0)
    def _():
        pi_ref[...] = jnp.zeros_like(pi_ref)
        pj_ref[...] = jnp.zeros_like(pj_ref)
        pg_ref[...] = jnp.zeros_like(pg_ref)

    grid_row = lax.broadcasted_iota(jnp.int32, (PEER_KEYS, PEER_ENTRIES), 0).astype(F32)

    def gate_grid(t):
        i_row = pi_ref[pl.ds(t, 1), :]
        j_row = pj_ref[pl.ds(t, 1), :]
        g_row = pg_ref[pl.ds(t, 1), :]
        a_t = jnp.where(grid_row == i_row, g_row, 0.0).astype(BF16)
        b_t = jnp.where(grid_row == j_row, 1.0, 0.0).astype(BF16)
        return _dot_nt(a_t, b_t)

    def store(h, ent_i, ent_j, gate):
        rows = pl.ds(pl.multiple_of(h * k, k), k)
        ei_ref[rows, :] = ent_i
        ej_ref[rows, :] = ent_j
        eg_ref[rows, :] = gate

    def fast_head(h, carry):
        ent_i, ent_j, gate, tie = _head_entries(s_ref, h, _pop_top)
        store(h, ent_i, ent_j, gate)
        tie_ref[pl.ds(h, 1), :] = tie
        for n in range(half_tile // PEER_HEADS):
            q = h * (half_tile // PEER_HEADS) + n
            packed = pltpu.pack_elementwise([gate_grid(q), gate_grid(q + half_tile)],
                                            packed_dtype=BF16)
            words = lax.bitcast_convert_type(packed, jnp.uint32)
            rows = pl.ds(pl.multiple_of(q * PEER_GRID_ROWS, PEER_GRID_ROWS), PEER_GRID_ROWS)
            for e in range(PEER_KEYS // PEER_GRID_ROWS):
                w_ref[e, rows, :] = words[e * PEER_GRID_ROWS:(e + 1) * PEER_GRID_ROWS, :]
        return carry

    def exact_head(h, carry):
        @pl.when(jnp.max(tie_ref[pl.ds(h, 1), :]) > 0.0)
        def _():
            ent_i, ent_j, gate, _ = _head_entries(s_ref, h, _extract_top)
            store(h, ent_i, ent_j, gate)
        return carry

    lax.fori_loop(0, PEER_HEADS, fast_head, 0, unroll=2)

    @pl.when(jnp.max(tie_ref[...]) > 0.0)
    def _():
        lax.fori_loop(0, PEER_HEADS, exact_head, 0)

    pi_ref[...] = ei_ref[...].T
    pj_ref[...] = ej_ref[...].T
    pg_ref[...] = eg_ref[...].T


def _topk(scores_t):
    n_hp, n_keys, t = scores_t.shape
    n_tiles = t // LANES
    tiles_per_block = PEER_BLOCK_T // LANES
    e_steps = PEER_KEYS // PEER_GRID_ROWS
    tile_rows = LANES // 2 * PEER_GRID_ROWS

    def out_index(g):
        tile = jnp.maximum(g - 1, 0)
        return (tile // tiles_per_block, 0, tile % tiles_per_block, 0)

    return pl.pallas_call(
        _topk_kernel,
        grid=(n_tiles + 1,),
        in_specs=[pl.BlockSpec((n_hp, n_keys, LANES),
                               lambda g: (0, 0, jnp.minimum(g, n_tiles - 1)))],
        out_specs=pl.BlockSpec((None, e_steps, tile_rows, PEER_KEYS), out_index),
        out_shape=jax.ShapeDtypeStruct(
            (t // PEER_BLOCK_T, e_steps, tiles_per_block * tile_rows, PEER_KEYS), jnp.uint32),
        scratch_shapes=[pltpu.VMEM((PEER_ENTRIES, LANES), F32)] * 3
        + [pltpu.VMEM((PEER_HEADS, LANES), F32)]
        + [pltpu.VMEM((LANES, PEER_ENTRIES), F32)] * 3,
        compiler_params=pltpu.CompilerParams(
            dimension_semantics=("arbitrary",), vmem_limit_bytes=VMEM_LIMIT),
        name="topk",
    )(scores_t)


PAIR = 2 * PEER_KEYS


def _gelu(x):
    return 0.5 * x * (1.0 + lax.erf(x * (2.0 ** -0.5)))


def _peer_kernel(n_ref, w_ref, u_ref, v_ref, h_ref, nw_ref, o_ref, hid_ref, acc_ref,
                 *, block_t, block_e):
    e_step = pl.program_id(1)
    n_pairs = block_e // PAIR
    grid_rows = block_e // PEER_KEYS
    half_tile = LANES // 2

    @pl.when(e_step == 0)
    def _():
        acc_ref[...] = jnp.zeros_like(acc_ref)

    def gate_rows(i):
        words = w_ref[pl.ds(i, block_t // 2, stride=grid_rows), :]
        lo, hi = [pltpu.unpack_elementwise(words, index=k, packed_dtype=BF16, unpacked_dtype=F32)
                  for k in range(2)]
        pieces = []
        for tile in range(block_t // LANES):
            rows = slice(tile * half_tile, (tile + 1) * half_tile)
            pieces += [lo[rows], hi[rows]]
        return jnp.concatenate(pieces, axis=0)

    n = n_ref[...]
    for p in range(n_pairs):
        experts = slice(p * PAIR, (p + 1) * PAIR)
        act = _dot_nt(n, u_ref[experts, :])
        w_pair = jnp.concatenate([gate_rows(2 * p), gate_rows(2 * p + 1)], axis=-1)
        hid_ref[:, experts] = (_gelu(act) * w_pair).astype(BF16)
    acc_ref[...] += _dot(hid_ref[...], v_ref[...])

    @pl.when(e_step == pl.num_programs(1) - 1)
    def _():
        h = h_ref[...] + acc_ref[...]
        o_ref[...] = h * _rms_scale(h) * nw_ref[...]


def _peer(n2, w_packed, u, v, h1, norm_w, block_t, block_e):
    t = n2.shape[0]
    n_exp = u.shape[0]
    tok = lambda w: pl.BlockSpec((block_t, w), lambda i, e: (i, 0))
    return pl.pallas_call(
        functools.partial(_peer_kernel, block_t=block_t, block_e=block_e),
        grid=(t // block_t, n_exp // block_e),
        in_specs=[tok(D_MODEL),
                  pl.BlockSpec((None, None, block_t // 2 * (block_e // PEER_KEYS), PEER_KEYS),
                               lambda i, e: (i, e, 0, 0)),
                  pl.BlockSpec((block_e, D_MODEL), lambda i, e: (e, 0)),
                  pl.BlockSpec((block_e, D_MODEL), lambda i, e: (e, 0)),
                  tok(D_MODEL),
                  pl.BlockSpec((1, D_MODEL), lambda i, e: (0, 0))],
        out_specs=tok(D_MODEL),
        out_shape=jax.ShapeDtypeStruct((t, D_MODEL), F32),
        scratch_shapes=[pltpu.VMEM((block_t, block_e), BF16),
                        pltpu.VMEM((block_t, D_MODEL), F32)],
        compiler_params=pltpu.CompilerParams(
            dimension_semantics=("parallel", "arbitrary"), vmem_limit_bytes=VMEM_LIMIT),
        name="peer",
    )(n2, w_packed, u, v, h1, norm_w)


def _pad_cols(w, width):
    return jnp.pad(w, ((0, 0), (0, width - w.shape[1])))


def _in_proj_slabs(w_in):
    parts, start = [], 0
    for size in _IN_SIZES:
        parts.append(w_in[:, start:start + size])
        start += size
    q, k, v, gate_lr, g_out, z, xbc, dt = parts
    return jnp.concatenate(
        [q, k, v, g_out, z, xbc, _pad_cols(gate_lr, LANES), _pad_cols(dt, LANES)],
        axis=1).astype(BF16)


def _layer(x2, batch, seq, p, peer_u, peer_v):
    row = lambda a: a.reshape(1, -1).astype(F32)
    q, k, v, g_out, z, xbc, gate_lr, dt = _in_proj(
        x2, row(p["norm_mix_w"]), _in_proj_slabs(p["w_in"]), block_t=512)

    w2p = jnp.pad(p["gla_w_gate2"].astype(F32), ((0, LANES - GLA_RANK), (0, 0)))
    o_gla, u_bf = _gla(q, k, v, gate_lr, g_out, w2p, row(p["gla_b_gate"]), row(p["gla_norm_w"]),
                       peer_u.astype(F32), batch, seq, block_l=256)

    head_rows = lambda a: jnp.broadcast_to(a.astype(F32).reshape(SSD_HEADS, 1), (SSD_HEADS, LANES))
    d_skip = jnp.repeat(p["ssd_d"].astype(F32), SSD_P).reshape(1, SSD_WIDTH)
    o_ssd, v_bf = _ssd(z, xbc, dt, p["ssd_conv_w"].astype(F32), row(p["ssd_conv_b"]),
                       head_rows(p["ssd_dt_bias"]), head_rows(p["ssd_a_log"]), d_skip,
                       row(p["ssd_norm_w"]), peer_v.astype(F32), batch, seq, block_l=512)

    keys = p["peer_sub_keys"].reshape(2 * PEER_HEADS, PEER_KEYS, PEER_HALF).astype(BF16)
    h1, n2, scores_t = _out_query(
        o_gla, o_ssd, x2, p["w_out"].astype(BF16), row(p["norm_ffn_w"]),
        p["peer_w_query"].astype(BF16), keys, block_t=512)

    return n2, _topk(scores_t), h1, u_bf, v_bf


def kernel(x, norm_mix_w, w_in, gla_w_gate2, gla_b_gate, gla_norm_w, ssd_conv_w, ssd_conv_b,
           ssd_dt_bias, ssd_a_log, ssd_d, ssd_norm_w, w_out, norm_ffn_w, peer_w_query,
           peer_sub_keys, peer_u, peer_v, norm_final_w):
    batch, seq, d = x.shape
    assert w_in.shape[0] == 1, "single-layer trunk"
    params = dict(norm_mix_w=norm_mix_w, w_in=w_in, gla_w_gate2=gla_w_gate2, gla_b_gate=gla_b_gate,
                  gla_norm_w=gla_norm_w, ssd_conv_w=ssd_conv_w, ssd_conv_b=ssd_conv_b,
                  ssd_dt_bias=ssd_dt_bias, ssd_a_log=ssd_a_log, ssd_d=ssd_d, ssd_norm_w=ssd_norm_w,
                  w_out=w_out, norm_ffn_w=norm_ffn_w, peer_w_query=peer_w_query,
                  peer_sub_keys=peer_sub_keys)
    p = {name: value[0] for name, value in params.items()}
    x2 = x.reshape(batch * seq, d).astype(F32)
    n2, w_packed, h1, u_bf, v_bf = _layer(x2, batch, seq, p, peer_u[0], peer_v[0])
    y = _peer(n2, w_packed, u_bf, v_bf, h1,
              norm_final_w.reshape(1, d).astype(F32), block_t=PEER_BLOCK_T, block_e=PEER_BLOCK_E)
    return y.reshape(batch, seq, d).astype(x.dtype)
```

```python
import functools

import jax
import jax.numpy as jnp
from jax import lax
from jax.experimental import pallas as pl
from jax.experimental.pallas import tpu as pltpu

F32 = jnp.float32
BF16 = jnp.bfloat16
HIGHEST = lax.Precision.HIGHEST

EPS = 1e-6
D_MODEL = 1024

GLA_HEADS = 4
GLA_DK = 64
GLA_DV = 128
GLA_QK = GLA_HEADS * GLA_DK
GLA_V = GLA_HEADS * GLA_DV
GLA_RANK = 16
GLA_NORMALIZER = 16.0
GLA_CHUNK = 64
GLA_CUMSUM_ROWS = 256

SSD_HEADS = 8
SSD_P = 64
SSD_WIDTH = SSD_HEADS * SSD_P
SSD_GROUPS = 2
SSD_N = 64
SSD_CONV = 4
SSD_CHUNK = 128
SSD_BC = SSD_GROUPS * SSD_N
SSD_CONV_DIM = SSD_WIDTH + 2 * SSD_BC

PEER_HEADS = 8
PEER_KEYS = 128
PEER_TOPK = 16
PEER_HALF = 128
PEER_ENTRIES = PEER_HEADS * PEER_TOPK
PEER_BLOCK_T = 512
PEER_BLOCK_E = 2048
PEER_GRID_ROWS = PEER_BLOCK_E // PEER_KEYS

LANES = 128
SUBLANES = 8
VMEM_LIMIT = 56 * 1024 * 1024

_IN_SIZES = (GLA_QK, GLA_QK, GLA_V, GLA_RANK, GLA_V, SSD_WIDTH, SSD_CONV_DIM, SSD_HEADS)
_SLAB_WIDTHS = (GLA_QK, GLA_QK, GLA_V, GLA_V, SSD_WIDTH, SSD_CONV_DIM, LANES, LANES)


def _dot(a, b, dims=((1,), (0,)), precision=None):
    return lax.dot_general(a, b, (dims, ((), ())), precision=precision,
                           preferred_element_type=F32)


def _dot_nt(a, b, precision=None):
    return _dot(a, b, ((1,), (1,)), precision)


def _dot_tn(a, b, precision=None):
    return _dot(a, b, ((0,), (0,)), precision)


def _silu(x):
    return x * (1.0 / (1.0 + jnp.exp(-x)))


def _softplus(x):
    return jnp.maximum(x, 0.0) + jnp.log(1.0 + jnp.exp(-jnp.abs(x)))


def _rms_scale(x):
    return lax.rsqrt(jnp.mean(x * x, axis=-1, keepdims=True) + EPS)


def _in_proj_kernel(x_ref, nw_ref, w_ref, *out_refs):
    x = x_ref[...]
    n = (x * _rms_scale(x) * nw_ref[...]).astype(BF16)
    start = 0
    for o_ref, width in zip(out_refs, _SLAB_WIDTHS):
        o_ref[...] = _dot(n, w_ref[:, start:start + width])
        start += width


def _in_proj(x2, norm_w, w_slabs, block_t):
    t = x2.shape[0]
    total = sum(_SLAB_WIDTHS)
    out_shape = [jax.ShapeDtypeStruct((t, w), F32) for w in _SLAB_WIDTHS]
    out_specs = [pl.BlockSpec((block_t, w), lambda i: (i, 0)) for w in _SLAB_WIDTHS]
    return pl.pallas_call(
        _in_proj_kernel,
        grid=(t // block_t,),
        in_specs=[pl.BlockSpec((block_t, D_MODEL), lambda i: (i, 0)),
                  pl.BlockSpec((1, D_MODEL), lambda i: (0, 0)),
                  pl.BlockSpec((D_MODEL, total), lambda i: (0, 0))],
        out_specs=out_specs,
        out_shape=out_shape,
        compiler_params=pltpu.CompilerParams(
            dimension_semantics=("parallel",), vmem_limit_bytes=VMEM_LIMIT),
        name="in_proj",
    )(x2, norm_w, w_slabs)


def _gla_kernel(q_ref, k_ref, v_ref, glr_ref, gout_ref, w2_ref, bg_ref, nw_ref, table_ref,
                o_ref, table_bf_ref, st_ref, *, n_chunks):
    c_len = GLA_CHUNK
    table_bf_ref[...] = table_ref[...].astype(BF16)

    @pl.when(pl.program_id(1) == 0)
    def _():
        st_ref[...] = jnp.zeros_like(st_ref)

    def iota(shape, dim):
        return lax.broadcasted_iota(jnp.int32, shape, dim)

    block_l = n_chunks * c_len
    gate = _dot(glr_ref[...], w2_ref[...], precision=HIGHEST) + bg_ref[...]
    log_a = -_softplus(-gate) * (1.0 / GLA_NORMALIZER)
    group = GLA_CUMSUM_ROWS
    tri = ((iota((group, group), 0) // c_len == iota((group, group), 1) // c_len)
           & (iota((group, group), 0) >= iota((group, group), 1))).astype(F32)
    g_cum = jnp.concatenate(
        [_dot(tri, log_a[g * group:(g + 1) * group], precision=HIGHEST)
         for g in range(block_l // group)], axis=0)
    g_end = [g_cum[(c + 1) * c_len - 1:(c + 1) * c_len, :] for c in range(n_chunks)]
    g_last = jnp.concatenate([jnp.broadcast_to(g, (c_len, GLA_QK)) for g in g_end], axis=0)
    k = k_ref[...]
    q_in = (q_ref[...] * (GLA_DK ** -0.5) * jnp.exp(g_cum)).astype(BF16)
    k_in = k * jnp.exp(-g_cum)
    k_end = (k * jnp.exp(g_last - g_cum)).astype(BF16)
    v = v_ref[...]

    k_bd_mask = iota((GLA_QK, GLA_QK), 0) // c_len == iota((GLA_QK, GLA_QK), 1) // GLA_DK
    v_bd_mask = iota((GLA_QK, GLA_V), 0) // c_len == iota((GLA_QK, GLA_V), 1) // GLA_DV
    causal = iota((c_len, GLA_QK), 0) >= iota((c_len, GLA_QK), 1) % c_len
    state_mask = iota((GLA_V, GLA_QK), 0) // GLA_DV == iota((GLA_V, GLA_QK), 1) // GLA_DK

    att = []
    for c in range(n_chunks):
        rows = slice(c * c_len, (c + 1) * c_len)
        k_bd = jnp.where(k_bd_mask, jnp.concatenate([k_in[rows]] * GLA_HEADS, axis=0), 0.0)
        scores = _dot_nt(q_in[rows], k_bd.astype(BF16))
        att.append(jnp.where(causal, scores, 0.0).astype(BF16))

    state_t = st_ref[...]
    nw = jnp.concatenate([nw_ref[...]] * GLA_HEADS, axis=1)
    for c in range(n_chunks):
        rows = slice(c * c_len, (c + 1) * c_len)
        v_c = v[rows]
        new_state = _dot_tn(v_c.astype(BF16), k_end[rows])
        o_inter = _dot_nt(q_in[rows], state_t.astype(BF16))
        state_t = state_t * jnp.exp(g_end[c]) + jnp.where(state_mask, new_state, 0.0)
        v_bd = jnp.where(v_bd_mask, jnp.concatenate([v_c] * GLA_HEADS, axis=0), 0.0)
        o = _dot(att[c], v_bd.astype(BF16)) + o_inter
        outs = []
        for h in range(GLA_HEADS):
            o_h = o[:, h * GLA_DV:(h + 1) * GLA_DV]
            outs.append(o_h * _rms_scale(o_h))
        o = jnp.concatenate(outs, axis=1) * nw
        o_ref[rows, :] = (o * _silu(gout_ref[rows, :])).astype(o_ref.dtype)
    st_ref[...] = state_t


def _table_slab_spec(table, batch, nblk):
    rows, width = table.shape
    slab = rows // (batch * nblk)
    assert slab * batch * nblk == rows and slab % (2 * SUBLANES) == 0
    return pl.BlockSpec((slab, width), lambda b, j: (b * nblk + j, 0))


def _gla(q, k, v, glr, gout, w2p, bg, nw, table, batch, seq, block_l):
    t = batch * seq
    nblk = seq // block_l
    tok = lambda w: pl.BlockSpec((block_l, w), lambda b, j: (b * nblk + j, 0))
    const = lambda shape: pl.BlockSpec(shape, lambda b, j: (0, 0))
    table_spec = _table_slab_spec(table, batch, nblk)
    return pl.pallas_call(
        functools.partial(_gla_kernel, n_chunks=block_l // GLA_CHUNK),
        grid=(batch, nblk),
        in_specs=[tok(GLA_QK), tok(GLA_QK), tok(GLA_V), tok(LANES), tok(GLA_V),
                  const((LANES, GLA_QK)), const((1, GLA_QK)), const((1, GLA_DV)), table_spec],
        out_specs=[tok(GLA_V), table_spec],
        out_shape=[jax.ShapeDtypeStruct((t, GLA_V), BF16),
                   jax.ShapeDtypeStruct(table.shape, BF16)],
        scratch_shapes=[pltpu.VMEM((GLA_V, GLA_QK), F32)],
        compiler_params=pltpu.CompilerParams(
            dimension_semantics=("parallel", "arbitrary"), vmem_limit_bytes=VMEM_LIMIT),
        name="gla",
    )(q, k, v, glr, gout, w2p, bg, nw, table)


def _ssd_kernel(z_ref, xbc_ref, dt_ref, cw_ref, cb_ref, dtb_ref, alog_ref, dskip_ref, nw_ref,
                table_ref, o_ref, table_bf_ref, xs_ref, st_ref, *, block_l):
    c_len = SSD_CHUNK
    halo = SUBLANES
    table_bf_ref[...] = table_ref[...].astype(BF16)

    @pl.when(pl.program_id(1) == 0)
    def _():
        xs_ref[0:halo, :] = jnp.zeros((halo, SSD_CONV_DIM), F32)
        st_ref[...] = jnp.zeros_like(st_ref)

    xs_ref[halo:halo + block_l, :] = xbc_ref[...]

    row = lax.broadcasted_iota(jnp.int32, (c_len, c_len), 0)
    col = lax.broadcasted_iota(jnp.int32, (c_len, c_len), 1)
    causal = row >= col
    upper = (row <= col).astype(F32)
    first_half = col < SSD_N
    cw = cw_ref[...]
    cb = cb_ref[...]
    a_neg = -jnp.exp(alog_ref[...])
    dtb = dtb_ref[...]
    head_pad = jnp.zeros((LANES - SSD_HEADS, c_len), F32)

    def chunk(c, carry):
        r0 = pl.multiple_of(c * c_len, c_len)
        rows = pl.ds(r0, c_len)
        window = xs_ref[pl.ds(r0, c_len + halo), :]
        conv = cb
        for tap in range(SSD_CONV):
            shift = halo - (SSD_CONV - 1) + tap
            conv = conv + cw[tap:tap + 1, :] * window[shift:shift + c_len, :]
        xc = _silu(conv)
        b_m = xc[:, SSD_WIDTH:SSD_WIDTH + SSD_BC]
        c_m = xc[:, SSD_WIDTH + SSD_BC:]

        dt_t = _softplus(dt_ref[rows, :].T[0:SSD_HEADS, :] + dtb)
        a_cum_t = _dot(dt_t * a_neg, upper, precision=HIGHEST)
        dt = jnp.concatenate([dt_t, head_pad], axis=0).T
        a_cum = jnp.concatenate([a_cum_t, head_pad], axis=0).T

        scores = []
        c_g = []
        for g in range(SSD_GROUPS):
            in_group = first_half if g == 0 else jnp.logical_not(first_half)
            c_g.append(jnp.where(in_group, c_m, 0.0).astype(BF16))
            scores.append(_dot_nt(c_g[g], b_m.astype(BF16)))

        for m in range(SSD_HEADS // 2):
            g = (2 * m) // (SSD_HEADS // SSD_GROUPS)
            lanes = slice(m * LANES, (m + 1) * LANES)
            x_pair = xc[:, lanes]
            halves = []
            dts = []
            for par in range(2):
                h = 2 * m + par
                dts.append(jnp.broadcast_to(dt[:, h:h + 1], (c_len, LANES)))
            xdt = (x_pair * jnp.where(first_half, dts[0], dts[1])).astype(BF16)
            for par in range(2):
                h = 2 * m + par
                a_col = jnp.broadcast_to(a_cum[:, h:h + 1], (c_len, c_len))
                a_row = jnp.broadcast_to(a_cum_t[h:h + 1, :], (c_len, c_len))
                a_end = a_col[c_len - 1:c_len, :]
                decay = jnp.exp(jnp.where(causal, a_col - a_row, -jnp.inf))
                y_diag = _dot((scores[g] * decay).astype(BF16), xdt)
                state = st_ref[h]
                y_off = _dot(c_g[g], state.astype(BF16)) * jnp.exp(a_col)
                halves.append(y_diag + y_off)
                b_dec = (b_m * jnp.exp(a_end - a_col)).astype(BF16)
                st_ref[h] = state * jnp.exp(a_end) + _dot_tn(b_dec, xdt)
            y = jnp.where(first_half, halves[0], halves[1]) + x_pair * dskip_ref[:, lanes]
            y = y * _silu(z_ref[rows, lanes])
            xs_pair_sq = jnp.sum(y * y, axis=-1, keepdims=True)
            if m % 2 == 0:
                y_prev, sq_prev = y, xs_pair_sq
            else:
                scale = lax.rsqrt((sq_prev + xs_pair_sq) * (1.0 / (2 * LANES)) + EPS)
                lo = slice((m - 1) * LANES, m * LANES)
                o_ref[rows, lo] = (y_prev * scale * nw_ref[:, lo]).astype(o_ref.dtype)
                o_ref[rows, lanes] = (y * scale * nw_ref[:, lanes]).astype(o_ref.dtype)
        return carry

    lax.fori_loop(0, block_l // c_len, chunk, 0)
    xs_ref[0:halo, :] = xs_ref[block_l:block_l + halo, :]


def _ssd(z, xbc, dt, cw, cb, dtb, alog, dskip, nw, table, batch, seq, block_l):
    t = batch * seq
    nblk = seq // block_l
    tok = lambda w: pl.BlockSpec((block_l, w), lambda b, j: (b * nblk + j, 0))
    const = lambda shape: pl.BlockSpec(shape, lambda b, j: (0, 0))
    table_spec = _table_slab_spec(table, batch, nblk)
    return pl.pallas_call(
        functools.partial(_ssd_kernel, block_l=block_l),
        grid=(batch, nblk),
        in_specs=[tok(SSD_WIDTH), tok(SSD_CONV_DIM), tok(LANES),
                  const((SSD_CONV, SSD_CONV_DIM)), const((1, SSD_CONV_DIM)),
                  const((SSD_HEADS, LANES)), const((SSD_HEADS, LANES)), const((1, SSD_WIDTH)),
                  const((1, SSD_WIDTH)), table_spec],
        out_specs=[tok(SSD_WIDTH), table_spec],
        out_shape=[jax.ShapeDtypeStruct((t, SSD_WIDTH), BF16),
                   jax.ShapeDtypeStruct(table.shape, BF16)],
        scratch_shapes=[pltpu.VMEM((block_l + SUBLANES, SSD_CONV_DIM), F32),
                        pltpu.VMEM((SSD_HEADS, SSD_BC, LANES), F32)],
        compiler_params=pltpu.CompilerParams(
            dimension_semantics=("parallel", "arbitrary"), vmem_limit_bytes=VMEM_LIMIT),
        name="ssd",
    )(z, xbc, dt, cw, cb, dtb, alog, dskip, nw, table)


def _out_query_kernel(og_ref, os_ref, x_ref, wo_ref, nw_ref, wq_ref, keys_ref,
                      h_ref, n_ref, s_ref, q_ref):
    mixed = _dot(og_ref[...], wo_ref[0:GLA_V, :]) + _dot(os_ref[...], wo_ref[GLA_V:, :])
    h = x_ref[...] + mixed
    h_ref[...] = h
    n = (h * _rms_scale(h) * nw_ref[...]).astype(BF16)
    n_ref[...] = n
    q_ref[...] = _dot(n, wq_ref[...]).astype(BF16)
    for hp in range(2 * PEER_HEADS):
        lanes = slice(hp * PEER_HALF, (hp + 1) * PEER_HALF)
        s_ref[hp] = _dot_nt(keys_ref[hp], q_ref[:, lanes])


def _out_query(o_gla, o_ssd, x2, w_out, norm_w, w_query, keys, block_t):
    t = x2.shape[0]
    n_hp = 2 * PEER_HEADS
    return pl.pallas_call(
        _out_query_kernel,
        grid=(t // block_t,),
        in_specs=[pl.BlockSpec((block_t, GLA_V), lambda i: (i, 0)),
                  pl.BlockSpec((block_t, SSD_WIDTH), lambda i: (i, 0)),
                  pl.BlockSpec((block_t, D_MODEL), lambda i: (i, 0)),
                  pl.BlockSpec((D_MODEL, D_MODEL), lambda i: (0, 0)),
                  pl.BlockSpec((1, D_MODEL), lambda i: (0, 0)),
                  pl.BlockSpec((D_MODEL, n_hp * PEER_HALF), lambda i: (0, 0)),
                  pl.BlockSpec((n_hp, PEER_KEYS, PEER_HALF), lambda i: (0, 0, 0))],
        out_specs=[pl.BlockSpec((block_t, D_MODEL), lambda i: (i, 0)),
                   pl.BlockSpec((block_t, D_MODEL), lambda i: (i, 0)),
                   pl.BlockSpec((n_hp, PEER_KEYS, block_t), lambda i: (0, 0, i))],
        out_shape=[jax.ShapeDtypeStruct((t, D_MODEL), F32),
                   jax.ShapeDtypeStruct((t, D_MODEL), BF16),
                   jax.ShapeDtypeStruct((n_hp, PEER_KEYS, t), F32)],
        scratch_shapes=[pltpu.VMEM((block_t, n_hp * PEER_HALF), BF16)],
        compiler_params=pltpu.CompilerParams(
            dimension_semantics=("parallel",), vmem_limit_bytes=VMEM_LIMIT),
        name="out_query",
    )(o_gla, o_ssd, x2, w_out, norm_w, w_query, keys)


def _extract_top(values, ids, count):
    lanes = values.shape[1]
    slot = lax.broadcasted_iota(jnp.int32, (count, lanes), 0)
    top_v = jnp.zeros((count, lanes), F32)
    top_i = jnp.zeros((count, lanes), F32)
    for r in range(count):
        m = jnp.max(values, axis=0, keepdims=True)
        sel = jnp.min(jnp.where(values == m, ids, jnp.inf), axis=0, keepdims=True)
        values = jnp.where(ids == sel, -jnp.inf, values)
        top_v = jnp.where(slot == r, m, top_v)
        top_i = jnp.where(slot == r, sel, top_i)
    return top_v, top_i, jnp.zeros((1, lanes), F32)


def _sorting_network(n):
    size = 1 << (n - 1).bit_length()
    pairs = []
    p = 1
    while p < size:
        k = p
        while k >= 1:
            for j in range(k % p, size - k, 2 * k):
                for i in range(min(k, size - j - k)):
                    if (i + j) // (2 * p) == (i + j + k) // (2 * p):
                        pairs.append((i + j, i + j + k))
            k //= 2
        p *= 2
    return [(a, b) for a, b in pairs if b < n]


def _pop_top(values, ids, count):
    rows, lanes = values.shape
    n = rows // SUBLANES
    vals = [values[v * SUBLANES:(v + 1) * SUBLANES] for v in range(n)]
    idl = [ids[v * SUBLANES:(v + 1) * SUBLANES] for v in range(n)]
    for a, b in _sorting_network(n):
        swap = vals[b] > vals[a]
        vals[a], vals[b] = jnp.where(swap, vals[b], vals[a]), jnp.where(swap, vals[a], vals[b])
        idl[a], idl[b] = jnp.where(swap, idl[b], idl[a]), jnp.where(swap, idl[a], idl[b])

    slot = lax.broadcasted_iota(jnp.int32, (count, lanes), 0)
    top_v = jnp.zeros((count, lanes), F32)
    top_i = jnp.zeros((count, lanes), F32)
    tie = jnp.zeros((1, lanes), F32)
    prev = None
    for r in range(count + 1):
        m = jnp.max(vals[0], axis=0, keepdims=True)
        if prev is not None:
            tie = jnp.where(m == prev, 1.0, tie)
        prev = m
        if r == count:
            break
        sel = jnp.min(jnp.where(vals[0] == m, idl[0], jnp.inf), axis=0, keepdims=True)
        top_v = jnp.where(slot == r, m, top_v)
        top_i = jnp.where(slot == r, sel, top_i)
        hit = idl[0] == sel
        depth = min(n - 1, count - r)
        for v in range(depth):
            vals[v] = jnp.where(hit, vals[v + 1], vals[v])
            idl[v] = jnp.where(hit, idl[v + 1], idl[v])
        vals[depth] = jnp.where(hit, -jnp.inf, vals[depth])
    return top_v, top_i, tie


def _gather_rows(table, index):
    row = lax.broadcasted_iota(jnp.int32, table.shape, 0).astype(F32)
    return jnp.sum(jnp.where(row == index, table, 0.0), axis=0, keepdims=True)


def _head_entries(s_ref, h, top_fn):
    k = PEER_TOPK
    lanes = LANES
    key_id = lax.broadcasted_iota(jnp.int32, (PEER_KEYS, lanes), 0).astype(F32)
    sub = lax.broadcasted_iota(jnp.int32, (SUBLANES, lanes), 0).astype(F32)
    slot = lax.broadcasted_iota(jnp.int32, (k, lanes), 0)

    s1, i1, tie1 = top_fn(s_ref[2 * h], key_id, k)
    s2, i2, tie2 = top_fn(s_ref[2 * h + 1], key_id, k)
    pieces, flats = [], []
    for b0 in (0, SUBLANES):
        pieces.append(s1[0:1, :] + s2[b0:b0 + SUBLANES, :])
        flats.append(sub + float(b0))
    for a in range(1, SUBLANES):
        limit = k // (a + 1)
        pieces.append(jnp.where(sub < float(limit), s1[a:a + 1, :] + s2[0:SUBLANES, :], -jnp.inf))
        flats.append(sub + float(a * k))
    pieces.append(s1[SUBLANES:k, :] + s2[0:1, :])
    flats.append((sub + float(SUBLANES)) * float(k))
    cand = jnp.concatenate(pieces, axis=0)
    flat = jnp.concatenate(flats, axis=0)
    best_s, best_flat, tie3 = top_fn(cand, flat, k)

    ent_i = jnp.zeros((k, lanes), F32)
    ent_j = jnp.zeros((k, lanes), F32)
    for r in range(k):
        pos = best_flat[r:r + 1, :]
        a_idx = jnp.floor(pos * (1.0 / k))
        b_idx = pos - a_idx * float(k)
        ent_i = jnp.where(slot == r, _gather_rows(i1, a_idx), ent_i)
        ent_j = jnp.where(slot == r, _gather_rows(i2, b_idx), ent_j)
    e = jnp.exp(best_s - best_s[0:1, :])
    gate = e / jnp.sum(e, axis=0, keepdims=True)
    return ent_i, ent_j, gate, jnp.maximum(jnp.maximum(tie1, tie2), tie3)


def _topk_kernel(s_ref, w_ref, ei_ref, ej_ref, eg_ref, tie_ref, pi_ref, pj_ref, pg_ref):
    k = PEER_TOPK
    half_tile = LANES // 2

    @pl.when(pl.program_id(0) == 0)
    def _():
        pi_ref[...] = jnp.zeros_like(pi_ref)
        pj_ref[...] = jnp.zeros_like(pj_ref)
        pg_ref[...] = jnp.zeros_like(pg_ref)

    grid_row = lax.broadcasted_iota(jnp.int32, (PEER_KEYS, PEER_ENTRIES), 0).astype(F32)

    def gate_grid(t):
        i_row = pi_ref[pl.ds(t, 1), :]
        j_row = pj_ref[pl.ds(t, 1), :]
        g_row = pg_ref[pl.ds(t, 1), :]
        a_t = jnp.where(grid_row == i_row, g_row, 0.0).astype(BF16)
        b_t = jnp.where(grid_row == j_row, 1.0, 0.0).astype(BF16)
        return _dot_nt(a_t, b_t)

    def store(h, ent_i, ent_j, gate):
        rows = pl.ds(pl.multiple_of(h * k, k), k)
        ei_ref[rows, :] = ent_i
        ej_ref[rows, :] = ent_j
        eg_ref[rows, :] = gate

    def fast_head(h, carry):
        ent_i, ent_j, gate, tie = _head_entries(s_ref, h, _pop_top)
        store(h, ent_i, ent_j, gate)
        tie_ref[pl.ds(h, 1), :] = tie
        for n in range(half_tile // PEER_HEADS):
            q = h * (half_tile // PEER_HEADS) + n
            packed = pltpu.pack_elementwise([gate_grid(q), gate_grid(q + half_tile)],
                                            packed_dtype=BF16)
            words = lax.bitcast_convert_type(packed, jnp.uint32)
            rows = pl.ds(pl.multiple_of(q * PEER_GRID_ROWS, PEER_GRID_ROWS), PEER_GRID_ROWS)
            for e in range(PEER_KEYS // PEER_GRID_ROWS):
                w_ref[e, rows, :] = words[e * PEER_GRID_ROWS:(e + 1) * PEER_GRID_ROWS, :]
        return carry

    def exact_head(h, carry):
        @pl.when(jnp.max(tie_ref[pl.ds(h, 1), :]) > 0.0)
        def _():
            ent_i, ent_j, gate, _ = _head_entries(s_ref, h, _extract_top)
            store(h, ent_i, ent_j, gate)
        return carry

    lax.fori_loop(0, PEER_HEADS, fast_head, 0, unroll=2)

    @pl.when(jnp.max(tie_ref[...]) > 0.0)
    def _():
        lax.fori_loop(0, PEER_HEADS, exact_head, 0)

    pi_ref[...] = ei_ref[...].T
    pj_ref[...] = ej_ref[...].T
    pg_ref[...] = eg_ref[...].T


def _topk(scores_t):
    n_hp, n_keys, t = scores_t.shape
    n_tiles = t // LANES
    tiles_per_block = PEER_BLOCK_T // LANES
    e_steps = PEER_KEYS // PEER_GRID_ROWS
    tile_rows = LANES // 2 * PEER_GRID_ROWS

    def out_index(g):
        tile = jnp.maximum(g - 1, 0)
        return (tile // tiles_per_block, 0, tile % tiles_per_block, 0)

    return pl.pallas_call(
        _topk_kernel,
        grid=(n_tiles + 1,),
        in_specs=[pl.BlockSpec((n_hp, n_keys, LANES),
                               lambda g: (0, 0, jnp.minimum(g, n_tiles - 1)))],
        out_specs=pl.BlockSpec((None, e_steps, tile_rows, PEER_KEYS), out_index),
        out_shape=jax.ShapeDtypeStruct(
            (t // PEER_BLOCK_T, e_steps, tiles_per_block * tile_rows, PEER_KEYS), jnp.uint32),
        scratch_shapes=[pltpu.VMEM((PEER_ENTRIES, LANES), F32)] * 3
        + [pltpu.VMEM((PEER_HEADS, LANES), F32)]
        + [pltpu.VMEM((LANES, PEER_ENTRIES), F32)] * 3,
        compiler_params=pltpu.CompilerParams(
            dimension_semantics=("arbitrary",), vmem_limit_bytes=VMEM_LIMIT),
        name="topk",
    )(scores_t)


PAIR = 2 * PEER_KEYS


def _gelu(x):
    return 0.5 * x * (1.0 + lax.erf(x * (2.0 ** -0.5)))


def _peer_kernel(n_ref, w_ref, u_ref, v_ref, h_ref, nw_ref, o_ref, hid_ref, acc_ref,
                 *, block_t, block_e):
    e_step = pl.program_id(1)
    n_pairs = block_e // PAIR
    grid_rows = block_e // PEER_KEYS
    half_tile = LANES // 2

    @pl.when(e_step == 0)
    def _():
        acc_ref[...] = jnp.zeros_like(acc_ref)

    def gate_rows(i):
        words = w_ref[pl.ds(i, block_t // 2, stride=grid_rows), :]
        lo, hi = [pltpu.unpack_elementwise(words, index=k, packed_dtype=BF16, unpacked_dtype=F32)
                  for k in range(2)]
        pieces = []
        for tile in range(block_t // LANES):
            rows = slice(tile * half_tile, (tile + 1) * half_tile)
            pieces += [lo[rows], hi[rows]]
        return jnp.concatenate(pieces, axis=0)

    n = n_ref[...]
    for p in range(n_pairs):
        experts = slice(p * PAIR, (p + 1) * PAIR)
        act = _dot_nt(n, u_ref[experts, :])
        w_pair = jnp.concatenate([gate_rows(2 * p), gate_rows(2 * p + 1)], axis=-1)
        hid_ref[:, experts] = (_gelu(act) * w_pair).astype(BF16)
    acc_ref[...] += _dot(hid_ref[...], v_ref[...])

    @pl.when(e_step == pl.num_programs(1) - 1)
    def _():
        h = h_ref[...] + acc_ref[...]
        o_ref[...] = h * _rms_scale(h) * nw_ref[...]


def _peer(n2, w_packed, u, v, h1, norm_w, block_t, block_e):
    t = n2.shape[0]
    n_exp = u.shape[0]
    tok = lambda w: pl.BlockSpec((block_t, w), lambda i, e: (i, 0))
    return pl.pallas_call(
        functools.partial(_peer_kernel, block_t=block_t, block_e=block_e),
        grid=(t // block_t, n_exp // block_e),
        in_specs=[tok(D_MODEL),
                  pl.BlockSpec((None, None, block_t // 2 * (block_e // PEER_KEYS), PEER_KEYS),
                               lambda i, e: (i, e, 0, 0)),
                  pl.BlockSpec((block_e, D_MODEL), lambda i, e: (e, 0)),
                  pl.BlockSpec((block_e, D_MODEL), lambda i, e: (e, 0)),
                  tok(D_MODEL),
                  pl.BlockSpec((1, D_MODEL), lambda i, e: (0, 0))],
        out_specs=tok(D_MODEL),
        out_shape=jax.ShapeDtypeStruct((t, D_MODEL), F32),
        scratch_shapes=[pltpu.VMEM((block_t, block_e), BF16),
                        pltpu.VMEM((block_t, D_MODEL), F32)],
        compiler_params=pltpu.CompilerParams(
            dimension_semantics=("parallel", "arbitrary"), vmem_limit_bytes=VMEM_LIMIT),
        name="peer",
    )(n2, w_packed, u, v, h1, norm_w)


def _pad_cols(w, width):
    return jnp.pad(w, ((0, 0), (0, width - w.shape[1])))


def _in_proj_slabs(w_in):
    parts, start = [], 0
    for size in _IN_SIZES:
        parts.append(w_in[:, start:start + size])
        start += size
    q, k, v, gate_lr, g_out, z, xbc, dt = parts
    return jnp.concatenate(
        [q, k, v, g_out, z, xbc, _pad_cols(gate_lr, LANES), _pad_cols(dt, LANES)],
        axis=1).astype(BF16)


def _layer(x2, batch, seq, p, peer_u, peer_v):
    row = lambda a: a.reshape(1, -1).astype(F32)
    q, k, v, g_out, z, xbc, gate_lr, dt = _in_proj(
        x2, row(p["norm_mix_w"]), _in_proj_slabs(p["w_in"]), block_t=512)

    w2p = jnp.pad(p["gla_w_gate2"].astype(F32), ((0, LANES - GLA_RANK), (0, 0)))
    o_gla, u_bf = _gla(q, k, v, gate_lr, g_out, w2p, row(p["gla_b_gate"]), row(p["gla_norm_w"]),
                       peer_u.astype(F32), batch, seq, block_l=512)

    head_rows = lambda a: jnp.broadcast_to(a.astype(F32).reshape(SSD_HEADS, 1), (SSD_HEADS, LANES))
    d_skip = jnp.repeat(p["ssd_d"].astype(F32), SSD_P).reshape(1, SSD_WIDTH)
    o_ssd, v_bf = _ssd(z, xbc, dt, p["ssd_conv_w"].astype(F32), row(p["ssd_conv_b"]),
                       head_rows(p["ssd_dt_bias"]), head_rows(p["ssd_a_log"]), d_skip,
                       row(p["ssd_norm_w"]), peer_v.astype(F32), batch, seq, block_l=512)

    keys = p["peer_sub_keys"].reshape(2 * PEER_HEADS, PEER_KEYS, PEER_HALF).astype(BF16)
    h1, n2, scores_t = _out_query(
        o_gla, o_ssd, x2, p["w_out"].astype(BF16), row(p["norm_ffn_w"]),
        p["peer_w_query"].astype(BF16), keys, block_t=512)

    return n2, _topk(scores_t), h1, u_bf, v_bf


def kernel(x, norm_mix_w, w_in, gla_w_gate2, gla_b_gate, gla_norm_w, ssd_conv_w, ssd_conv_b,
           ssd_dt_bias, ssd_a_log, ssd_d, ssd_norm_w, w_out, norm_ffn_w, peer_w_query,
           peer_sub_keys, peer_u, peer_v, norm_final_w):
    batch, seq, d = x.shape
    assert w_in.shape[0] == 1, "single-layer trunk"
    params = dict(norm_mix_w=norm_mix_w, w_in=w_in, gla_w_gate2=gla_w_gate2, gla_b_gate=gla_b_gate,
                  gla_norm_w=gla_norm_w, ssd_conv_w=ssd_conv_w, ssd_conv_b=ssd_conv_b,
                  ssd_dt_bias=ssd_dt_bias, ssd_a_log=ssd_a_log, ssd_d=ssd_d, ssd_norm_w=ssd_norm_w,
                  w_out=w_out, norm_ffn_w=norm_ffn_w, peer_w_query=peer_w_query,
                  peer_sub_keys=peer_sub_keys)
    p = {name: value[0] for name, value in params.items()}
    x2 = x.reshape(batch * seq, d).astype(F32)
    n2, w_packed, h1, u_bf, v_bf = _layer(x2, batch, seq, p, peer_u[0], peer_v[0])
    y = _peer(n2, w_packed, u_bf, v_bf, h1,
              norm_final_w.reshape(1, d).astype(F32), block_t=PEER_BLOCK_T, block_e=PEER_BLOCK_E)
    return y.reshape(batch, seq, d).astype(x.dtype)
```

```python
import functools

import jax
import jax.numpy as jnp
from jax import lax
from jax.experimental import pallas as pl
from jax.experimental.pallas import tpu as pltpu

F32 = jnp.float32
BF16 = jnp.bfloat16
HIGHEST = lax.Precision.HIGHEST

EPS = 1e-6
D_MODEL = 1024

GLA_HEADS = 4
GLA_DK = 64
GLA_DV = 128
GLA_QK = GLA_HEADS * GLA_DK
GLA_V = GLA_HEADS * GLA_DV
GLA_RANK = 16
GLA_NORMALIZER = 16.0
GLA_CHUNK = 64
GLA_CUMSUM_ROWS = 256

SSD_HEADS = 8
SSD_P = 64
SSD_WIDTH = SSD_HEADS * SSD_P
SSD_GROUPS = 2
SSD_N = 64
SSD_CONV = 4
SSD_CHUNK = 128
SSD_BC = SSD_GROUPS * SSD_N
SSD_CONV_DIM = SSD_WIDTH + 2 * SSD_BC

PEER_HEADS = 8
PEER_KEYS = 128
PEER_TOPK = 16
PEER_HALF = 128
PEER_ENTRIES = PEER_HEADS * PEER_TOPK
PEER_BLOCK_T = 1024
PEER_BLOCK_E = 2048
PEER_GRID_ROWS = PEER_BLOCK_E // PEER_KEYS

LANES = 128
SUBLANES = 8
VMEM_LIMIT = 56 * 1024 * 1024

_IN_SIZES = (GLA_QK, GLA_QK, GLA_V, GLA_RANK, GLA_V, SSD_WIDTH, SSD_CONV_DIM, SSD_HEADS)
_SLAB_WIDTHS = (GLA_QK, GLA_QK, GLA_V, GLA_V, SSD_WIDTH, SSD_CONV_DIM, LANES, LANES)


def _dot(a, b, dims=((1,), (0,)), precision=None):
    return lax.dot_general(a, b, (dims, ((), ())), precision=precision,
                           preferred_element_type=F32)


def _dot_nt(a, b, precision=None):
    return _dot(a, b, ((1,), (1,)), precision)


def _dot_tn(a, b, precision=None):
    return _dot(a, b, ((0,), (0,)), precision)


def _silu(x):
    return x * (1.0 / (1.0 + jnp.exp(-x)))


def _softplus(x):
    return jnp.maximum(x, 0.0) + jnp.log(1.0 + jnp.exp(-jnp.abs(x)))


def _rms_scale(x):
    return lax.rsqrt(jnp.mean(x * x, axis=-1, keepdims=True) + EPS)


def _in_proj_kernel(x_ref, nw_ref, w_ref, *out_refs):
    x = x_ref[...]
    n = (x * _rms_scale(x) * nw_ref[...]).astype(BF16)
    start = 0
    for o_ref, width in zip(out_refs, _SLAB_WIDTHS):
        o_ref[...] = _dot(n, w_ref[:, start:start + width])
        start += width


def _in_proj(x2, norm_w, w_slabs, block_t):
    t = x2.shape[0]
    total = sum(_SLAB_WIDTHS)
    out_shape = [jax.ShapeDtypeStruct((t, w), F32) for w in _SLAB_WIDTHS]
    out_specs = [pl.BlockSpec((block_t, w), lambda i: (i, 0)) for w in _SLAB_WIDTHS]
    return pl.pallas_call(
        _in_proj_kernel,
        grid=(t // block_t,),
        in_specs=[pl.BlockSpec((block_t, D_MODEL), lambda i: (i, 0)),
                  pl.BlockSpec((1, D_MODEL), lambda i: (0, 0)),
                  pl.BlockSpec((D_MODEL, total), lambda i: (0, 0))],
        out_specs=out_specs,
        out_shape=out_shape,
        compiler_params=pltpu.CompilerParams(
            dimension_semantics=("parallel",), vmem_limit_bytes=VMEM_LIMIT),
        name="in_proj",
    )(x2, norm_w, w_slabs)


def _gla_kernel(q_ref, k_ref, v_ref, glr_ref, gout_ref, w2_ref, bg_ref, nw_ref, table_ref,
                o_ref, table_bf_ref, st_ref, *, n_chunks):
    c_len = GLA_CHUNK
    table_bf_ref[...] = table_ref[...].astype(BF16)

    @pl.when(pl.program_id(1) == 0)
    def _():
        st_ref[...] = jnp.zeros_like(st_ref)

    def iota(shape, dim):
        return lax.broadcasted_iota(jnp.int32, shape, dim)

    block_l = n_chunks * c_len
    gate = _dot(glr_ref[...], w2_ref[...], precision=HIGHEST) + bg_ref[...]
    log_a = -_softplus(-gate) * (1.0 / GLA_NORMALIZER)
    group = GLA_CUMSUM_ROWS
    tri = ((iota((group, group), 0) // c_len == iota((group, group), 1) // c_len)
           & (iota((group, group), 0) >= iota((group, group), 1))).astype(F32)
    g_cum = jnp.concatenate(
        [_dot(tri, log_a[g * group:(g + 1) * group], precision=HIGHEST)
         for g in range(block_l // group)], axis=0)
    g_end = [g_cum[(c + 1) * c_len - 1:(c + 1) * c_len, :] for c in range(n_chunks)]
    g_last = jnp.concatenate([jnp.broadcast_to(g, (c_len, GLA_QK)) for g in g_end], axis=0)
    k = k_ref[...]
    q_in = (q_ref[...] * (GLA_DK ** -0.5) * jnp.exp(g_cum)).astype(BF16)
    k_in = k * jnp.exp(-g_cum)
    k_end = (k * jnp.exp(g_last - g_cum)).astype(BF16)
    v = v_ref[...]

    k_bd_mask = iota((GLA_QK, GLA_QK), 0) // c_len == iota((GLA_QK, GLA_QK), 1) // GLA_DK
    v_bd_mask = iota((GLA_QK, GLA_V), 0) // c_len == iota((GLA_QK, GLA_V), 1) // GLA_DV
    causal = iota((c_len, GLA_QK), 0) >= iota((c_len, GLA_QK), 1) % c_len
    state_mask = iota((GLA_V, GLA_QK), 0) // GLA_DV == iota((GLA_V, GLA_QK), 1) // GLA_DK

    att = []
    for c in range(n_chunks):
        rows = slice(c * c_len, (c + 1) * c_len)
        k_bd = jnp.where(k_bd_mask, jnp.concatenate([k_in[rows]] * GLA_HEADS, axis=0), 0.0)
        scores = _dot_nt(q_in[rows], k_bd.astype(BF16))
        att.append(jnp.where(causal, scores, 0.0).astype(BF16))

    state_t = st_ref[...]
    nw = jnp.concatenate([nw_ref[...]] * GLA_HEADS, axis=1)
    for c in range(n_chunks):
        rows = slice(c * c_len, (c + 1) * c_len)
        v_c = v[rows]
        new_state = _dot_tn(v_c.astype(BF16), k_end[rows])
        o_inter = _dot_nt(q_in[rows], state_t.astype(BF16))
        state_t = state_t * jnp.exp(g_end[c]) + jnp.where(state_mask, new_state, 0.0)
        v_bd = jnp.where(v_bd_mask, jnp.concatenate([v_c] * GLA_HEADS, axis=0), 0.0)
        o = _dot(att[c], v_bd.astype(BF16)) + o_inter
        outs = []
        for h in range(GLA_HEADS):
            o_h = o[:, h * GLA_DV:(h + 1) * GLA_DV]
            outs.append(o_h * _rms_scale(o_h))
        o = jnp.concatenate(outs, axis=1) * nw
        o_ref[rows, :] = (o * _silu(gout_ref[rows, :])).astype(o_ref.dtype)
    st_ref[...] = state_t


def _table_slab_spec(table, batch, nblk):
    rows, width = table.shape
    slab = rows // (batch * nblk)
    assert slab * batch * nblk == rows and slab % (2 * SUBLANES) == 0
    return pl.BlockSpec((slab, width), lambda b, j: (b * nblk + j, 0))


def _gla(q, k, v, glr, gout, w2p, bg, nw, table, batch, seq, block_l):
    t = batch * seq
    nblk = seq // block_l
    tok = lambda w: pl.BlockSpec((block_l, w), lambda b, j: (b * nblk + j, 0))
    const = lambda shape: pl.BlockSpec(shape, lambda b, j: (0, 0))
    table_spec = _table_slab_spec(table, batch, nblk)
    return pl.pallas_call(
        functools.partial(_gla_kernel, n_chunks=block_l // GLA_CHUNK),
        grid=(batch, nblk),
        in_specs=[tok(GLA_QK), tok(GLA_QK), tok(GLA_V), tok(LANES), tok(GLA_V),
                  const((LANES, GLA_QK)), const((1, GLA_QK)), const((1, GLA_DV)), table_spec],
        out_specs=[tok(GLA_V), table_spec],
        out_shape=[jax.ShapeDtypeStruct((t, GLA_V), BF16),
                   jax.ShapeDtypeStruct(table.shape, BF16)],
        scratch_shapes=[pltpu.VMEM((GLA_V, GLA_QK), F32)],
        compiler_params=pltpu.CompilerParams(
            dimension_semantics=("parallel", "arbitrary"), vmem_limit_bytes=VMEM_LIMIT),
        name="gla",
    )(q, k, v, glr, gout, w2p, bg, nw, table)


def _ssd_kernel(z_ref, xbc_ref, dt_ref, cw_ref, cb_ref, dtb_ref, alog_ref, dskip_ref, nw_ref,
                table_ref, o_ref, table_bf_ref, xs_ref, st_ref, *, block_l):
    c_len = SSD_CHUNK
    halo = SUBLANES
    table_bf_ref[...] = table_ref[...].astype(BF16)

    @pl.when(pl.program_id(1) == 0)
    def _():
        xs_ref[0:halo, :] = jnp.zeros((halo, SSD_CONV_DIM), F32)
        st_ref[...] = jnp.zeros_like(st_ref)

    xs_ref[halo:halo + block_l, :] = xbc_ref[...]

    row = lax.broadcasted_iota(jnp.int32, (c_len, c_len), 0)
    col = lax.broadcasted_iota(jnp.int32, (c_len, c_len), 1)
    causal = row >= col
    upper = (row <= col).astype(F32)
    first_half = col < SSD_N
    cw = cw_ref[...]
    cb = cb_ref[...]
    a_neg = -jnp.exp(alog_ref[...])
    dtb = dtb_ref[...]
    head_pad = jnp.zeros((LANES - SSD_HEADS, c_len), F32)

    def chunk(c, carry):
        r0 = pl.multiple_of(c * c_len, c_len)
        rows = pl.ds(r0, c_len)
        window = xs_ref[pl.ds(r0, c_len + halo), :]
        conv = cb
        for tap in range(SSD_CONV):
            shift = halo - (SSD_CONV - 1) + tap
            conv = conv + cw[tap:tap + 1, :] * window[shift:shift + c_len, :]
        xc = _silu(conv)
        b_m = xc[:, SSD_WIDTH:SSD_WIDTH + SSD_BC]
        c_m = xc[:, SSD_WIDTH + SSD_BC:]

        dt_t = _softplus(dt_ref[rows, :].T[0:SSD_HEADS, :] + dtb)
        a_cum_t = _dot(dt_t * a_neg, upper, precision=HIGHEST)
        dt = jnp.concatenate([dt_t, head_pad], axis=0).T
        a_cum = jnp.concatenate([a_cum_t, head_pad], axis=0).T

        scores = []
        c_g = []
        for g in range(SSD_GROUPS):
            in_group = first_half if g == 0 else jnp.logical_not(first_half)
            c_g.append(jnp.where(in_group, c_m, 0.0).astype(BF16))
            scores.append(_dot_nt(c_g[g], b_m.astype(BF16)))

        for m in range(SSD_HEADS // 2):
            g = (2 * m) // (SSD_HEADS // SSD_GROUPS)
            lanes = slice(m * LANES, (m + 1) * LANES)
            x_pair = xc[:, lanes]
            halves = []
            dts = []
            for par in range(2):
                h = 2 * m + par
                dts.append(jnp.broadcast_to(dt[:, h:h + 1], (c_len, LANES)))
            xdt = (x_pair * jnp.where(first_half, dts[0], dts[1])).astype(BF16)
            for par in range(2):
                h = 2 * m + par
                a_col = jnp.broadcast_to(a_cum[:, h:h + 1], (c_len, c_len))
                a_row = jnp.broadcast_to(a_cum_t[h:h + 1, :], (c_len, c_len))
                a_end = a_col[c_len - 1:c_len, :]
                decay = jnp.exp(jnp.where(causal, a_col - a_row, -jnp.inf))
                y_diag = _dot((scores[g] * decay).astype(BF16), xdt)
                state = st_ref[h]
                y_off = _dot(c_g[g], state.astype(BF16)) * jnp.exp(a_col)
                halves.append(y_diag + y_off)
                b_dec = (b_m * jnp.exp(a_end - a_col)).astype(BF16)
                st_ref[h] = state * jnp.exp(a_end) + _dot_tn(b_dec, xdt)
            y = jnp.where(first_half, halves[0], halves[1]) + x_pair * dskip_ref[:, lanes]
            y = y * _silu(z_ref[rows, lanes])
            xs_pair_sq = jnp.sum(y * y, axis=-1, keepdims=True)
            if m % 2 == 0:
                y_prev, sq_prev = y, xs_pair_sq
            else:
                scale = lax.rsqrt((sq_prev + xs_pair_sq) * (1.0 / (2 * LANES)) + EPS)
                lo = slice((m - 1) * LANES, m * LANES)
                o_ref[rows, lo] = (y_prev * scale * nw_ref[:, lo]).astype(o_ref.dtype)
                o_ref[rows, lanes] = (y * scale * nw_ref[:, lanes]).astype(o_ref.dtype)
        return carry

    lax.fori_loop(0, block_l // c_len, chunk, 0)
    xs_ref[0:halo, :] = xs_ref[block_l:block_l + halo, :]


def _ssd(z, xbc, dt, cw, cb, dtb, alog, dskip, nw, table, batch, seq, block_l):
    t = batch * seq
    nblk = seq // block_l
    tok = lambda w: pl.BlockSpec((block_l, w), lambda b, j: (b * nblk + j, 0))
    const = lambda shape: pl.BlockSpec(shape, lambda b, j: (0, 0))
    table_spec = _table_slab_spec(table, batch, nblk)
    return pl.pallas_call(
        functools.partial(_ssd_kernel, block_l=block_l),
        grid=(batch, nblk),
        in_specs=[tok(SSD_WIDTH), tok(SSD_CONV_DIM), tok(LANES),
                  const((SSD_CONV, SSD_CONV_DIM)), const((1, SSD_CONV_DIM)),
                  const((SSD_HEADS, LANES)), const((SSD_HEADS, LANES)), const((1, SSD_WIDTH)),
                  const((1, SSD_WIDTH)), table_spec],
        out_specs=[tok(SSD_WIDTH), table_spec],
        out_shape=[jax.ShapeDtypeStruct((t, SSD_WIDTH), BF16),
                   jax.ShapeDtypeStruct(table.shape, BF16)],
        scratch_shapes=[pltpu.VMEM((block_l + SUBLANES, SSD_CONV_DIM), F32),
                        pltpu.VMEM((SSD_HEADS, SSD_BC, LANES), F32)],
        compiler_params=pltpu.CompilerParams(
            dimension_semantics=("parallel", "arbitrary"), vmem_limit_bytes=VMEM_LIMIT),
        name="ssd",
    )(z, xbc, dt, cw, cb, dtb, alog, dskip, nw, table)


def _out_query_kernel(og_ref, os_ref, x_ref, wo_ref, nw_ref, wq_ref, keys_ref,
                      h_ref, n_ref, s_ref, q_ref):
    mixed = _dot(og_ref[...], wo_ref[0:GLA_V, :]) + _dot(os_ref[...], wo_ref[GLA_V:, :])
    h = x_ref[...] + mixed
    h_ref[...] = h
    n = (h * _rms_scale(h) * nw_ref[...]).astype(BF16)
    n_ref[...] = n
    q_ref[...] = _dot(n, wq_ref[...]).astype(BF16)
    for hp in range(2 * PEER_HEADS):
        lanes = slice(hp * PEER_HALF, (hp + 1) * PEER_HALF)
        s_ref[hp] = _dot_nt(keys_ref[hp], q_ref[:, lanes])


def _out_query(o_gla, o_ssd, x2, w_out, norm_w, w_query, keys, block_t):
    t = x2.shape[0]
    n_hp = 2 * PEER_HEADS
    return pl.pallas_call(
        _out_query_kernel,
        grid=(t // block_t,),
        in_specs=[pl.BlockSpec((block_t, GLA_V), lambda i: (i, 0)),
                  pl.BlockSpec((block_t, SSD_WIDTH), lambda i: (i, 0)),
                  pl.BlockSpec((block_t, D_MODEL), lambda i: (i, 0)),
                  pl.BlockSpec((D_MODEL, D_MODEL), lambda i: (0, 0)),
                  pl.BlockSpec((1, D_MODEL), lambda i: (0, 0)),
                  pl.BlockSpec((D_MODEL, n_hp * PEER_HALF), lambda i: (0, 0)),
                  pl.BlockSpec((n_hp, PEER_KEYS, PEER_HALF), lambda i: (0, 0, 0))],
        out_specs=[pl.BlockSpec((block_t, D_MODEL), lambda i: (i, 0)),
                   pl.BlockSpec((block_t, D_MODEL), lambda i: (i, 0)),
                   pl.BlockSpec((n_hp, PEER_KEYS, block_t), lambda i: (0, 0, i))],
        out_shape=[jax.ShapeDtypeStruct((t, D_MODEL), F32),
                   jax.ShapeDtypeStruct((t, D_MODEL), BF16),
                   jax.ShapeDtypeStruct((n_hp, PEER_KEYS, t), F32)],
        scratch_shapes=[pltpu.VMEM((block_t, n_hp * PEER_HALF), BF16)],
        compiler_params=pltpu.CompilerParams(
            dimension_semantics=("parallel",), vmem_limit_bytes=VMEM_LIMIT),
        name="out_query",
    )(o_gla, o_ssd, x2, w_out, norm_w, w_query, keys)


def _extract_top(values, ids, count):
    lanes = values.shape[1]
    slot = lax.broadcasted_iota(jnp.int32, (count, lanes), 0)
    top_v = jnp.zeros((count, lanes), F32)
    top_i = jnp.zeros((count, lanes), F32)
    for r in range(count):
        m = jnp.max(values, axis=0, keepdims=True)
        sel = jnp.min(jnp.where(values == m, ids, jnp.inf), axis=0, keepdims=True)
        values = jnp.where(ids == sel, -jnp.inf, values)
        top_v = jnp.where(slot == r, m, top_v)
        top_i = jnp.where(slot == r, sel, top_i)
    return top_v, top_i, jnp.zeros((1, lanes), F32)


def _sorting_network(n):
    size = 1 << (n - 1).bit_length()
    pairs = []
    p = 1
    while p < size:
        k = p
        while k >= 1:
            for j in range(k % p, size - k, 2 * k):
                for i in range(min(k, size - j - k)):
                    if (i + j) // (2 * p) == (i + j + k) // (2 * p):
                        pairs.append((i + j, i + j + k))
            k //= 2
        p *= 2
    return [(a, b) for a, b in pairs if b < n]


def _pop_top(values, ids, count):
    rows, lanes = values.shape
    n = rows // SUBLANES
    vals = [values[v * SUBLANES:(v + 1) * SUBLANES] for v in range(n)]
    idl = [ids[v * SUBLANES:(v + 1) * SUBLANES] for v in range(n)]
    for a, b in _sorting_network(n):
        swap = vals[b] > vals[a]
        vals[a], vals[b] = jnp.where(swap, vals[b], vals[a]), jnp.where(swap, vals[a], vals[b])
        idl[a], idl[b] = jnp.where(swap, idl[b], idl[a]), jnp.where(swap, idl[a], idl[b])

    slot = lax.broadcasted_iota(jnp.int32, (count, lanes), 0)
    top_v = jnp.zeros((count, lanes), F32)
    top_i = jnp.zeros((count, lanes), F32)
    tie = jnp.zeros((1, lanes), F32)
    prev = None
    for r in range(count + 1):
        m = jnp.max(vals[0], axis=0, keepdims=True)
        if prev is not None:
            tie = jnp.where(m == prev, 1.0, tie)
        prev = m
        if r == count:
            break
        sel = jnp.min(jnp.where(vals[0] == m, idl[0], jnp.inf), axis=0, keepdims=True)
        top_v = jnp.where(slot == r, m, top_v)
        top_i = jnp.where(slot == r, sel, top_i)
        hit = idl[0] == sel
        depth = min(n - 1, count - r)
        for v in range(depth):
            vals[v] = jnp.where(hit, vals[v + 1], vals[v])
            idl[v] = jnp.where(hit, idl[v + 1], idl[v])
        vals[depth] = jnp.where(hit, -jnp.inf, vals[depth])
    return top_v, top_i, tie


def _gather_rows(table, index):
    row = lax.broadcasted_iota(jnp.int32, table.shape, 0).astype(F32)
    return jnp.sum(jnp.where(row == index, table, 0.0), axis=0, keepdims=True)


def _head_entries(s_ref, h, top_fn):
    k = PEER_TOPK
    lanes = LANES
    key_id = lax.broadcasted_iota(jnp.int32, (PEER_KEYS, lanes), 0).astype(F32)
    sub = lax.broadcasted_iota(jnp.int32, (SUBLANES, lanes), 0).astype(F32)
    slot = lax.broadcasted_iota(jnp.int32, (k, lanes), 0)

    s1, i1, tie1 = top_fn(s_ref[2 * h], key_id, k)
    s2, i2, tie2 = top_fn(s_ref[2 * h + 1], key_id, k)
    pieces, flats = [], []
    for b0 in (0, SUBLANES):
        pieces.append(s1[0:1, :] + s2[b0:b0 + SUBLANES, :])
        flats.append(sub + float(b0))
    for a in range(1, SUBLANES):
        limit = k // (a + 1)
        pieces.append(jnp.where(sub < float(limit), s1[a:a + 1, :] + s2[0:SUBLANES, :], -jnp.inf))
        flats.append(sub + float(a * k))
    pieces.append(s1[SUBLANES:k, :] + s2[0:1, :])
    flats.append((sub + float(SUBLANES)) * float(k))
    cand = jnp.concatenate(pieces, axis=0)
    flat = jnp.concatenate(flats, axis=0)
    best_s, best_flat, tie3 = top_fn(cand, flat, k)

    ent_i = jnp.zeros((k, lanes), F32)
    ent_j = jnp.zeros((k, lanes), F32)
    for r in range(k):
        pos = best_flat[r:r + 1, :]
        a_idx = jnp.floor(pos * (1.0 / k))
        b_idx = pos - a_idx * float(k)
        ent_i = jnp.where(slot == r, _gather_rows(i1, a_idx), ent_i)
        ent_j = jnp.where(slot == r, _gather_rows(i2, b_idx), ent_j)
    e = jnp.exp(best_s - best_s[0:1, :])
    gate = e / jnp.sum(e, axis=0, keepdims=True)
    return ent_i, ent_j, gate, jnp.maximum(jnp.maximum(tie1, tie2), tie3)


def _topk_kernel(s_ref, w_ref, ei_ref, ej_ref, eg_ref, tie_ref, pi_ref, pj_ref, pg_ref):
    k = PEER_TOPK
    half_tile = LANES // 2

    @pl.when(pl.program_id(0) == 0)
    def _():
        pi_ref[...] = jnp.zeros_like(pi_ref)
        pj_ref[...] = jnp.zeros_like(pj_ref)
        pg_ref[...] = jnp.zeros_like(pg_ref)

    grid_row = lax.broadcasted_iota(jnp.int32, (PEER_KEYS, PEER_ENTRIES), 0).astype(F32)

    def gate_grid(t):
        i_row = pi_ref[pl.ds(t, 1), :]
        j_row = pj_ref[pl.ds(t, 1), :]
        g_row = pg_ref[pl.ds(t, 1), :]
        a_t = jnp.where(grid_row == i_row, g_row, 0.0).astype(BF16)
        b_t = jnp.where(grid_row == j_row, 1.0, 0.0).astype(BF16)
        return _dot_nt(a_t, b_t)

    def store(h, ent_i, ent_j, gate):
        rows = pl.ds(pl.multiple_of(h * k, k), k)
        ei_ref[rows, :] = ent_i
        ej_ref[rows, :] = ent_j
        eg_ref[rows, :] = gate

    def fast_head(h, carry):
        ent_i, ent_j, gate, tie = _head_entries(s_ref, h, _pop_top)
        store(h, ent_i, ent_j, gate)
        tie_ref[pl.ds(h, 1), :] = tie
        for n in range(half_tile // PEER_HEADS):
            q = h * (half_tile // PEER_HEADS) + n
            packed = pltpu.pack_elementwise([gate_grid(q), gate_grid(q + half_tile)],
                                            packed_dtype=BF16)
            words = lax.bitcast_convert_type(packed, jnp.uint32)
            rows = pl.ds(pl.multiple_of(q * PEER_GRID_ROWS, PEER_GRID_ROWS), PEER_GRID_ROWS)
            for e in range(PEER_KEYS // PEER_GRID_ROWS):
                w_ref[e, rows, :] = words[e * PEER_GRID_ROWS:(e + 1) * PEER_GRID_ROWS, :]
        return carry

    def exact_head(h, carry):
        @pl.when(jnp.max(tie_ref[pl.ds(h, 1), :]) > 0.0)
        def _():
            ent_i, ent_j, gate, _ = _head_entries(s_ref, h, _extract_top)
            store(h, ent_i, ent_j, gate)
        return carry

    lax.fori_loop(0, PEER_HEADS, fast_head, 0, unroll=2)

    @pl.when(jnp.max(tie_ref[...]) > 0.0)
    def _():
        lax.fori_loop(0, PEER_HEADS, exact_head, 0)

    pi_ref[...] = ei_ref[...].T
    pj_ref[...] = ej_ref[...].T
    pg_ref[...] = eg_ref[...].T


def _topk(scores_t):
    n_hp, n_keys, t = scores_t.shape
    n_tiles = t // LANES
    tiles_per_block = PEER_BLOCK_T // LANES
    e_steps = PEER_KEYS // PEER_GRID_ROWS
    tile_rows = LANES // 2 * PEER_GRID_ROWS

    def out_index(g):
        tile = jnp.maximum(g - 1, 0)
        return (tile // tiles_per_block, 0, tile % tiles_per_block, 0)

    return pl.pallas_call(
        _topk_kernel,
        grid=(n_tiles + 1,),
        in_specs=[pl.BlockSpec((n_hp, n_keys, LANES),
                               lambda g: (0, 0, jnp.minimum(g, n_tiles - 1)))],
        out_specs=pl.BlockSpec((None, e_steps, tile_rows, PEER_KEYS), out_index),
        out_shape=jax.ShapeDtypeStruct(
            (t // PEER_BLOCK_T, e_steps, tiles_per_block * tile_rows, PEER_KEYS), jnp.uint32),
        scratch_shapes=[pltpu.VMEM((PEER_ENTRIES, LANES), F32)] * 3
        + [pltpu.VMEM((PEER_HEADS, LANES), F32)]
        + [pltpu.VMEM((LANES, PEER_ENTRIES), F32)] * 3,
        compiler_params=pltpu.CompilerParams(
            dimension_semantics=("arbitrary",), vmem_limit_bytes=VMEM_LIMIT),
        name="topk",
    )(scores_t)


PAIR = 2 * PEER_KEYS


def _gelu(x):
    return 0.5 * x * (1.0 + lax.erf(x * (2.0 ** -0.5)))


def _peer_kernel(n_ref, w_ref, u_ref, v_ref, h_ref, nw_ref, o_ref, hid_ref,
                 *, block_t, block_e):
    e_step = pl.program_id(1)
    n_pairs = block_e // PAIR
    grid_rows = block_e // PEER_KEYS
    half_tile = LANES // 2

    @pl.when(e_step == 0)
    def _():
        o_ref[...] = jnp.zeros_like(o_ref)

    def gate_rows(i):
        words = w_ref[pl.ds(i, block_t // 2, stride=grid_rows), :]
        lo, hi = [pltpu.unpack_elementwise(words, index=k, packed_dtype=BF16, unpacked_dtype=F32)
                  for k in range(2)]
        pieces = []
        for tile in range(block_t // LANES):
            rows = slice(tile * half_tile, (tile + 1) * half_tile)
            pieces += [lo[rows], hi[rows]]
        return jnp.concatenate(pieces, axis=0)

    n = n_ref[...]
    for p in range(n_pairs):
        experts = slice(p * PAIR, (p + 1) * PAIR)
        act = _dot_nt(n, u_ref[experts, :])
        w_pair = jnp.concatenate([gate_rows(2 * p), gate_rows(2 * p + 1)], axis=-1)
        hid_ref[:, experts] = (_gelu(act) * w_pair).astype(BF16)
    o_ref[...] += _dot(hid_ref[...], v_ref[...])

    @pl.when(e_step == pl.num_programs(1) - 1)
    def _():
        h = h_ref[...] + o_ref[...]
        o_ref[...] = h * _rms_scale(h) * nw_ref[...]


def _peer(n2, w_packed, u, v, h1, norm_w, block_t, block_e):
    t = n2.shape[0]
    n_exp = u.shape[0]
    tok = lambda w: pl.BlockSpec((block_t, w), lambda i, e: (i, 0))
    return pl.pallas_call(
        functools.partial(_peer_kernel, block_t=block_t, block_e=block_e),
        grid=(t // block_t, n_exp // block_e),
        in_specs=[tok(D_MODEL),
                  pl.BlockSpec((None, None, block_t // 2 * (block_e // PEER_KEYS), PEER_KEYS),
                               lambda i, e: (i, e, 0, 0)),
                  pl.BlockSpec((block_e, D_MODEL), lambda i, e: (e, 0)),
                  pl.BlockSpec((block_e, D_MODEL), lambda i, e: (e, 0)),
                  tok(D_MODEL),
                  pl.BlockSpec((1, D_MODEL), lambda i, e: (0, 0))],
        out_specs=tok(D_MODEL),
        out_shape=jax.ShapeDtypeStruct((t, D_MODEL), F32),
        scratch_shapes=[pltpu.VMEM((block_t, block_e), BF16)],
        compiler_params=pltpu.CompilerParams(
            dimension_semantics=("parallel", "arbitrary"), vmem_limit_bytes=VMEM_LIMIT),
        name="peer",
    )(n2, w_packed, u, v, h1, norm_w)


def _pad_cols(w, width):
    return jnp.pad(w, ((0, 0), (0, width - w.shape[1])))


def _in_proj_slabs(w_in):
    parts, start = [], 0
    for size in _IN_SIZES:
        parts.append(w_in[:, start:start + size])
        start += size
    q, k, v, gate_lr, g_out, z, xbc, dt = parts
    return jnp.concatenate(
        [q, k, v, g_out, z, xbc, _pad_cols(gate_lr, LANES), _pad_cols(dt, LANES)],
        axis=1).astype(BF16)


def _layer(x2, batch, seq, p, peer_u, peer_v):
    row = lambda a: a.reshape(1, -1).astype(F32)
    q, k, v, g_out, z, xbc, gate_lr, dt = _in_proj(
        x2, row(p["norm_mix_w"]), _in_proj_slabs(p["w_in"]), block_t=512)

    w2p = jnp.pad(p["gla_w_gate2"].astype(F32), ((0, LANES - GLA_RANK), (0, 0)))
    o_gla, u_bf = _gla(q, k, v, gate_lr, g_out, w2p, row(p["gla_b_gate"]), row(p["gla_norm_w"]),
                       peer_u.astype(F32), batch, seq, block_l=512)

    head_rows = lambda a: jnp.broadcast_to(a.astype(F32).reshape(SSD_HEADS, 1), (SSD_HEADS, LANES))
    d_skip = jnp.repeat(p["ssd_d"].astype(F32), SSD_P).reshape(1, SSD_WIDTH)
    o_ssd, v_bf = _ssd(z, xbc, dt, p["ssd_conv_w"].astype(F32), row(p["ssd_conv_b"]),
                       head_rows(p["ssd_dt_bias"]), head_rows(p["ssd_a_log"]), d_skip,
                       row(p["ssd_norm_w"]), peer_v.astype(F32), batch, seq, block_l=512)

    keys = p["peer_sub_keys"].reshape(2 * PEER_HEADS, PEER_KEYS, PEER_HALF).astype(BF16)
    h1, n2, scores_t = _out_query(
        o_gla, o_ssd, x2, p["w_out"].astype(BF16), row(p["norm_ffn_w"]),
        p["peer_w_query"].astype(BF16), keys, block_t=512)

    return n2, _topk(scores_t), h1, u_bf, v_bf


def kernel(x, norm_mix_w, w_in, gla_w_gate2, gla_b_gate, gla_norm_w, ssd_conv_w, ssd_conv_b,
           ssd_dt_bias, ssd_a_log, ssd_d, ssd_norm_w, w_out, norm_ffn_w, peer_w_query,
           peer_sub_keys, peer_u, peer_v, norm_final_w):
    batch, seq, d = x.shape
    assert w_in.shape[0] == 1, "single-layer trunk"
    params = dict(norm_mix_w=norm_mix_w, w_in=w_in, gla_w_gate2=gla_w_gate2, gla_b_gate=gla_b_gate,
                  gla_norm_w=gla_norm_w, ssd_conv_w=ssd_conv_w, ssd_conv_b=ssd_conv_b,
                  ssd_dt_bias=ssd_dt_bias, ssd_a_log=ssd_a_log, ssd_d=ssd_d, ssd_norm_w=ssd_norm_w,
                  w_out=w_out, norm_ffn_w=norm_ffn_w, peer_w_query=peer_w_query,
                  peer_sub_keys=peer_sub_keys)
    p = {name: value[0] for name, value in params.items()}
    x2 = x.reshape(batch * seq, d).astype(F32)
    n2, w_packed, h1, u_bf, v_bf = _layer(x2, batch, seq, p, peer_u[0], peer_v[0])
    y = _peer(n2, w_packed, u_bf, v_bf, h1,
              norm_final_w.reshape(1, d).astype(F32), block_t=PEER_BLOCK_T, block_e=PEER_BLOCK_E)
    return y.reshape(batch, seq, d).astype(x.dtype)
```

```python
import functools

import jax
import jax.numpy as jnp
from jax import lax
from jax.experimental import pallas as pl
from jax.experimental.pallas import tpu as pltpu

F32 = jnp.float32
BF16 = jnp.bfloat16
HIGHEST = lax.Precision.HIGHEST

EPS = 1e-6
D_MODEL = 1024

GLA_HEADS = 4
GLA_DK = 64
GLA_DV = 128
GLA_QK = GLA_HEADS * GLA_DK
GLA_V = GLA_HEADS * GLA_DV
GLA_RANK = 16
GLA_NORMALIZER = 16.0
GLA_CHUNK = 64
GLA_CUMSUM_ROWS = 256

SSD_HEADS = 8
SSD_P = 64
SSD_WIDTH = SSD_HEADS * SSD_P
SSD_GROUPS = 2
SSD_N = 64
SSD_CONV = 4
SSD_CHUNK = 128
SSD_BC = SSD_GROUPS * SSD_N
SSD_CONV_DIM = SSD_WIDTH + 2 * SSD_BC

PEER_HEADS = 8
PEER_KEYS = 128
PEER_TOPK = 16
PEER_HALF = 128
PEER_ENTRIES = PEER_HEADS * PEER_TOPK
PEER_BLOCK_T = 1024
PEER_BLOCK_E = 2048
PEER_GRID_ROWS = PEER_BLOCK_E // PEER_KEYS

LANES = 128
SUBLANES = 8
VMEM_LIMIT = 56 * 1024 * 1024

_IN_SIZES = (GLA_QK, GLA_QK, GLA_V, GLA_RANK, GLA_V, SSD_WIDTH, SSD_CONV_DIM, SSD_HEADS)
_SLAB_WIDTHS = (GLA_QK, GLA_QK, GLA_V, GLA_V, SSD_WIDTH, SSD_CONV_DIM, LANES, LANES)


def _dot(a, b, dims=((1,), (0,)), precision=None):
    return lax.dot_general(a, b, (dims, ((), ())), precision=precision,
                           preferred_element_type=F32)


def _dot_nt(a, b, precision=None):
    return _dot(a, b, ((1,), (1,)), precision)


def _dot_tn(a, b, precision=None):
    return _dot(a, b, ((0,), (0,)), precision)


def _silu(x):
    return x * (1.0 / (1.0 + jnp.exp(-x)))


def _softplus(x):
    return jnp.maximum(x, 0.0) + jnp.log(1.0 + jnp.exp(-jnp.abs(x)))


def _rms_scale(x):
    return lax.rsqrt(jnp.mean(x * x, axis=-1, keepdims=True) + EPS)


def _in_proj_kernel(x_ref, nw_ref, w_ref, *out_refs):
    x = x_ref[...]
    n = (x * _rms_scale(x) * nw_ref[...]).astype(BF16)
    start = 0
    for o_ref, width in zip(out_refs, _SLAB_WIDTHS):
        o_ref[...] = _dot(n, w_ref[:, start:start + width])
        start += width


def _in_proj(x2, norm_w, w_slabs, block_t):
    t = x2.shape[0]
    total = sum(_SLAB_WIDTHS)
    out_shape = [jax.ShapeDtypeStruct((t, w), F32) for w in _SLAB_WIDTHS]
    out_specs = [pl.BlockSpec((block_t, w), lambda i: (i, 0)) for w in _SLAB_WIDTHS]
    return pl.pallas_call(
        _in_proj_kernel,
        grid=(t // block_t,),
        in_specs=[pl.BlockSpec((block_t, D_MODEL), lambda i: (i, 0)),
                  pl.BlockSpec((1, D_MODEL), lambda i: (0, 0)),
                  pl.BlockSpec((D_MODEL, total), lambda i: (0, 0))],
        out_specs=out_specs,
        out_shape=out_shape,
        compiler_params=pltpu.CompilerParams(
            dimension_semantics=("parallel",), vmem_limit_bytes=VMEM_LIMIT),
        name="in_proj",
    )(x2, norm_w, w_slabs)


def _gla_kernel(q_ref, k_ref, v_ref, glr_ref, gout_ref, w2_ref, bg_ref, nw_ref, table_ref,
                o_ref, table_bf_ref, st_ref, *, n_chunks):
    c_len = GLA_CHUNK
    table_bf_ref[...] = table_ref[...].astype(BF16)

    @pl.when(pl.program_id(1) == 0)
    def _():
        st_ref[...] = jnp.zeros_like(st_ref)

    def iota(shape, dim):
        return lax.broadcasted_iota(jnp.int32, shape, dim)

    block_l = n_chunks * c_len
    gate = _dot(glr_ref[...], w2_ref[...], precision=HIGHEST) + bg_ref[...]
    log_a = -_softplus(-gate) * (1.0 / GLA_NORMALIZER)
    group = GLA_CUMSUM_ROWS
    tri = ((iota((group, group), 0) // c_len == iota((group, group), 1) // c_len)
           & (iota((group, group), 0) >= iota((group, group), 1))).astype(F32)
    g_cum = jnp.concatenate(
        [_dot(tri, log_a[g * group:(g + 1) * group], precision=HIGHEST)
         for g in range(block_l // group)], axis=0)
    g_end = [g_cum[(c + 1) * c_len - 1:(c + 1) * c_len, :] for c in range(n_chunks)]
    g_last = jnp.concatenate([jnp.broadcast_to(g, (c_len, GLA_QK)) for g in g_end], axis=0)
    k = k_ref[...]
    q_in = (q_ref[...] * (GLA_DK ** -0.5) * jnp.exp(g_cum)).astype(BF16)
    k_in = k * jnp.exp(-g_cum)
    k_end = (k * jnp.exp(g_last - g_cum)).astype(BF16)
    v = v_ref[...]

    k_bd_mask = iota((GLA_QK, GLA_QK), 0) // c_len == iota((GLA_QK, GLA_QK), 1) // GLA_DK
    v_bd_mask = iota((GLA_QK, GLA_V), 0) // c_len == iota((GLA_QK, GLA_V), 1) // GLA_DV
    causal = iota((c_len, GLA_QK), 0) >= iota((c_len, GLA_QK), 1) % c_len
    state_mask = iota((GLA_V, GLA_QK), 0) // GLA_DV == iota((GLA_V, GLA_QK), 1) // GLA_DK

    att = []
    for c in range(n_chunks):
        rows = slice(c * c_len, (c + 1) * c_len)
        k_bd = jnp.where(k_bd_mask, jnp.concatenate([k_in[rows]] * GLA_HEADS, axis=0), 0.0)
        scores = _dot_nt(q_in[rows], k_bd.astype(BF16))
        att.append(jnp.where(causal, scores, 0.0).astype(BF16))

    state_t = st_ref[...]
    nw = jnp.concatenate([nw_ref[...]] * GLA_HEADS, axis=1)
    for c in range(n_chunks):
        rows = slice(c * c_len, (c + 1) * c_len)
        v_c = v[rows]
        new_state = _dot_tn(v_c.astype(BF16), k_end[rows])
        o_inter = _dot_nt(q_in[rows], state_t.astype(BF16))
        state_t = state_t * jnp.exp(g_end[c]) + jnp.where(state_mask, new_state, 0.0)
        v_bd = jnp.where(v_bd_mask, jnp.concatenate([v_c] * GLA_HEADS, axis=0), 0.0)
        o = _dot(att[c], v_bd.astype(BF16)) + o_inter
        outs = []
        for h in range(GLA_HEADS):
            o_h = o[:, h * GLA_DV:(h + 1) * GLA_DV]
            outs.append(o_h * _rms_scale(o_h))
        o = jnp.concatenate(outs, axis=1) * nw
        o_ref[rows, :] = (o * _silu(gout_ref[rows, :])).astype(o_ref.dtype)
    st_ref[...] = state_t


def _table_slab_spec(table, batch, nblk):
    rows, width = table.shape
    slab = rows // (batch * nblk)
    assert slab * batch * nblk == rows and slab % (2 * SUBLANES) == 0
    return pl.BlockSpec((slab, width), lambda b, j: (b * nblk + j, 0))


def _gla(q, k, v, glr, gout, w2p, bg, nw, table, batch, seq, block_l):
    t = batch * seq
    nblk = seq // block_l
    tok = lambda w: pl.BlockSpec((block_l, w), lambda b, j: (b * nblk + j, 0))
    const = lambda shape: pl.BlockSpec(shape, lambda b, j: (0, 0))
    table_spec = _table_slab_spec(table, batch, nblk)
    return pl.pallas_call(
        functools.partial(_gla_kernel, n_chunks=block_l // GLA_CHUNK),
        grid=(batch, nblk),
        in_specs=[tok(GLA_QK), tok(GLA_QK), tok(GLA_V), tok(LANES), tok(GLA_V),
                  const((LANES, GLA_QK)), const((1, GLA_QK)), const((1, GLA_DV)), table_spec],
        out_specs=[tok(GLA_V), table_spec],
        out_shape=[jax.ShapeDtypeStruct((t, GLA_V), BF16),
                   jax.ShapeDtypeStruct(table.shape, BF16)],
        scratch_shapes=[pltpu.VMEM((GLA_V, GLA_QK), F32)],
        compiler_params=pltpu.CompilerParams(
            dimension_semantics=("parallel", "arbitrary"), vmem_limit_bytes=VMEM_LIMIT),
        name="gla",
    )(q, k, v, glr, gout, w2p, bg, nw, table)


def _ssd_kernel(z_ref, xbc_ref, dt_ref, cw_ref, cb_ref, dtb_ref, alog_ref, dskip_ref, nw_ref,
                table_ref, o_ref, table_bf_ref, xs_ref, st_ref, *, block_l):
    c_len = SSD_CHUNK
    halo = SUBLANES
    table_bf_ref[...] = table_ref[...].astype(BF16)

    @pl.when(pl.program_id(1) == 0)
    def _():
        xs_ref[0:halo, :] = jnp.zeros((halo, SSD_CONV_DIM), F32)
        st_ref[...] = jnp.zeros_like(st_ref)

    xs_ref[halo:halo + block_l, :] = xbc_ref[...]

    row = lax.broadcasted_iota(jnp.int32, (c_len, c_len), 0)
    col = lax.broadcasted_iota(jnp.int32, (c_len, c_len), 1)
    causal = row >= col
    upper = (row <= col).astype(F32)
    first_half = col < SSD_N
    cw = cw_ref[...]
    cb = cb_ref[...]
    a_neg = -jnp.exp(alog_ref[...])
    dtb = dtb_ref[...]
    head_pad = jnp.zeros((LANES - SSD_HEADS, c_len), F32)

    def chunk(c, carry):
        r0 = pl.multiple_of(c * c_len, c_len)
        rows = pl.ds(r0, c_len)
        window = xs_ref[pl.ds(r0, c_len + halo), :]
        conv = cb
        for tap in reversed(range(SSD_CONV)):
            shift = halo - (SSD_CONV - 1) + tap
            conv = conv + cw[tap:tap + 1, :] * window[shift:shift + c_len, :]
        xc = _silu(conv)
        b_m = xc[:, SSD_WIDTH:SSD_WIDTH + SSD_BC]
        c_m = xc[:, SSD_WIDTH + SSD_BC:]

        dt_t = _softplus(dt_ref[rows, :].T[0:SSD_HEADS, :] + dtb)
        a_cum_t = _dot(dt_t * a_neg, upper, precision=HIGHEST)
        dt = jnp.concatenate([dt_t, head_pad], axis=0).T
        a_cum = jnp.concatenate([a_cum_t, head_pad], axis=0).T

        scores = []
        c_g = []
        for g in range(SSD_GROUPS):
            in_group = first_half if g == 0 else jnp.logical_not(first_half)
            c_g.append(jnp.where(in_group, c_m, 0.0).astype(BF16))
            scores.append(_dot_nt(c_g[g], b_m.astype(BF16)))

        for m in range(SSD_HEADS // 2):
            g = (2 * m) // (SSD_HEADS // SSD_GROUPS)
            lanes = slice(m * LANES, (m + 1) * LANES)
            x_pair = xc[:, lanes]
            halves = []
            dts = []
            for par in range(2):
                h = 2 * m + par
                dts.append(jnp.broadcast_to(dt[:, h:h + 1], (c_len, LANES)))
            xdt = (x_pair * jnp.where(first_half, dts[0], dts[1])).astype(BF16)
            for par in range(2):
                h = 2 * m + par
                a_col = jnp.broadcast_to(a_cum[:, h:h + 1], (c_len, c_len))
                a_row = jnp.broadcast_to(a_cum_t[h:h + 1, :], (c_len, c_len))
                a_end = a_col[c_len - 1:c_len, :]
                decay = jnp.exp(jnp.where(causal, a_col - a_row, -jnp.inf))
                y_diag = _dot((scores[g] * decay).astype(BF16), xdt)
                state = st_ref[h]
                y_off = _dot(c_g[g], state.astype(BF16)) * jnp.exp(a_col)
                halves.append(y_diag + y_off)
                b_dec = (b_m * jnp.exp(a_end - a_col)).astype(BF16)
                st_ref[h] = state * jnp.exp(a_end) + _dot_tn(b_dec, xdt)
            y = jnp.where(first_half, halves[0], halves[1]) + x_pair * dskip_ref[:, lanes]
            y = y * _silu(z_ref[rows, lanes])
            xs_pair_sq = jnp.sum(y * y, axis=-1, keepdims=True)
            if m % 2 == 0:
                y_prev, sq_prev = y, xs_pair_sq
            else:
                scale = lax.rsqrt((sq_prev + xs_pair_sq) * (1.0 / (2 * LANES)) + EPS)
                lo = slice((m - 1) * LANES, m * LANES)
                o_ref[rows, lo] = (y_prev * scale * nw_ref[:, lo]).astype(o_ref.dtype)
                o_ref[rows, lanes] = (y * scale * nw_ref[:, lanes]).astype(o_ref.dtype)
        return carry

    lax.fori_loop(0, block_l // c_len, chunk, 0)
    xs_ref[0:halo, :] = xs_ref[block_l:block_l + halo, :]


def _ssd(z, xbc, dt, cw, cb, dtb, alog, dskip, nw, table, batch, seq, block_l):
    t = batch * seq
    nblk = seq // block_l
    tok = lambda w: pl.BlockSpec((block_l, w), lambda b, j: (b * nblk + j, 0))
    const = lambda shape: pl.BlockSpec(shape, lambda b, j: (0, 0))
    table_spec = _table_slab_spec(table, batch, nblk)
    return pl.pallas_call(
        functools.partial(_ssd_kernel, block_l=block_l),
        grid=(batch, nblk),
        in_specs=[tok(SSD_WIDTH), tok(SSD_CONV_DIM), tok(LANES),
                  const((SSD_CONV, SSD_CONV_DIM)), const((1, SSD_CONV_DIM)),
                  const((SSD_HEADS, LANES)), const((SSD_HEADS, LANES)), const((1, SSD_WIDTH)),
                  const((1, SSD_WIDTH)), table_spec],
        out_specs=[tok(SSD_WIDTH), table_spec],
        out_shape=[jax.ShapeDtypeStruct((t, SSD_WIDTH), BF16),
                   jax.ShapeDtypeStruct(table.shape, BF16)],
        scratch_shapes=[pltpu.VMEM((block_l + SUBLANES, SSD_CONV_DIM), F32),
                        pltpu.VMEM((SSD_HEADS, SSD_BC, LANES), F32)],
        compiler_params=pltpu.CompilerParams(
            dimension_semantics=("parallel", "arbitrary"), vmem_limit_bytes=VMEM_LIMIT),
        name="ssd",
    )(z, xbc, dt, cw, cb, dtb, alog, dskip, nw, table)


def _out_query_kernel(og_ref, os_ref, x_ref, wo_ref, nw_ref, wq_ref, keys_ref,
                      h_ref, n_ref, s_ref, q_ref):
    mixed = _dot(og_ref[...], wo_ref[0:GLA_V, :]) + _dot(os_ref[...], wo_ref[GLA_V:, :])
    h = x_ref[...] + mixed
    h_ref[...] = h
    n = (h * _rms_scale(h) * nw_ref[...]).astype(BF16)
    n_ref[...] = n
    q_ref[...] = _dot(n, wq_ref[...]).astype(BF16)
    for hp in range(2 * PEER_HEADS):
        lanes = slice(hp * PEER_HALF, (hp + 1) * PEER_HALF)
        s_ref[hp] = _dot_nt(keys_ref[hp], q_ref[:, lanes])


def _out_query(o_gla, o_ssd, x2, w_out, norm_w, w_query, keys, block_t):
    t = x2.shape[0]
    n_hp = 2 * PEER_HEADS
    return pl.pallas_call(
        _out_query_kernel,
        grid=(t // block_t,),
        in_specs=[pl.BlockSpec((block_t, GLA_V), lambda i: (i, 0)),
                  pl.BlockSpec((block_t, SSD_WIDTH), lambda i: (i, 0)),
                  pl.BlockSpec((block_t, D_MODEL), lambda i: (i, 0)),
                  pl.BlockSpec((D_MODEL, D_MODEL), lambda i: (0, 0)),
                  pl.BlockSpec((1, D_MODEL), lambda i: (0, 0)),
                  pl.BlockSpec((D_MODEL, n_hp * PEER_HALF), lambda i: (0, 0)),
                  pl.BlockSpec((n_hp, PEER_KEYS, PEER_HALF), lambda i: (0, 0, 0))],
        out_specs=[pl.BlockSpec((block_t, D_MODEL), lambda i: (i, 0)),
                   pl.BlockSpec((block_t, D_MODEL), lambda i: (i, 0)),
                   pl.BlockSpec((n_hp, PEER_KEYS, block_t), lambda i: (0, 0, i))],
        out_shape=[jax.ShapeDtypeStruct((t, D_MODEL), F32),
                   jax.ShapeDtypeStruct((t, D_MODEL), BF16),
                   jax.ShapeDtypeStruct((n_hp, PEER_KEYS, t), F32)],
        scratch_shapes=[pltpu.VMEM((block_t, n_hp * PEER_HALF), BF16)],
        compiler_params=pltpu.CompilerParams(
            dimension_semantics=("parallel",), vmem_limit_bytes=VMEM_LIMIT),
        name="out_query",
    )(o_gla, o_ssd, x2, w_out, norm_w, w_query, keys)


def _extract_top(values, ids, count):
    lanes = values.shape[1]
    slot = lax.broadcasted_iota(jnp.int32, (count, lanes), 0)
    top_v = jnp.zeros((count, lanes), F32)
    top_i = jnp.zeros((count, lanes), F32)
    for r in range(count):
        m = jnp.max(values, axis=0, keepdims=True)
        sel = jnp.min(jnp.where(values == m, ids, jnp.inf), axis=0, keepdims=True)
        values = jnp.where(ids == sel, -jnp.inf, values)
        top_v = jnp.where(slot == r, m, top_v)
        top_i = jnp.where(slot == r, sel, top_i)
    return top_v, top_i, jnp.zeros((1, lanes), F32)


def _sorting_network(n):
    size = 1 << (n - 1).bit_length()
    pairs = []
    p = 1
    while p < size:
        k = p
        while k >= 1:
            for j in range(k % p, size - k, 2 * k):
                for i in range(min(k, size - j - k)):
                    if (i + j) // (2 * p) == (i + j + k) // (2 * p):
                        pairs.append((i + j, i + j + k))
            k //= 2
        p *= 2
    return [(a, b) for a, b in pairs if b < n]


def _pop_top(values, ids, count):
    rows, lanes = values.shape
    n = rows // SUBLANES
    vals = [values[v * SUBLANES:(v + 1) * SUBLANES] for v in range(n)]
    idl = [ids[v * SUBLANES:(v + 1) * SUBLANES] for v in range(n)]
    for a, b in _sorting_network(n):
        swap = vals[b] > vals[a]
        vals[a], vals[b] = jnp.where(swap, vals[b], vals[a]), jnp.where(swap, vals[a], vals[b])
        idl[a], idl[b] = jnp.where(swap, idl[b], idl[a]), jnp.where(swap, idl[a], idl[b])

    slot = lax.broadcasted_iota(jnp.int32, (count, lanes), 0)
    top_v = jnp.zeros((count, lanes), F32)
    top_i = jnp.zeros((count, lanes), F32)
    tie = jnp.zeros((1, lanes), F32)
    prev = None
    for r in range(count + 1):
        m = jnp.max(vals[0], axis=0, keepdims=True)
        if prev is not None:
            tie = jnp.where(m == prev, 1.0, tie)
        prev = m
        if r == count:
            break
        sel = jnp.min(jnp.where(vals[0] == m, idl[0], jnp.inf), axis=0, keepdims=True)
        top_v = jnp.where(slot == r, m, top_v)
        top_i = jnp.where(slot == r, sel, top_i)
        hit = idl[0] == sel
        depth = min(n - 1, count - r)
        for v in range(depth):
            vals[v] = jnp.where(hit, vals[v + 1], vals[v])
            idl[v] = jnp.where(hit, idl[v + 1], idl[v])
        vals[depth] = jnp.where(hit, -jnp.inf, vals[depth])
    return top_v, top_i, tie


def _gather_rows(table, index):
    row = lax.broadcasted_iota(jnp.int32, table.shape, 0).astype(F32)
    return jnp.sum(jnp.where(row == index, table, 0.0), axis=0, keepdims=True)


def _head_entries(s_ref, h, top_fn):
    k = PEER_TOPK
    lanes = LANES
    key_id = lax.broadcasted_iota(jnp.int32, (PEER_KEYS, lanes), 0).astype(F32)
    sub = lax.broadcasted_iota(jnp.int32, (SUBLANES, lanes), 0).astype(F32)
    slot = lax.broadcasted_iota(jnp.int32, (k, lanes), 0)

    s1, i1, tie1 = top_fn(s_ref[2 * h], key_id, k)
    s2, i2, tie2 = top_fn(s_ref[2 * h + 1], key_id, k)
    pieces, flats = [], []
    for b0 in (0, SUBLANES):
        pieces.append(s1[0:1, :] + s2[b0:b0 + SUBLANES, :])
        flats.append(sub + float(b0))
    for a in range(1, SUBLANES):
        limit = k // (a + 1)
        pieces.append(jnp.where(sub < float(limit), s1[a:a + 1, :] + s2[0:SUBLANES, :], -jnp.inf))
        flats.append(sub + float(a * k))
    pieces.append(s1[SUBLANES:k, :] + s2[0:1, :])
    flats.append((sub + float(SUBLANES)) * float(k))
    cand = jnp.concatenate(pieces, axis=0)
    flat = jnp.concatenate(flats, axis=0)
    best_s, best_flat, tie3 = top_fn(cand, flat, k)

    ent_i = jnp.zeros((k, lanes), F32)
    ent_j = jnp.zeros((k, lanes), F32)
    for r in range(k):
        pos = best_flat[r:r + 1, :]
        a_idx = jnp.floor(pos * (1.0 / k))
        b_idx = pos - a_idx * float(k)
        ent_i = jnp.where(slot == r, _gather_rows(i1, a_idx), ent_i)
        ent_j = jnp.where(slot == r, _gather_rows(i2, b_idx), ent_j)
    e = jnp.exp(best_s - best_s[0:1, :])
    gate = e / jnp.sum(e, axis=0, keepdims=True)
    return ent_i, ent_j, gate, jnp.maximum(jnp.maximum(tie1, tie2), tie3)


def _topk_kernel(s_ref, w_ref, ei_ref, ej_ref, eg_ref, tie_ref, pi_ref, pj_ref, pg_ref):
    k = PEER_TOPK
    half_tile = LANES // 2

    @pl.when(pl.program_id(0) == 0)
    def _():
        pi_ref[...] = jnp.zeros_like(pi_ref)
        pj_ref[...] = jnp.zeros_like(pj_ref)
        pg_ref[...] = jnp.zeros_like(pg_ref)

    grid_row = lax.broadcasted_iota(jnp.int32, (PEER_KEYS, PEER_ENTRIES), 0).astype(F32)

    def gate_grid(t):
        i_row = pi_ref[pl.ds(t, 1), :]
        j_row = pj_ref[pl.ds(t, 1), :]
        g_row = pg_ref[pl.ds(t, 1), :]
        a_t = jnp.where(grid_row == i_row, g_row, 0.0).astype(BF16)
        b_t = jnp.where(grid_row == j_row, 1.0, 0.0).astype(BF16)
        return _dot_nt(a_t, b_t)

    def store(h, ent_i, ent_j, gate):
        rows = pl.ds(pl.multiple_of(h * k, k), k)
        ei_ref[rows, :] = ent_i
        ej_ref[rows, :] = ent_j
        eg_ref[rows, :] = gate

    def fast_head(h, carry):
        ent_i, ent_j, gate, tie = _head_entries(s_ref, h, _pop_top)
        store(h, ent_i, ent_j, gate)
        tie_ref[pl.ds(h, 1), :] = tie
        for n in range(half_tile // PEER_HEADS):
            q = h * (half_tile // PEER_HEADS) + n
            packed = pltpu.pack_elementwise([gate_grid(q), gate_grid(q + half_tile)],
                                            packed_dtype=BF16)
            words = lax.bitcast_convert_type(packed, jnp.uint32)
            rows = pl.ds(pl.multiple_of(q * PEER_GRID_ROWS, PEER_GRID_ROWS), PEER_GRID_ROWS)
            for e in range(PEER_KEYS // PEER_GRID_ROWS):
                w_ref[e, rows, :] = words[e * PEER_GRID_ROWS:(e + 1) * PEER_GRID_ROWS, :]
        return carry

    def exact_head(h, carry):
        @pl.when(jnp.max(tie_ref[pl.ds(h, 1), :]) > 0.0)
        def _():
            ent_i, ent_j, gate, _ = _head_entries(s_ref, h, _extract_top)
            store(h, ent_i, ent_j, gate)
        return carry

    lax.fori_loop(0, PEER_HEADS, fast_head, 0, unroll=2)

    @pl.when(jnp.max(tie_ref[...]) > 0.0)
    def _():
        lax.fori_loop(0, PEER_HEADS, exact_head, 0)

    pi_ref[...] = ei_ref[...].T
    pj_ref[...] = ej_ref[...].T
    pg_ref[...] = eg_ref[...].T


def _topk(scores_t):
    n_hp, n_keys, t = scores_t.shape
    n_tiles = t // LANES
    tiles_per_block = PEER_BLOCK_T // LANES
    e_steps = PEER_KEYS // PEER_GRID_ROWS
    tile_rows = LANES // 2 * PEER_GRID_ROWS

    def out_index(g):
        tile = jnp.maximum(g - 1, 0)
        return (tile // tiles_per_block, 0, tile % tiles_per_block, 0)

    return pl.pallas_call(
        _topk_kernel,
        grid=(n_tiles + 1,),
        in_specs=[pl.BlockSpec((n_hp, n_keys, LANES),
                               lambda g: (0, 0, jnp.minimum(g, n_tiles - 1)))],
        out_specs=pl.BlockSpec((None, e_steps, tile_rows, PEER_KEYS), out_index),
        out_shape=jax.ShapeDtypeStruct(
            (t // PEER_BLOCK_T, e_steps, tiles_per_block * tile_rows, PEER_KEYS), jnp.uint32),
        scratch_shapes=[pltpu.VMEM((PEER_ENTRIES, LANES), F32)] * 3
        + [pltpu.VMEM((PEER_HEADS, LANES), F32)]
        + [pltpu.VMEM((LANES, PEER_ENTRIES), F32)] * 3,
        compiler_params=pltpu.CompilerParams(
            dimension_semantics=("arbitrary",), vmem_limit_bytes=VMEM_LIMIT),
        name="topk",
    )(scores_t)


PAIR = 2 * PEER_KEYS


def _gelu(x):
    return 0.5 * x * (1.0 + lax.erf(x * (2.0 ** -0.5)))


def _peer_kernel(n_ref, w_ref, u_ref, v_ref, h_ref, nw_ref, o_ref, hid_ref,
                 *, block_t, block_e):
    e_step = pl.program_id(1)
    n_pairs = block_e // PAIR
    grid_rows = block_e // PEER_KEYS
    half_tile = LANES // 2

    @pl.when(e_step == 0)
    def _():
        o_ref[...] = jnp.zeros_like(o_ref)

    def gate_rows(i):
        words = w_ref[pl.ds(i, block_t // 2, stride=grid_rows), :]
        lo, hi = [pltpu.unpack_elementwise(words, index=k, packed_dtype=BF16, unpacked_dtype=F32)
                  for k in range(2)]
        pieces = []
        for tile in range(block_t // LANES):
            rows = slice(tile * half_tile, (tile + 1) * half_tile)
            pieces += [lo[rows], hi[rows]]
        return jnp.concatenate(pieces, axis=0)

    n = n_ref[...]
    for p in range(n_pairs):
        experts = slice(p * PAIR, (p + 1) * PAIR)
        act = _dot_nt(n, u_ref[experts, :])
        w_pair = jnp.concatenate([gate_rows(2 * p), gate_rows(2 * p + 1)], axis=-1)
        hid_ref[:, experts] = (_gelu(act) * w_pair).astype(BF16)
    o_ref[...] += _dot(hid_ref[...], v_ref[...])

    @pl.when(e_step == pl.num_programs(1) - 1)
    def _():
        h = h_ref[...] + o_ref[...]
        o_ref[...] = h * _rms_scale(h) * nw_ref[...]


def _peer(n2, w_packed, u, v, h1, norm_w, block_t, block_e):
    t = n2.shape[0]
    n_exp = u.shape[0]
    tok = lambda w: pl.BlockSpec((block_t, w), lambda i, e: (i, 0))
    return pl.pallas_call(
        functools.partial(_peer_kernel, block_t=block_t, block_e=block_e),
        grid=(t // block_t, n_exp // block_e),
        in_specs=[tok(D_MODEL),
                  pl.BlockSpec((None, None, block_t // 2 * (block_e // PEER_KEYS), PEER_KEYS),
                               lambda i, e: (i, e, 0, 0)),
                  pl.BlockSpec((block_e, D_MODEL), lambda i, e: (e, 0)),
                  pl.BlockSpec((block_e, D_MODEL), lambda i, e: (e, 0)),
                  tok(D_MODEL),
                  pl.BlockSpec((1, D_MODEL), lambda i, e: (0, 0))],
        out_specs=tok(D_MODEL),
        out_shape=jax.ShapeDtypeStruct((t, D_MODEL), F32),
        scratch_shapes=[pltpu.VMEM((block_t, block_e), BF16)],
        compiler_params=pltpu.CompilerParams(
            dimension_semantics=("parallel", "arbitrary"), vmem_limit_bytes=VMEM_LIMIT),
        name="peer",
    )(n2, w_packed, u, v, h1, norm_w)


def _pad_cols(w, width):
    return jnp.pad(w, ((0, 0), (0, width - w.shape[1])))


def _in_proj_slabs(w_in):
    parts, start = [], 0
    for size in _IN_SIZES:
        parts.append(w_in[:, start:start + size])
        start += size
    q, k, v, gate_lr, g_out, z, xbc, dt = parts
    return jnp.concatenate(
        [q, k, v, g_out, z, xbc, _pad_cols(gate_lr, LANES), _pad_cols(dt, LANES)],
        axis=1).astype(BF16)


def _layer(x2, batch, seq, p, peer_u, peer_v):
    row = lambda a: a.reshape(1, -1).astype(F32)
    q, k, v, g_out, z, xbc, gate_lr, dt = _in_proj(
        x2, row(p["norm_mix_w"]), _in_proj_slabs(p["w_in"]), block_t=1024)

    w2p = jnp.pad(p["gla_w_gate2"].astype(F32), ((0, LANES - GLA_RANK), (0, 0)))
    o_gla, u_bf = _gla(q, k, v, gate_lr, g_out, w2p, row(p["gla_b_gate"]), row(p["gla_norm_w"]),
                       peer_u.astype(F32), batch, seq, block_l=1024)

    head_rows = lambda a: jnp.broadcast_to(a.astype(F32).reshape(SSD_HEADS, 1), (SSD_HEADS, LANES))
    d_skip = jnp.repeat(p["ssd_d"].astype(F32), SSD_P).reshape(1, SSD_WIDTH)
    o_ssd, v_bf = _ssd(z, xbc, dt, p["ssd_conv_w"].astype(F32), row(p["ssd_conv_b"]),
                       head_rows(p["ssd_dt_bias"]), head_rows(p["ssd_a_log"]), d_skip,
                       row(p["ssd_norm_w"]), peer_v.astype(F32), batch, seq, block_l=1024)

    keys = p["peer_sub_keys"].reshape(2 * PEER_HEADS, PEER_KEYS, PEER_HALF).astype(BF16)
    h1, n2, scores_t = _out_query(
        o_gla, o_ssd, x2, p["w_out"].astype(BF16), row(p["norm_ffn_w"]),
        p["peer_w_query"].astype(BF16), keys, block_t=512)

    return n2, _topk(scores_t), h1, u_bf, v_bf


def kernel(x, norm_mix_w, w_in, gla_w_gate2, gla_b_gate, gla_norm_w, ssd_conv_w, ssd_conv_b,
           ssd_dt_bias, ssd_a_log, ssd_d, ssd_norm_w, w_out, norm_ffn_w, peer_w_query,
           peer_sub_keys, peer_u, peer_v, norm_final_w):
    batch, seq, d = x.shape
    assert w_in.shape[0] == 1, "single-layer trunk"
    params = dict(norm_mix_w=norm_mix_w, w_in=w_in, gla_w_gate2=gla_w_gate2, gla_b_gate=gla_b_gate,
                  gla_norm_w=gla_norm_w, ssd_conv_w=ssd_conv_w, ssd_conv_b=ssd_conv_b,
                  ssd_dt_bias=ssd_dt_bias, ssd_a_log=ssd_a_log, ssd_d=ssd_d, ssd_norm_w=ssd_norm_w,
                  w_out=w_out, norm_ffn_w=norm_ffn_w, peer_w_query=peer_w_query,
                  peer_sub_keys=peer_sub_keys)
    p = {name: value[0] for name, value in params.items()}
    x2 = x.reshape(batch * seq, d).astype(F32)
    n2, w_packed, h1, u_bf, v_bf = _layer(x2, batch, seq, p, peer_u[0], peer_v[0])
    y = _peer(n2, w_packed, u_bf, v_bf, h1,
              norm_final_w.reshape(1, d).astype(F32), block_t=PEER_BLOCK_T, block_e=PEER_BLOCK_E)
    return y.reshape(batch, seq, d).astype(x.dtype)
```

```python
import functools

import jax
import jax.numpy as jnp
from jax import lax
from jax.experimental import pallas as pl
from jax.experimental.pallas import tpu as pltpu

F32 = jnp.float32
BF16 = jnp.bfloat16
HIGHEST = lax.Precision.HIGHEST

EPS = 1e-6
D_MODEL = 1024

GLA_HEADS = 4
GLA_DK = 64
GLA_DV = 128
GLA_QK = GLA_HEADS * GLA_DK
GLA_V = GLA_HEADS * GLA_DV
GLA_RANK = 16
GLA_NORMALIZER = 16.0
GLA_CHUNK = 64
GLA_CUMSUM_ROWS = 256

SSD_HEADS = 8
SSD_P = 64
SSD_WIDTH = SSD_HEADS * SSD_P
SSD_GROUPS = 2
SSD_N = 64
SSD_CONV = 4
SSD_CHUNK = 128
SSD_BC = SSD_GROUPS * SSD_N
SSD_CONV_DIM = SSD_WIDTH + 2 * SSD_BC

PEER_HEADS = 8
PEER_KEYS = 128
PEER_TOPK = 16
PEER_HALF = 128
PEER_ENTRIES = PEER_HEADS * PEER_TOPK
PEER_BLOCK_T = 1024
PEER_BLOCK_E = 2048
PEER_GRID_ROWS = PEER_BLOCK_E // PEER_KEYS

LANES = 128
SUBLANES = 8
VMEM_LIMIT = 56 * 1024 * 1024

_IN_SIZES = (GLA_QK, GLA_QK, GLA_V, GLA_RANK, GLA_V, SSD_WIDTH, SSD_CONV_DIM, SSD_HEADS)
_SLAB_WIDTHS = (GLA_QK, GLA_QK, GLA_V, GLA_V, SSD_WIDTH, SSD_CONV_DIM, LANES, LANES)


def _dot(a, b, dims=((1,), (0,)), precision=None):
    return lax.dot_general(a, b, (dims, ((), ())), precision=precision,
                           preferred_element_type=F32)


def _dot_nt(a, b, precision=None):
    return _dot(a, b, ((1,), (1,)), precision)


def _dot_tn(a, b, precision=None):
    return _dot(a, b, ((0,), (0,)), precision)


def _silu(x):
    return x * (1.0 / (1.0 + jnp.exp(-x)))


def _softplus(x):
    return jnp.maximum(x, 0.0) + jnp.log(1.0 + jnp.exp(-jnp.abs(x)))


def _rms_scale(x):
    return lax.rsqrt(jnp.mean(x * x, axis=-1, keepdims=True) + EPS)


def _in_proj_kernel(x_ref, nw_ref, w_ref, *out_refs):
    x = x_ref[...]
    n = (x * _rms_scale(x) * nw_ref[...]).astype(BF16)
    start = 0
    for o_ref, width in zip(out_refs, _SLAB_WIDTHS):
        o_ref[...] = _dot(n, w_ref[:, start:start + width])
        start += width


def _in_proj(x2, norm_w, w_slabs, block_t):
    t = x2.shape[0]
    total = sum(_SLAB_WIDTHS)
    out_shape = [jax.ShapeDtypeStruct((t, w), F32) for w in _SLAB_WIDTHS]
    out_specs = [pl.BlockSpec((block_t, w), lambda i: (i, 0)) for w in _SLAB_WIDTHS]
    return pl.pallas_call(
        _in_proj_kernel,
        grid=(t // block_t,),
        in_specs=[pl.BlockSpec((block_t, D_MODEL), lambda i: (i, 0)),
                  pl.BlockSpec((1, D_MODEL), lambda i: (0, 0)),
                  pl.BlockSpec((D_MODEL, total), lambda i: (0, 0))],
        out_specs=out_specs,
        out_shape=out_shape,
        compiler_params=pltpu.CompilerParams(
            dimension_semantics=("parallel",), vmem_limit_bytes=VMEM_LIMIT),
        name="in_proj",
    )(x2, norm_w, w_slabs)


def _gla_kernel(q_ref, k_ref, v_ref, glr_ref, gout_ref, w2_ref, bg_ref, nw_ref, table_ref,
                o_ref, table_bf_ref, st_ref, *, n_chunks):
    c_len = GLA_CHUNK
    table_bf_ref[...] = table_ref[...].astype(BF16)

    @pl.when(pl.program_id(1) == 0)
    def _():
        st_ref[...] = jnp.zeros_like(st_ref)

    def iota(shape, dim):
        return lax.broadcasted_iota(jnp.int32, shape, dim)

    block_l = n_chunks * c_len
    gate = _dot(glr_ref[...], w2_ref[...], precision=HIGHEST) + bg_ref[...]
    log_a = -_softplus(-gate) * (1.0 / GLA_NORMALIZER)
    group = GLA_CUMSUM_ROWS
    tri = ((iota((group, group), 0) // c_len == iota((group, group), 1) // c_len)
           & (iota((group, group), 0) >= iota((group, group), 1))).astype(F32)
    g_cum = jnp.concatenate(
        [_dot(tri, log_a[g * group:(g + 1) * group], precision=HIGHEST)
         for g in range(block_l // group)], axis=0)
    g_end = [g_cum[(c + 1) * c_len - 1:(c + 1) * c_len, :] for c in range(n_chunks)]
    g_last = jnp.concatenate([jnp.broadcast_to(g, (c_len, GLA_QK)) for g in g_end], axis=0)
    k = k_ref[...]
    q_in = (q_ref[...] * (GLA_DK ** -0.5) * jnp.exp(g_cum)).astype(BF16)
    k_in = k * jnp.exp(-g_cum)
    k_end = (k * jnp.exp(g_last - g_cum)).astype(BF16)
    v = v_ref[...]

    k_bd_mask = iota((GLA_QK, GLA_QK), 0) // c_len == iota((GLA_QK, GLA_QK), 1) // GLA_DK
    v_bd_mask = iota((GLA_QK, GLA_V), 0) // c_len == iota((GLA_QK, GLA_V), 1) // GLA_DV
    causal = iota((c_len, GLA_QK), 0) >= iota((c_len, GLA_QK), 1) % c_len
    state_mask = iota((GLA_V, GLA_QK), 0) // GLA_DV == iota((GLA_V, GLA_QK), 1) // GLA_DK

    att = []
    for c in range(n_chunks):
        rows = slice(c * c_len, (c + 1) * c_len)
        k_bd = jnp.where(k_bd_mask, jnp.concatenate([k_in[rows]] * GLA_HEADS, axis=0), 0.0)
        scores = _dot_nt(q_in[rows], k_bd.astype(BF16))
        att.append(jnp.where(causal, scores, 0.0).astype(BF16))

    state_t = st_ref[...]
    nw = jnp.concatenate([nw_ref[...]] * GLA_HEADS, axis=1)
    for c in range(n_chunks):
        rows = slice(c * c_len, (c + 1) * c_len)
        v_c = v[rows]
        new_state = _dot_tn(v_c.astype(BF16), k_end[rows])
        o_inter = _dot_nt(q_in[rows], state_t.astype(BF16))
        state_t = state_t * jnp.exp(g_end[c]) + jnp.where(state_mask, new_state, 0.0)
        v_bd = jnp.where(v_bd_mask, jnp.concatenate([v_c] * GLA_HEADS, axis=0), 0.0)
        o = _dot(att[c], v_bd.astype(BF16)) + o_inter
        outs = []
        for h in range(GLA_HEADS):
            o_h = o[:, h * GLA_DV:(h + 1) * GLA_DV]
            outs.append(o_h * _rms_scale(o_h))
        o = jnp.concatenate(outs, axis=1) * nw
        o_ref[rows, :] = (o * _silu(gout_ref[rows, :])).astype(o_ref.dtype)
    st_ref[...] = state_t


def _table_slab_spec(table, batch, nblk):
    rows, width = table.shape
    slab = rows // (batch * nblk)
    assert slab * batch * nblk == rows and slab % (2 * SUBLANES) == 0
    return pl.BlockSpec((slab, width), lambda b, j: (b * nblk + j, 0))


def _gla(q, k, v, glr, gout, w2p, bg, nw, table, batch, seq, block_l):
    t = batch * seq
    nblk = seq // block_l
    tok = lambda w: pl.BlockSpec((block_l, w), lambda b, j: (b * nblk + j, 0))
    const = lambda shape: pl.BlockSpec(shape, lambda b, j: (0, 0))
    table_spec = _table_slab_spec(table, batch, nblk)
    return pl.pallas_call(
        functools.partial(_gla_kernel, n_chunks=block_l // GLA_CHUNK),
        grid=(batch, nblk),
        in_specs=[tok(GLA_QK), tok(GLA_QK), tok(GLA_V), tok(LANES), tok(GLA_V),
                  const((LANES, GLA_QK)), const((1, GLA_QK)), const((1, GLA_DV)), table_spec],
        out_specs=[tok(GLA_V), table_spec],
        out_shape=[jax.ShapeDtypeStruct((t, GLA_V), BF16),
                   jax.ShapeDtypeStruct(table.shape, BF16)],
        scratch_shapes=[pltpu.VMEM((GLA_V, GLA_QK), F32)],
        compiler_params=pltpu.CompilerParams(
            dimension_semantics=("parallel", "arbitrary"), vmem_limit_bytes=VMEM_LIMIT),
        name="gla",
    )(q, k, v, glr, gout, w2p, bg, nw, table)


def _ssd_kernel(z_ref, xbc_ref, dt_ref, cw_ref, cb_ref, dtb_ref, alog_ref, dskip_ref, nw_ref,
                table_ref, o_ref, table_bf_ref, xs_ref, st_ref, *, block_l):
    c_len = SSD_CHUNK
    halo = SUBLANES
    table_bf_ref[...] = table_ref[...].astype(BF16)

    @pl.when(pl.program_id(1) == 0)
    def _():
        xs_ref[0:halo, :] = jnp.zeros((halo, SSD_CONV_DIM), F32)
        st_ref[...] = jnp.zeros_like(st_ref)

    xs_ref[halo:halo + block_l, :] = xbc_ref[...]

    row = lax.broadcasted_iota(jnp.int32, (c_len, c_len), 0)
    col = lax.broadcasted_iota(jnp.int32, (c_len, c_len), 1)
    causal = row >= col
    upper = (row <= col).astype(F32)
    first_half = col < SSD_N
    cw = cw_ref[...]
    cb = cb_ref[...]
    a_neg = -jnp.exp(alog_ref[...])
    dtb = dtb_ref[...]
    head_pad = jnp.zeros((LANES - SSD_HEADS, c_len), F32)

    def chunk(c, carry):
        r0 = pl.multiple_of(c * c_len, c_len)
        rows = pl.ds(r0, c_len)
        window = xs_ref[pl.ds(r0, c_len + halo), :]
        conv = cb
        for tap in reversed(range(SSD_CONV)):
            shift = halo - (SSD_CONV - 1) + tap
            conv = conv + cw[tap:tap + 1, :] * window[shift:shift + c_len, :]
        xc = _silu(conv)
        b_m = xc[:, SSD_WIDTH:SSD_WIDTH + SSD_BC]
        c_m = xc[:, SSD_WIDTH + SSD_BC:]

        dt_t = _softplus(dt_ref[rows, :].T[0:SSD_HEADS, :] + dtb)
        a_cum_t = _dot(dt_t * a_neg, upper, precision=HIGHEST)
        dt = jnp.concatenate([dt_t, head_pad], axis=0).T
        a_cum = jnp.concatenate([a_cum_t, head_pad], axis=0).T

        scores = []
        c_g = []
        for g in range(SSD_GROUPS):
            in_group = first_half if g == 0 else jnp.logical_not(first_half)
            c_g.append(jnp.where(in_group, c_m, 0.0).astype(BF16))
            scores.append(_dot_nt(c_g[g], b_m.astype(BF16)))

        for m in range(SSD_HEADS // 2):
            g = (2 * m) // (SSD_HEADS // SSD_GROUPS)
            lanes = slice(m * LANES, (m + 1) * LANES)
            x_pair = xc[:, lanes]
            halves = []
            dts = []
            for par in range(2):
                h = 2 * m + par
                dts.append(jnp.broadcast_to(dt[:, h:h + 1], (c_len, LANES)))
            xdt = (x_pair * jnp.where(first_half, dts[0], dts[1])).astype(BF16)
            for par in range(2):
                h = 2 * m + par
                a_col = jnp.broadcast_to(a_cum[:, h:h + 1], (c_len, c_len))
                a_row = jnp.broadcast_to(a_cum_t[h:h + 1, :], (c_len, c_len))
                a_end = a_col[c_len - 1:c_len, :]
                decay = jnp.exp(jnp.where(causal, a_col - a_row, -jnp.inf))
                y_diag = _dot((scores[g] * decay).astype(BF16), xdt)
                state = st_ref[h]
                y_off = _dot(c_g[g], state.astype(BF16)) * jnp.exp(a_col)
                halves.append(y_diag + y_off)
                b_dec = (b_m * jnp.exp(a_end - a_col)).astype(BF16)
                st_ref[h] = state * jnp.exp(a_end) + _dot_tn(b_dec, xdt)
            y = jnp.where(first_half, halves[0], halves[1]) + x_pair * dskip_ref[:, lanes]
            y = y * _silu(z_ref[rows, lanes])
            xs_pair_sq = jnp.sum(y * y, axis=-1, keepdims=True)
            if m % 2 == 0:
                y_prev, sq_prev = y, xs_pair_sq
            else:
                scale = lax.rsqrt((sq_prev + xs_pair_sq) * (1.0 / (2 * LANES)) + EPS)
                lo = slice((m - 1) * LANES, m * LANES)
                o_ref[rows, lo] = (y_prev * scale * nw_ref[:, lo]).astype(o_ref.dtype)
                o_ref[rows, lanes] = (y * scale * nw_ref[:, lanes]).astype(o_ref.dtype)
        return carry

    lax.fori_loop(0, block_l // c_len, chunk, 0)
    xs_ref[0:halo, :] = xs_ref[block_l:block_l + halo, :]


def _ssd(z, xbc, dt, cw, cb, dtb, alog, dskip, nw, table, batch, seq, block_l):
    t = batch * seq
    nblk = seq // block_l
    tok = lambda w: pl.BlockSpec((block_l, w), lambda b, j: (b * nblk + j, 0))
    const = lambda shape: pl.BlockSpec(shape, lambda b, j: (0, 0))
    table_spec = _table_slab_spec(table, batch, nblk)
    return pl.pallas_call(
        functools.partial(_ssd_kernel, block_l=block_l),
        grid=(batch, nblk),
        in_specs=[tok(SSD_WIDTH), tok(SSD_CONV_DIM), tok(LANES),
                  const((SSD_CONV, SSD_CONV_DIM)), const((1, SSD_CONV_DIM)),
                  const((SSD_HEADS, LANES)), const((SSD_HEADS, LANES)), const((1, SSD_WIDTH)),
                  const((1, SSD_WIDTH)), table_spec],
        out_specs=[tok(SSD_WIDTH), table_spec],
        out_shape=[jax.ShapeDtypeStruct((t, SSD_WIDTH), BF16),
                   jax.ShapeDtypeStruct(table.shape, BF16)],
        scratch_shapes=[pltpu.VMEM((block_l + SUBLANES, SSD_CONV_DIM), F32),
                        pltpu.VMEM((SSD_HEADS, SSD_BC, LANES), F32)],
        compiler_params=pltpu.CompilerParams(
            dimension_semantics=("parallel", "arbitrary"), vmem_limit_bytes=VMEM_LIMIT),
        name="ssd",
    )(z, xbc, dt, cw, cb, dtb, alog, dskip, nw, table)


def _out_query_kernel(og_ref, os_ref, x_ref, wo_ref, nw_ref, wq_ref, keys_ref,
                      h_ref, n_ref, s_ref, q_ref):
    mixed = _dot(og_ref[...], wo_ref[0:GLA_V, :]) + _dot(os_ref[...], wo_ref[GLA_V:, :])
    h = x_ref[...] + mixed
    h_ref[...] = h
    n = (h * _rms_scale(h) * nw_ref[...]).astype(BF16)
    n_ref[...] = n
    q_ref[...] = _dot(n, wq_ref[...]).astype(BF16)
    for hp in range(2 * PEER_HEADS):
        lanes = slice(hp * PEER_HALF, (hp + 1) * PEER_HALF)
        s_ref[hp] = _dot_nt(keys_ref[hp], q_ref[:, lanes])


def _out_query(o_gla, o_ssd, x2, w_out, norm_w, w_query, keys, block_t):
    t = x2.shape[0]
    n_hp = 2 * PEER_HEADS
    return pl.pallas_call(
        _out_query_kernel,
        grid=(t // block_t,),
        in_specs=[pl.BlockSpec((block_t, GLA_V), lambda i: (i, 0)),
                  pl.BlockSpec((block_t, SSD_WIDTH), lambda i: (i, 0)),
                  pl.BlockSpec((block_t, D_MODEL), lambda i: (i, 0)),
                  pl.BlockSpec((D_MODEL, D_MODEL), lambda i: (0, 0)),
                  pl.BlockSpec((1, D_MODEL), lambda i: (0, 0)),
                  pl.BlockSpec((D_MODEL, n_hp * PEER_HALF), lambda i: (0, 0)),
                  pl.BlockSpec((n_hp, PEER_KEYS, PEER_HALF), lambda i: (0, 0, 0))],
        out_specs=[pl.BlockSpec((block_t, D_MODEL), lambda i: (i, 0)),
                   pl.BlockSpec((block_t, D_MODEL), lambda i: (i, 0)),
                   pl.BlockSpec((n_hp, PEER_KEYS, block_t), lambda i: (0, 0, i))],
        out_shape=[jax.ShapeDtypeStruct((t, D_MODEL), F32),
                   jax.ShapeDtypeStruct((t, D_MODEL), BF16),
                   jax.ShapeDtypeStruct((n_hp, PEER_KEYS, t), F32)],
        scratch_shapes=[pltpu.VMEM((block_t, n_hp * PEER_HALF), BF16)],
        compiler_params=pltpu.CompilerParams(
            dimension_semantics=("parallel",), vmem_limit_bytes=VMEM_LIMIT),
        name="out_query",
    )(o_gla, o_ssd, x2, w_out, norm_w, w_query, keys)


def _extract_top(values, ids, count):
    lanes = values.shape[1]
    slot = lax.broadcasted_iota(jnp.int32, (count, lanes), 0)
    top_v = jnp.zeros((count, lanes), F32)
    top_i = jnp.zeros((count, lanes), F32)
    for r in range(count):
        m = jnp.max(values, axis=0, keepdims=True)
        sel = jnp.min(jnp.where(values == m, ids, jnp.inf), axis=0, keepdims=True)
        values = jnp.where(ids == sel, -jnp.inf, values)
        top_v = jnp.where(slot == r, m, top_v)
        top_i = jnp.where(slot == r, sel, top_i)
    return top_v, top_i, jnp.zeros((1, lanes), F32)


def _sorting_network(n):
    size = 1 << (n - 1).bit_length()
    pairs = []
    p = 1
    while p < size:
        k = p
        while k >= 1:
            for j in range(k % p, size - k, 2 * k):
                for i in range(min(k, size - j - k)):
                    if (i + j) // (2 * p) == (i + j + k) // (2 * p):
                        pairs.append((i + j, i + j + k))
            k //= 2
        p *= 2
    return [(a, b) for a, b in pairs if b < n]


def _pop_top(values, ids, count):
    rows, lanes = values.shape
    n = rows // SUBLANES
    vals = [values[v * SUBLANES:(v + 1) * SUBLANES] for v in range(n)]
    idl = [ids[v * SUBLANES:(v + 1) * SUBLANES] for v in range(n)]
    for a, b in _sorting_network(n):
        swap = vals[b] > vals[a]
        vals[a], vals[b] = jnp.where(swap, vals[b], vals[a]), jnp.where(swap, vals[a], vals[b])
        idl[a], idl[b] = jnp.where(swap, idl[b], idl[a]), jnp.where(swap, idl[a], idl[b])

    slot = lax.broadcasted_iota(jnp.int32, (count, lanes), 0)
    top_v = jnp.zeros((count, lanes), F32)
    top_i = jnp.zeros((count, lanes), F32)
    tie = jnp.zeros((1, lanes), F32)
    prev = None
    for r in range(count + 1):
        m = jnp.max(vals[0], axis=0, keepdims=True)
        if prev is not None:
            tie = jnp.where(m == prev, 1.0, tie)
        prev = m
        if r == count:
            break
        sel = jnp.min(jnp.where(vals[0] == m, idl[0], jnp.inf), axis=0, keepdims=True)
        top_v = jnp.where(slot == r, m, top_v)
        top_i = jnp.where(slot == r, sel, top_i)
        hit = idl[0] == sel
        depth = min(n - 1, count - r)
        for v in range(depth):
            vals[v] = jnp.where(hit, vals[v + 1], vals[v])
            idl[v] = jnp.where(hit, idl[v + 1], idl[v])
        vals[depth] = jnp.where(hit, -jnp.inf, vals[depth])
    return top_v, top_i, tie


def _gather_rows(table, index):
    row = lax.broadcasted_iota(jnp.int32, table.shape, 0).astype(F32)
    return jnp.sum(jnp.where(row == index, table, 0.0), axis=0, keepdims=True)


def _head_entries(s_ref, h, top_fn):
    k = PEER_TOPK
    lanes = LANES
    key_id = lax.broadcasted_iota(jnp.int32, (PEER_KEYS, lanes), 0).astype(F32)
    sub = lax.broadcasted_iota(jnp.int32, (SUBLANES, lanes), 0).astype(F32)
    slot = lax.broadcasted_iota(jnp.int32, (k, lanes), 0)

    s1, i1, tie1 = top_fn(s_ref[2 * h], key_id, k)
    s2, i2, tie2 = top_fn(s_ref[2 * h + 1], key_id, k)
    pieces, flats = [], []
    for b0 in (0, SUBLANES):
        pieces.append(s1[0:1, :] + s2[b0:b0 + SUBLANES, :])
        flats.append(sub + float(b0))
    for a in range(1, SUBLANES):
        limit = k // (a + 1)
        pieces.append(jnp.where(sub < float(limit), s1[a:a + 1, :] + s2[0:SUBLANES, :], -jnp.inf))
        flats.append(sub + float(a * k))
    pieces.append(s1[SUBLANES:k, :] + s2[0:1, :])
    flats.append((sub + float(SUBLANES)) * float(k))
    cand = jnp.concatenate(pieces, axis=0)
    flat = jnp.concatenate(flats, axis=0)
    best_s, best_flat, tie3 = top_fn(cand, flat, k)

    ent_i = jnp.zeros((k, lanes), F32)
    ent_j = jnp.zeros((k, lanes), F32)
    for r in range(k):
        pos = best_flat[r:r + 1, :]
        a_idx = jnp.floor(pos * (1.0 / k))
        b_idx = pos - a_idx * float(k)
        ent_i = jnp.where(slot == r, _gather_rows(i1, a_idx), ent_i)
        ent_j = jnp.where(slot == r, _gather_rows(i2, b_idx), ent_j)
    e = jnp.exp(best_s - best_s[0:1, :])
    gate = e / jnp.sum(e, axis=0, keepdims=True)
    return ent_i, ent_j, gate, jnp.maximum(jnp.maximum(tie1, tie2), tie3)


def _topk_kernel(s_ref, w_ref, ei_ref, ej_ref, eg_ref, tie_ref, pi_ref, pj_ref, pg_ref):
    k = PEER_TOPK
    half_tile = LANES // 2

    @pl.when(pl.program_id(0) == 0)
    def _():
        pi_ref[...] = jnp.zeros_like(pi_ref)
        pj_ref[...] = jnp.zeros_like(pj_ref)
        pg_ref[...] = jnp.zeros_like(pg_ref)

    grid_row = lax.broadcasted_iota(jnp.int32, (PEER_KEYS, PEER_ENTRIES), 0).astype(F32)

    def gate_grid(t):
        i_row = pi_ref[pl.ds(t, 1), :]
        j_row = pj_ref[pl.ds(t, 1), :]
        g_row = pg_ref[pl.ds(t, 1), :]
        a_t = jnp.where(grid_row == i_row, g_row, 0.0).astype(BF16)
        b_t = jnp.where(grid_row == j_row, 1.0, 0.0).astype(BF16)
        return _dot_nt(a_t, b_t)

    def store(h, ent_i, ent_j, gate):
        rows = pl.ds(pl.multiple_of(h * k, k), k)
        ei_ref[rows, :] = ent_i
        ej_ref[rows, :] = ent_j
        eg_ref[rows, :] = gate

    def fast_head(h, carry):
        ent_i, ent_j, gate, tie = _head_entries(s_ref, h, _pop_top)
        store(h, ent_i, ent_j, gate)
        tie_ref[pl.ds(h, 1), :] = tie
        for n in range(half_tile // PEER_HEADS):
            q = h * (half_tile // PEER_HEADS) + n
            packed = pltpu.pack_elementwise([gate_grid(q), gate_grid(q + half_tile)],
                                            packed_dtype=BF16)
            words = lax.bitcast_convert_type(packed, jnp.uint32)
            rows = pl.ds(pl.multiple_of(q * PEER_GRID_ROWS, PEER_GRID_ROWS), PEER_GRID_ROWS)
            for e in range(PEER_KEYS // PEER_GRID_ROWS):
                w_ref[e, rows, :] = words[e * PEER_GRID_ROWS:(e + 1) * PEER_GRID_ROWS, :]
        return carry

    def exact_head(h, carry):
        @pl.when(jnp.max(tie_ref[pl.ds(h, 1), :]) > 0.0)
        def _():
            ent_i, ent_j, gate, _ = _head_entries(s_ref, h, _extract_top)
            store(h, ent_i, ent_j, gate)
        return carry

    lax.fori_loop(0, PEER_HEADS, fast_head, 0, unroll=2)

    @pl.when(jnp.max(tie_ref[...]) > 0.0)
    def _():
        lax.fori_loop(0, PEER_HEADS, exact_head, 0)

    pi_ref[...] = ei_ref[...].T
    pj_ref[...] = ej_ref[...].T
    pg_ref[...] = eg_ref[...].T


def _topk(scores_t):
    n_hp, n_keys, t = scores_t.shape
    n_tiles = t // LANES
    tiles_per_block = PEER_BLOCK_T // LANES
    e_steps = PEER_KEYS // PEER_GRID_ROWS
    tile_rows = LANES // 2 * PEER_GRID_ROWS

    def out_index(g):
        tile = jnp.maximum(g - 1, 0)
        return (tile // tiles_per_block, 0, tile % tiles_per_block, 0)

    return pl.pallas_call(
        _topk_kernel,
        grid=(n_tiles + 1,),
        in_specs=[pl.BlockSpec((n_hp, n_keys, LANES),
                               lambda g: (0, 0, jnp.minimum(g, n_tiles - 1)))],
        out_specs=pl.BlockSpec((None, e_steps, tile_rows, PEER_KEYS), out_index),
        out_shape=jax.ShapeDtypeStruct(
            (t // PEER_BLOCK_T, e_steps, tiles_per_block * tile_rows, PEER_KEYS), jnp.uint32),
        scratch_shapes=[pltpu.VMEM((PEER_ENTRIES, LANES), F32)] * 3
        + [pltpu.VMEM((PEER_HEADS, LANES), F32)]
        + [pltpu.VMEM((LANES, PEER_ENTRIES), F32)] * 3,
        compiler_params=pltpu.CompilerParams(
            dimension_semantics=("arbitrary",), vmem_limit_bytes=VMEM_LIMIT),
        name="topk",
    )(scores_t)


PAIR = 2 * PEER_KEYS


def _gelu(x):
    return 0.5 * x * (1.0 + lax.erf(x * (2.0 ** -0.5)))


def _peer_kernel(n_ref, w_ref, u_ref, v_ref, h_ref, nw_ref, o_ref, hid_ref,
                 *, block_t, block_e):
    e_step = pl.program_id(1)
    n_pairs = block_e // PAIR
    grid_rows = block_e // PEER_KEYS
    half_tile = LANES // 2

    @pl.when(e_step == 0)
    def _():
        o_ref[...] = jnp.zeros_like(o_ref)

    def gate_rows(i):
        words = w_ref[pl.ds(i, block_t // 2, stride=grid_rows), :]
        lo, hi = [pltpu.unpack_elementwise(words, index=k, packed_dtype=BF16, unpacked_dtype=F32)
                  for k in range(2)]
        pieces = []
        for tile in range(block_t // LANES):
            rows = slice(tile * half_tile, (tile + 1) * half_tile)
            pieces += [lo[rows], hi[rows]]
        return jnp.concatenate(pieces, axis=0)

    n = n_ref[...]
    for p in range(n_pairs):
        experts = slice(p * PAIR, (p + 1) * PAIR)
        act = _dot_nt(n, u_ref[experts, :]).astype(BF16)
        w_pair = jnp.concatenate([gate_rows(2 * p), gate_rows(2 * p + 1)], axis=-1)
        hid_ref[:, experts] = _gelu(act) * w_pair.astype(BF16)
    o_ref[...] += _dot(hid_ref[...], v_ref[...])

    @pl.when(e_step == pl.num_programs(1) - 1)
    def _():
        h = h_ref[...] + o_ref[...]
        o_ref[...] = h * _rms_scale(h) * nw_ref[...]


def _peer(n2, w_packed, u, v, h1, norm_w, block_t, block_e):
    t = n2.shape[0]
    n_exp = u.shape[0]
    tok = lambda w: pl.BlockSpec((block_t, w), lambda i, e: (i, 0))
    return pl.pallas_call(
        functools.partial(_peer_kernel, block_t=block_t, block_e=block_e),
        grid=(t // block_t, n_exp // block_e),
        in_specs=[tok(D_MODEL),
                  pl.BlockSpec((None, None, block_t // 2 * (block_e // PEER_KEYS), PEER_KEYS),
                               lambda i, e: (i, e, 0, 0)),
                  pl.BlockSpec((block_e, D_MODEL), lambda i, e: (e, 0)),
                  pl.BlockSpec((block_e, D_MODEL), lambda i, e: (e, 0)),
                  tok(D_MODEL),
                  pl.BlockSpec((1, D_MODEL), lambda i, e: (0, 0))],
        out_specs=tok(D_MODEL),
        out_shape=jax.ShapeDtypeStruct((t, D_MODEL), F32),
        scratch_shapes=[pltpu.VMEM((block_t, block_e), BF16)],
        compiler_params=pltpu.CompilerParams(
            dimension_semantics=("parallel", "arbitrary"), vmem_limit_bytes=VMEM_LIMIT),
        name="peer",
    )(n2, w_packed, u, v, h1, norm_w)


def _pad_cols(w, width):
    return jnp.pad(w, ((0, 0), (0, width - w.shape[1])))


def _in_proj_slabs(w_in):
    parts, start = [], 0
    for size in _IN_SIZES:
        parts.append(w_in[:, start:start + size])
        start += size
    q, k, v, gate_lr, g_out, z, xbc, dt = parts
    return jnp.concatenate(
        [q, k, v, g_out, z, xbc, _pad_cols(gate_lr, LANES), _pad_cols(dt, LANES)],
        axis=1).astype(BF16)


def _layer(x2, batch, seq, p, peer_u, peer_v):
    row = lambda a: a.reshape(1, -1).astype(F32)
    q, k, v, g_out, z, xbc, gate_lr, dt = _in_proj(
        x2, row(p["norm_mix_w"]), _in_proj_slabs(p["w_in"]), block_t=1024)

    w2p = jnp.pad(p["gla_w_gate2"].astype(F32), ((0, LANES - GLA_RANK), (0, 0)))
    o_gla, u_bf = _gla(q, k, v, gate_lr, g_out, w2p, row(p["gla_b_gate"]), row(p["gla_norm_w"]),
                       peer_u.astype(F32), batch, seq, block_l=1024)

    head_rows = lambda a: jnp.broadcast_to(a.astype(F32).reshape(SSD_HEADS, 1), (SSD_HEADS, LANES))
    d_skip = jnp.repeat(p["ssd_d"].astype(F32), SSD_P).reshape(1, SSD_WIDTH)
    o_ssd, v_bf = _ssd(z, xbc, dt, p["ssd_conv_w"].astype(F32), row(p["ssd_conv_b"]),
                       head_rows(p["ssd_dt_bias"]), head_rows(p["ssd_a_log"]), d_skip,
                       row(p["ssd_norm_w"]), peer_v.astype(F32), batch, seq, block_l=1024)

    keys = p["peer_sub_keys"].reshape(2 * PEER_HEADS, PEER_KEYS, PEER_HALF).astype(BF16)
    h1, n2, scores_t = _out_query(
        o_gla, o_ssd, x2, p["w_out"].astype(BF16), row(p["norm_ffn_w"]),
        p["peer_w_query"].astype(BF16), keys, block_t=512)

    return n2, _topk(scores_t), h1, u_bf, v_bf


def kernel(x, norm_mix_w, w_in, gla_w_gate2, gla_b_gate, gla_norm_w, ssd_conv_w, ssd_conv_b,
           ssd_dt_bias, ssd_a_log, ssd_d, ssd_norm_w, w_out, norm_ffn_w, peer_w_query,
           peer_sub_keys, peer_u, peer_v, norm_final_w):
    batch, seq, d = x.shape
    assert w_in.shape[0] == 1, "single-layer trunk"
    params = dict(norm_mix_w=norm_mix_w, w_in=w_in, gla_w_gate2=gla_w_gate2, gla_b_gate=gla_b_gate,
                  gla_norm_w=gla_norm_w, ssd_conv_w=ssd_conv_w, ssd_conv_b=ssd_conv_b,
                  ssd_dt_bias=ssd_dt_bias, ssd_a_log=ssd_a_log, ssd_d=ssd_d, ssd_norm_w=ssd_norm_w,
                  w_out=w_out, norm_ffn_w=norm_ffn_w, peer_w_query=peer_w_query,
                  peer_sub_keys=peer_sub_keys)
    p = {name: value[0] for name, value in params.items()}
    x2 = x.reshape(batch * seq, d).astype(F32)
    n2, w_packed, h1, u_bf, v_bf = _layer(x2, batch, seq, p, peer_u[0], peer_v[0])
    y = _peer(n2, w_packed, u_bf, v_bf, h1,
              norm_final_w.reshape(1, d).astype(F32), block_t=PEER_BLOCK_T, block_e=PEER_BLOCK_E)
    return y.reshape(batch, seq, d).astype(x.dtype)
```

```python
import functools

import jax
import jax.numpy as jnp
from jax import lax
from jax.experimental import pallas as pl
from jax.experimental.pallas import tpu as pltpu

F32 = jnp.float32
BF16 = jnp.bfloat16
HIGHEST = lax.Precision.HIGHEST

EPS = 1e-6
D_MODEL = 1024

GLA_HEADS = 4
GLA_DK = 64
GLA_DV = 128
GLA_QK = GLA_HEADS * GLA_DK
GLA_V = GLA_HEADS * GLA_DV
GLA_RANK = 16
GLA_NORMALIZER = 16.0
GLA_CHUNK = 64
GLA_CUMSUM_ROWS = 256

SSD_HEADS = 8
SSD_P = 64
SSD_WIDTH = SSD_HEADS * SSD_P
SSD_GROUPS = 2
SSD_N = 64
SSD_CONV = 4
SSD_CHUNK = 128
SSD_BC = SSD_GROUPS * SSD_N
SSD_CONV_DIM = SSD_WIDTH + 2 * SSD_BC

PEER_HEADS = 8
PEER_KEYS = 128
PEER_TOPK = 16
PEER_HALF = 128
PEER_ENTRIES = PEER_HEADS * PEER_TOPK
PEER_BLOCK_T = 1024
PEER_BLOCK_E = 2048
PEER_GRID_ROWS = PEER_BLOCK_E // PEER_KEYS

LANES = 128
SUBLANES = 8
VMEM_LIMIT = 56 * 1024 * 1024

_IN_SIZES = (GLA_QK, GLA_QK, GLA_V, GLA_RANK, GLA_V, SSD_WIDTH, SSD_CONV_DIM, SSD_HEADS)
_SLAB_WIDTHS = (GLA_QK, GLA_QK, GLA_V, GLA_V, SSD_WIDTH, SSD_CONV_DIM, LANES, LANES)


def _dot(a, b, dims=((1,), (0,)), precision=None):
    return lax.dot_general(a, b, (dims, ((), ())), precision=precision,
                           preferred_element_type=F32)


def _dot_nt(a, b, precision=None):
    return _dot(a, b, ((1,), (1,)), precision)


def _dot_tn(a, b, precision=None):
    return _dot(a, b, ((0,), (0,)), precision)


def _silu(x):
    return x * (1.0 / (1.0 + jnp.exp(-x)))


def _softplus(x):
    return jnp.maximum(x, 0.0) + jnp.log(1.0 + jnp.exp(-jnp.abs(x)))


def _rms_scale(x):
    return lax.rsqrt(jnp.mean(x * x, axis=-1, keepdims=True) + EPS)


def _in_proj_kernel(x_ref, nw_ref, w_ref, *out_refs):
    x = x_ref[...]
    n = (x * _rms_scale(x) * nw_ref[...]).astype(BF16)
    start = 0
    for o_ref, width in zip(out_refs, _SLAB_WIDTHS):
        o_ref[...] = _dot(n, w_ref[:, start:start + width])
        start += width


def _in_proj(x2, norm_w, w_slabs, block_t):
    t = x2.shape[0]
    total = sum(_SLAB_WIDTHS)
    out_shape = [jax.ShapeDtypeStruct((t, w), F32) for w in _SLAB_WIDTHS]
    out_specs = [pl.BlockSpec((block_t, w), lambda i: (i, 0)) for w in _SLAB_WIDTHS]
    return pl.pallas_call(
        _in_proj_kernel,
        grid=(t // block_t,),
        in_specs=[pl.BlockSpec((block_t, D_MODEL), lambda i: (i, 0)),
                  pl.BlockSpec((1, D_MODEL), lambda i: (0, 0)),
                  pl.BlockSpec((D_MODEL, total), lambda i: (0, 0))],
        out_specs=out_specs,
        out_shape=out_shape,
        compiler_params=pltpu.CompilerParams(
            dimension_semantics=("parallel",), vmem_limit_bytes=VMEM_LIMIT),
        name="in_proj",
    )(x2, norm_w, w_slabs)


def _gla_kernel(q_ref, k_ref, v_ref, glr_ref, gout_ref, w2_ref, bg_ref, nw_ref, table_ref,
                o_ref, table_bf_ref, st_ref, *, n_chunks):
    c_len = GLA_CHUNK
    table_bf_ref[...] = table_ref[...].astype(BF16)

    @pl.when(pl.program_id(1) == 0)
    def _():
        st_ref[...] = jnp.zeros_like(st_ref)

    def iota(shape, dim):
        return lax.broadcasted_iota(jnp.int32, shape, dim)

    block_l = n_chunks * c_len
    gate = _dot(glr_ref[...], w2_ref[...], precision=HIGHEST) + bg_ref[...]
    log_a = -_softplus(-gate) * (1.0 / GLA_NORMALIZER)
    group = GLA_CUMSUM_ROWS
    tri = ((iota((group, group), 0) // c_len == iota((group, group), 1) // c_len)
           & (iota((group, group), 0) >= iota((group, group), 1))).astype(F32)
    g_cum = jnp.concatenate(
        [_dot(tri, log_a[g * group:(g + 1) * group], precision=HIGHEST)
         for g in range(block_l // group)], axis=0)
    g_end = [g_cum[(c + 1) * c_len - 1:(c + 1) * c_len, :] for c in range(n_chunks)]
    g_last = jnp.concatenate([jnp.broadcast_to(g, (c_len, GLA_QK)) for g in g_end], axis=0)
    k = k_ref[...]
    q_in = (q_ref[...] * (GLA_DK ** -0.5) * jnp.exp(g_cum)).astype(BF16)
    k_in = k * jnp.exp(-g_cum)
    k_end = (k * jnp.exp(g_last - g_cum)).astype(BF16)
    v = v_ref[...]

    k_bd_mask = iota((GLA_QK, GLA_QK), 0) // c_len == iota((GLA_QK, GLA_QK), 1) // GLA_DK
    v_bd_mask = iota((GLA_QK, GLA_V), 0) // c_len == iota((GLA_QK, GLA_V), 1) // GLA_DV
    causal = iota((c_len, GLA_QK), 0) >= iota((c_len, GLA_QK), 1) % c_len
    state_mask = iota((GLA_V, GLA_QK), 0) // GLA_DV == iota((GLA_V, GLA_QK), 1) // GLA_DK

    att = []
    for c in range(n_chunks):
        rows = slice(c * c_len, (c + 1) * c_len)
        k_bd = jnp.where(k_bd_mask, jnp.concatenate([k_in[rows]] * GLA_HEADS, axis=0), 0.0)
        scores = _dot_nt(q_in[rows], k_bd.astype(BF16))
        att.append(jnp.where(causal, scores, 0.0).astype(BF16))

    state_t = st_ref[...]
    nw = jnp.concatenate([nw_ref[...]] * GLA_HEADS, axis=1)
    for c in range(n_chunks):
        rows = slice(c * c_len, (c + 1) * c_len)
        v_c = v[rows]
        new_state = _dot_tn(v_c.astype(BF16), k_end[rows])
        o_inter = _dot_nt(q_in[rows], state_t.astype(BF16))
        state_t = state_t * jnp.exp(g_end[c]) + jnp.where(state_mask, new_state, 0.0)
        v_bd = jnp.where(v_bd_mask, jnp.concatenate([v_c] * GLA_HEADS, axis=0), 0.0)
        o = _dot(att[c], v_bd.astype(BF16)) + o_inter
        outs = []
        for h in range(GLA_HEADS):
            o_h = o[:, h * GLA_DV:(h + 1) * GLA_DV]
            outs.append(o_h * _rms_scale(o_h))
        o = jnp.concatenate(outs, axis=1) * nw
        o_ref[rows, :] = (o * _silu(gout_ref[rows, :])).astype(o_ref.dtype)
    st_ref[...] = state_t


def _table_slab_spec(table, batch, nblk):
    rows, width = table.shape
    slab = rows // (batch * nblk)
    assert slab * batch * nblk == rows and slab % (2 * SUBLANES) == 0
    return pl.BlockSpec((slab, width), lambda b, j: (b * nblk + j, 0))


def _gla(q, k, v, glr, gout, w2p, bg, nw, table, batch, seq, block_l):
    t = batch * seq
    nblk = seq // block_l
    tok = lambda w: pl.BlockSpec((block_l, w), lambda b, j: (b * nblk + j, 0))
    const = lambda shape: pl.BlockSpec(shape, lambda b, j: (0, 0))
    table_spec = _table_slab_spec(table, batch, nblk)
    return pl.pallas_call(
        functools.partial(_gla_kernel, n_chunks=block_l // GLA_CHUNK),
        grid=(batch, nblk),
        in_specs=[tok(GLA_QK), tok(GLA_QK), tok(GLA_V), tok(LANES), tok(GLA_V),
                  const((LANES, GLA_QK)), const((1, GLA_QK)), const((1, GLA_DV)), table_spec],
        out_specs=[tok(GLA_V), table_spec],
        out_shape=[jax.ShapeDtypeStruct((t, GLA_V), BF16),
                   jax.ShapeDtypeStruct(table.shape, BF16)],
        scratch_shapes=[pltpu.VMEM((GLA_V, GLA_QK), F32)],
        compiler_params=pltpu.CompilerParams(
            dimension_semantics=("parallel", "arbitrary"), vmem_limit_bytes=VMEM_LIMIT),
        name="gla",
    )(q, k, v, glr, gout, w2p, bg, nw, table)


def _ssd_kernel(z_ref, xbc_ref, dt_ref, cw_ref, cb_ref, dtb_ref, alog_ref, dskip_ref, nw_ref,
                table_ref, o_ref, table_bf_ref, xs_ref, st_ref, *, block_l):
    c_len = SSD_CHUNK
    halo = SUBLANES
    table_bf_ref[...] = table_ref[...].astype(BF16)

    @pl.when(pl.program_id(1) == 0)
    def _():
        xs_ref[0:halo, :] = jnp.zeros((halo, SSD_CONV_DIM), F32)
        st_ref[...] = jnp.zeros_like(st_ref)

    xs_ref[halo:halo + block_l, :] = xbc_ref[...]

    row = lax.broadcasted_iota(jnp.int32, (c_len, c_len), 0)
    col = lax.broadcasted_iota(jnp.int32, (c_len, c_len), 1)
    causal = row >= col
    upper = (row <= col).astype(F32)
    first_half = col < SSD_N
    cw = cw_ref[...]
    cb = cb_ref[...]
    a_neg = -jnp.exp(alog_ref[...])
    dtb = dtb_ref[...]
    head_pad = jnp.zeros((LANES - SSD_HEADS, c_len), F32)

    def chunk(c, carry):
        r0 = pl.multiple_of(c * c_len, c_len)
        rows = pl.ds(r0, c_len)
        window = xs_ref[pl.ds(r0, c_len + halo), :]
        conv = cb
        for tap in reversed(range(SSD_CONV)):
            shift = halo - (SSD_CONV - 1) + tap
            conv = conv + cw[tap:tap + 1, :] * window[shift:shift + c_len, :]
        xc = _silu(conv)
        b_m = xc[:, SSD_WIDTH:SSD_WIDTH + SSD_BC]
        c_m = xc[:, SSD_WIDTH + SSD_BC:]

        dt_t = _softplus(dt_ref[rows, :].T[0:SSD_HEADS, :] + dtb)
        a_cum_t = _dot(dt_t * a_neg, upper, precision=HIGHEST)
        dt = jnp.concatenate([dt_t, head_pad], axis=0).T
        a_cum = jnp.concatenate([a_cum_t, head_pad], axis=0).T

        scores = []
        c_g = []
        for g in range(SSD_GROUPS):
            in_group = first_half if g == 0 else jnp.logical_not(first_half)
            c_g.append(jnp.where(in_group, c_m, 0.0).astype(BF16))
            scores.append(_dot_nt(c_g[g], b_m.astype(BF16)))

        for m in range(SSD_HEADS // 2):
            g = (2 * m) // (SSD_HEADS // SSD_GROUPS)
            lanes = slice(m * LANES, (m + 1) * LANES)
            x_pair = xc[:, lanes]
            halves = []
            dts = []
            for par in range(2):
                h = 2 * m + par
                dts.append(jnp.broadcast_to(dt[:, h:h + 1], (c_len, LANES)))
            xdt = (x_pair * jnp.where(first_half, dts[0], dts[1])).astype(BF16)
            for par in range(2):
                h = 2 * m + par
                a_col = jnp.broadcast_to(a_cum[:, h:h + 1], (c_len, c_len))
                a_row = jnp.broadcast_to(a_cum_t[h:h + 1, :], (c_len, c_len))
                a_end = a_col[c_len - 1:c_len, :]
                decay = jnp.exp(jnp.where(causal, a_col - a_row, -jnp.inf))
                y_diag = _dot((scores[g] * decay).astype(BF16), xdt)
                state = st_ref[h]
                y_off = _dot(c_g[g], state.astype(BF16)) * jnp.exp(a_col)
                halves.append(y_diag + y_off)
                b_dec = (b_m * jnp.exp(a_end - a_col)).astype(BF16)
                st_ref[h] = state * jnp.exp(a_end) + _dot_tn(b_dec, xdt)
            y = jnp.where(first_half, halves[0], halves[1]) + x_pair * dskip_ref[:, lanes]
            y = y * _silu(z_ref[rows, lanes])
            xs_pair_sq = jnp.sum(y * y, axis=-1, keepdims=True)
            if m % 2 == 0:
                y_prev, sq_prev = y, xs_pair_sq
            else:
                scale = lax.rsqrt((sq_prev + xs_pair_sq) * (1.0 / (2 * LANES)) + EPS)
                lo = slice((m - 1) * LANES, m * LANES)
                o_ref[rows, lo] = (y_prev * scale * nw_ref[:, lo]).astype(o_ref.dtype)
                o_ref[rows, lanes] = (y * scale * nw_ref[:, lanes]).astype(o_ref.dtype)
        return carry

    lax.fori_loop(0, block_l // c_len, chunk, 0)
    xs_ref[0:halo, :] = xs_ref[block_l:block_l + halo, :]


def _ssd(z, xbc, dt, cw, cb, dtb, alog, dskip, nw, table, batch, seq, block_l):
    t = batch * seq
    nblk = seq // block_l
    tok = lambda w: pl.BlockSpec((block_l, w), lambda b, j: (b * nblk + j, 0))
    const = lambda shape: pl.BlockSpec(shape, lambda b, j: (0, 0))
    table_spec = _table_slab_spec(table, batch, nblk)
    return pl.pallas_call(
        functools.partial(_ssd_kernel, block_l=block_l),
        grid=(batch, nblk),
        in_specs=[tok(SSD_WIDTH), tok(SSD_CONV_DIM), tok(LANES),
                  const((SSD_CONV, SSD_CONV_DIM)), const((1, SSD_CONV_DIM)),
                  const((SSD_HEADS, LANES)), const((SSD_HEADS, LANES)), const((1, SSD_WIDTH)),
                  const((1, SSD_WIDTH)), table_spec],
        out_specs=[tok(SSD_WIDTH), table_spec],
        out_shape=[jax.ShapeDtypeStruct((t, SSD_WIDTH), BF16),
                   jax.ShapeDtypeStruct(table.shape, BF16)],
        scratch_shapes=[pltpu.VMEM((block_l + SUBLANES, SSD_CONV_DIM), F32),
                        pltpu.VMEM((SSD_HEADS, SSD_BC, LANES), F32)],
        compiler_params=pltpu.CompilerParams(
            dimension_semantics=("parallel", "arbitrary"), vmem_limit_bytes=VMEM_LIMIT),
        name="ssd",
    )(z, xbc, dt, cw, cb, dtb, alog, dskip, nw, table)


def _out_query_kernel(og_ref, os_ref, x_ref, wo_ref, nw_ref, wq_ref, keys_ref,
                      h_ref, n_ref, s_ref, q_ref):
    mixed = _dot(og_ref[...], wo_ref[0:GLA_V, :]) + _dot(os_ref[...], wo_ref[GLA_V:, :])
    h = x_ref[...] + mixed
    h_ref[...] = h
    n = (h * _rms_scale(h) * nw_ref[...]).astype(BF16)
    n_ref[...] = n
    q_ref[...] = _dot(n, wq_ref[...]).astype(BF16)
    for hp in range(2 * PEER_HEADS):
        lanes = slice(hp * PEER_HALF, (hp + 1) * PEER_HALF)
        s_ref[hp] = _dot_nt(keys_ref[hp], q_ref[:, lanes])


def _out_query(o_gla, o_ssd, x2, w_out, norm_w, w_query, keys, block_t):
    t = x2.shape[0]
    n_hp = 2 * PEER_HEADS
    return pl.pallas_call(
        _out_query_kernel,
        grid=(t // block_t,),
        in_specs=[pl.BlockSpec((block_t, GLA_V), lambda i: (i, 0)),
                  pl.BlockSpec((block_t, SSD_WIDTH), lambda i: (i, 0)),
                  pl.BlockSpec((block_t, D_MODEL), lambda i: (i, 0)),
                  pl.BlockSpec((D_MODEL, D_MODEL), lambda i: (0, 0)),
                  pl.BlockSpec((1, D_MODEL), lambda i: (0, 0)),
                  pl.BlockSpec((D_MODEL, n_hp * PEER_HALF), lambda i: (0, 0)),
                  pl.BlockSpec((n_hp, PEER_KEYS, PEER_HALF), lambda i: (0, 0, 0))],
        out_specs=[pl.BlockSpec((block_t, D_MODEL), lambda i: (i, 0)),
                   pl.BlockSpec((block_t, D_MODEL), lambda i: (i, 0)),
                   pl.BlockSpec((n_hp, PEER_KEYS, block_t), lambda i: (0, 0, i))],
        out_shape=[jax.ShapeDtypeStruct((t, D_MODEL), F32),
                   jax.ShapeDtypeStruct((t, D_MODEL), BF16),
                   jax.ShapeDtypeStruct((n_hp, PEER_KEYS, t), F32)],
        scratch_shapes=[pltpu.VMEM((block_t, n_hp * PEER_HALF), BF16)],
        compiler_params=pltpu.CompilerParams(
            dimension_semantics=("parallel",), vmem_limit_bytes=VMEM_LIMIT),
        name="out_query",
    )(o_gla, o_ssd, x2, w_out, norm_w, w_query, keys)


def _extract_top(values, ids, count):
    lanes = values.shape[1]
    slot = lax.broadcasted_iota(jnp.int32, (count, lanes), 0)
    top_v = jnp.zeros((count, lanes), F32)
    top_i = jnp.zeros((count, lanes), F32)
    for r in range(count):
        m = jnp.max(values, axis=0, keepdims=True)
        sel = jnp.min(jnp.where(values == m, ids, jnp.inf), axis=0, keepdims=True)
        values = jnp.where(ids == sel, -jnp.inf, values)
        top_v = jnp.where(slot == r, m, top_v)
        top_i = jnp.where(slot == r, sel, top_i)
    return top_v, top_i, jnp.zeros((1, lanes), F32)


def _sorting_network(n):
    size = 1 << (n - 1).bit_length()
    pairs = []
    p = 1
    while p < size:
        k = p
        while k >= 1:
            for j in range(k % p, size - k, 2 * k):
                for i in range(min(k, size - j - k)):
                    if (i + j) // (2 * p) == (i + j + k) // (2 * p):
                        pairs.append((i + j, i + j + k))
            k //= 2
        p *= 2
    return [(a, b) for a, b in pairs if b < n]


def _pop_top(values, ids, count):
    rows, lanes = values.shape
    n = rows // SUBLANES
    vals = [values[v * SUBLANES:(v + 1) * SUBLANES] for v in range(n)]
    idl = [ids[v * SUBLANES:(v + 1) * SUBLANES] for v in range(n)]
    for a, b in _sorting_network(n):
        swap = vals[b] > vals[a]
        vals[a], vals[b] = jnp.where(swap, vals[b], vals[a]), jnp.where(swap, vals[a], vals[b])
        idl[a], idl[b] = jnp.where(swap, idl[b], idl[a]), jnp.where(swap, idl[a], idl[b])

    slot = lax.broadcasted_iota(jnp.int32, (count, lanes), 0)
    top_v = jnp.zeros((count, lanes), F32)
    top_i = jnp.zeros((count, lanes), F32)
    tie = jnp.zeros((1, lanes), F32)
    prev = None
    for r in range(count + 1):
        m = jnp.max(vals[0], axis=0, keepdims=True)
        if prev is not None:
            tie = jnp.where(m == prev, 1.0, tie)
        prev = m
        if r == count:
            break
        sel = jnp.min(jnp.where(vals[0] == m, idl[0], jnp.inf), axis=0, keepdims=True)
        top_v = jnp.where(slot == r, m, top_v)
        top_i = jnp.where(slot == r, sel, top_i)
        hit = idl[0] == sel
        depth = min(n - 1, count - r)
        for v in range(depth):
            vals[v] = jnp.where(hit, vals[v + 1], vals[v])
            idl[v] = jnp.where(hit, idl[v + 1], idl[v])
        vals[depth] = jnp.where(hit, -jnp.inf, vals[depth])
    return top_v, top_i, tie


def _gather_rows(table, index):
    row = lax.broadcasted_iota(jnp.int32, table.shape, 0).astype(F32)
    return jnp.sum(jnp.where(row == index, table, 0.0), axis=0, keepdims=True)


def _head_entries(s_ref, h, top_fn):
    k = PEER_TOPK
    lanes = LANES
    key_id = lax.broadcasted_iota(jnp.int32, (PEER_KEYS, lanes), 0).astype(F32)
    sub = lax.broadcasted_iota(jnp.int32, (SUBLANES, lanes), 0).astype(F32)
    slot = lax.broadcasted_iota(jnp.int32, (k, lanes), 0)

    s1, i1, tie1 = top_fn(s_ref[2 * h], key_id, k)
    s2, i2, tie2 = top_fn(s_ref[2 * h + 1], key_id, k)
    pieces, flats = [], []
    for b0 in (0, SUBLANES):
        pieces.append(s1[0:1, :] + s2[b0:b0 + SUBLANES, :])
        flats.append(sub + float(b0))
    for a in range(1, SUBLANES):
        limit = k // (a + 1)
        pieces.append(jnp.where(sub < float(limit), s1[a:a + 1, :] + s2[0:SUBLANES, :], -jnp.inf))
        flats.append(sub + float(a * k))
    pieces.append(s1[SUBLANES:k, :] + s2[0:1, :])
    flats.append((sub + float(SUBLANES)) * float(k))
    cand = jnp.concatenate(pieces, axis=0)
    flat = jnp.concatenate(flats, axis=0)
    best_s, best_flat, tie3 = top_fn(cand, flat, k)

    ent_i = jnp.zeros((k, lanes), F32)
    ent_j = jnp.zeros((k, lanes), F32)
    for r in range(k):
        pos = best_flat[r:r + 1, :]
        a_idx = jnp.floor(pos * (1.0 / k))
        b_idx = pos - a_idx * float(k)
        ent_i = jnp.where(slot == r, _gather_rows(i1, a_idx), ent_i)
        ent_j = jnp.where(slot == r, _gather_rows(i2, b_idx), ent_j)
    e = jnp.exp(best_s - best_s[0:1, :])
    gate = e / jnp.sum(e, axis=0, keepdims=True)
    return ent_i, ent_j, gate, jnp.maximum(jnp.maximum(tie1, tie2), tie3)


def _topk_kernel(s_ref, w_ref, ei_ref, ej_ref, eg_ref, tie_ref, pi_ref, pj_ref, pg_ref,
                 redo_ref):
    k = PEER_TOPK
    half_tile = LANES // 2

    @pl.when(pl.program_id(0) == 0)
    def _():
        pi_ref[...] = jnp.zeros_like(pi_ref)
        pj_ref[...] = jnp.zeros_like(pj_ref)
        pg_ref[...] = jnp.zeros_like(pg_ref)

    grid_row = lax.broadcasted_iota(jnp.int32, (PEER_KEYS, PEER_ENTRIES), 0).astype(F32)

    def gate_grid(t):
        i_row = pi_ref[pl.ds(t, 1), :]
        j_row = pj_ref[pl.ds(t, 1), :]
        g_row = pg_ref[pl.ds(t, 1), :]
        a_t = jnp.where(grid_row == i_row, g_row, 0.0).astype(BF16)
        b_t = jnp.where(grid_row == j_row, 1.0, 0.0).astype(BF16)
        return _dot_nt(a_t, b_t)

    def store(h, ent_i, ent_j, gate):
        rows = pl.ds(pl.multiple_of(h * k, k), k)
        ei_ref[rows, :] = ent_i
        ej_ref[rows, :] = ent_j
        eg_ref[rows, :] = gate

    def fast_head(h, carry):
        ent_i, ent_j, gate, tie = _head_entries(s_ref, h, _pop_top)
        store(h, ent_i, ent_j, gate)
        tie_ref[pl.ds(h, 1), :] = tie
        for n in range(half_tile // PEER_HEADS):
            q = h * (half_tile // PEER_HEADS) + n
            packed = pltpu.pack_elementwise([gate_grid(q), gate_grid(q + half_tile)],
                                            packed_dtype=BF16)
            words = lax.bitcast_convert_type(packed, jnp.uint32)
            rows = pl.ds(pl.multiple_of(q * PEER_GRID_ROWS, PEER_GRID_ROWS), PEER_GRID_ROWS)
            for e in range(PEER_KEYS // PEER_GRID_ROWS):
                w_ref[e, rows, :] = words[e * PEER_GRID_ROWS:(e + 1) * PEER_GRID_ROWS, :]
        return carry

    def exact_head(h, carry):
        @pl.when(redo_ref[h] > 0)
        def _():
            ent_i, ent_j, gate, _ = _head_entries(s_ref, h, _extract_top)
            store(h, ent_i, ent_j, gate)
        return carry

    lax.fori_loop(0, PEER_HEADS, fast_head, 0, unroll=2)

    @pl.when(jnp.max(tie_ref[...]) > 0.0)
    def _():
        for h in range(PEER_HEADS):
            redo_ref[h] = (jnp.max(tie_ref[h:h + 1, :]) > 0.0).astype(jnp.int32)
        lax.fori_loop(0, PEER_HEADS, exact_head, 0)

    pi_ref[...] = ei_ref[...].T
    pj_ref[...] = ej_ref[...].T
    pg_ref[...] = eg_ref[...].T


def _topk(scores_t):
    n_hp, n_keys, t = scores_t.shape
    n_tiles = t // LANES
    tiles_per_block = PEER_BLOCK_T // LANES
    e_steps = PEER_KEYS // PEER_GRID_ROWS
    tile_rows = LANES // 2 * PEER_GRID_ROWS

    def out_index(g):
        tile = jnp.maximum(g - 1, 0)
        return (tile // tiles_per_block, 0, tile % tiles_per_block, 0)

    return pl.pallas_call(
        _topk_kernel,
        grid=(n_tiles + 1,),
        in_specs=[pl.BlockSpec((n_hp, n_keys, LANES),
                               lambda g: (0, 0, jnp.minimum(g, n_tiles - 1)))],
        out_specs=pl.BlockSpec((None, e_steps, tile_rows, PEER_KEYS), out_index),
        out_shape=jax.ShapeDtypeStruct(
            (t // PEER_BLOCK_T, e_steps, tiles_per_block * tile_rows, PEER_KEYS), jnp.uint32),
        scratch_shapes=[pltpu.VMEM((PEER_ENTRIES, LANES), F32)] * 3
        + [pltpu.VMEM((PEER_HEADS, LANES), F32)]
        + [pltpu.VMEM((LANES, PEER_ENTRIES), F32)] * 3
        + [pltpu.SMEM((PEER_HEADS,), jnp.int32)],
        compiler_params=pltpu.CompilerParams(
            dimension_semantics=("arbitrary",), vmem_limit_bytes=VMEM_LIMIT),
        name="topk",
    )(scores_t)


PAIR = 2 * PEER_KEYS


def _gelu(x):
    return 0.5 * x * (1.0 + lax.erf(x * (2.0 ** -0.5)))


def _peer_kernel(n_ref, w_ref, u_ref, v_ref, h_ref, nw_ref, o_ref, hid_ref,
                 *, block_t, block_e):
    e_step = pl.program_id(1)
    n_pairs = block_e // PAIR
    grid_rows = block_e // PEER_KEYS
    half_tile = LANES // 2

    @pl.when(e_step == 0)
    def _():
        o_ref[...] = jnp.zeros_like(o_ref)

    def gate_rows(i):
        words = w_ref[pl.ds(i, block_t // 2, stride=grid_rows), :]
        lo, hi = [pltpu.unpack_elementwise(words, index=k, packed_dtype=BF16, unpacked_dtype=F32)
                  for k in range(2)]
        pieces = []
        for tile in range(block_t // LANES):
            rows = slice(tile * half_tile, (tile + 1) * half_tile)
            pieces += [lo[rows], hi[rows]]
        return jnp.concatenate(pieces, axis=0)

    n = n_ref[...]
    for p in range(n_pairs):
        experts = slice(p * PAIR, (p + 1) * PAIR)
        act = _dot_nt(n, u_ref[experts, :]).astype(BF16)
        w_pair = jnp.concatenate([gate_rows(2 * p), gate_rows(2 * p + 1)], axis=-1)
        hid_ref[:, experts] = _gelu(act) * w_pair.astype(BF16)
    o_ref[...] += _dot(hid_ref[...], v_ref[...])

    @pl.when(e_step == pl.num_programs(1) - 1)
    def _():
        h = h_ref[...] + o_ref[...]
        o_ref[...] = h * _rms_scale(h) * nw_ref[...]


def _peer(n2, w_packed, u, v, h1, norm_w, block_t, block_e):
    t = n2.shape[0]
    n_exp = u.shape[0]
    tok = lambda w: pl.BlockSpec((block_t, w), lambda i, e: (i, 0))
    return pl.pallas_call(
        functools.partial(_peer_kernel, block_t=block_t, block_e=block_e),
        grid=(t // block_t, n_exp // block_e),
        in_specs=[tok(D_MODEL),
                  pl.BlockSpec((None, None, block_t // 2 * (block_e // PEER_KEYS), PEER_KEYS),
                               lambda i, e: (i, e, 0, 0)),
                  pl.BlockSpec((block_e, D_MODEL), lambda i, e: (e, 0)),
                  pl.BlockSpec((block_e, D_MODEL), lambda i, e: (e, 0)),
                  tok(D_MODEL),
                  pl.BlockSpec((1, D_MODEL), lambda i, e: (0, 0))],
        out_specs=tok(D_MODEL),
        out_shape=jax.ShapeDtypeStruct((t, D_MODEL), F32),
        scratch_shapes=[pltpu.VMEM((block_t, block_e), BF16)],
        compiler_params=pltpu.CompilerParams(
            dimension_semantics=("parallel", "arbitrary"), vmem_limit_bytes=VMEM_LIMIT),
        name="peer",
    )(n2, w_packed, u, v, h1, norm_w)


def _pad_cols(w, width):
    return jnp.pad(w, ((0, 0), (0, width - w.shape[1])))


def _in_proj_slabs(w_in):
    parts, start = [], 0
    for size in _IN_SIZES:
        parts.append(w_in[:, start:start + size])
        start += size
    q, k, v, gate_lr, g_out, z, xbc, dt = parts
    return jnp.concatenate(
        [q, k, v, g_out, z, xbc, _pad_cols(gate_lr, LANES), _pad_cols(dt, LANES)],
        axis=1).astype(BF16)


def _layer(x2, batch, seq, p, peer_u, peer_v):
    row = lambda a: a.reshape(1, -1).astype(F32)
    q, k, v, g_out, z, xbc, gate_lr, dt = _in_proj(
        x2, row(p["norm_mix_w"]), _in_proj_slabs(p["w_in"]), block_t=1024)

    w2p = jnp.pad(p["gla_w_gate2"].astype(F32), ((0, LANES - GLA_RANK), (0, 0)))
    o_gla, u_bf = _gla(q, k, v, gate_lr, g_out, w2p, row(p["gla_b_gate"]), row(p["gla_norm_w"]),
                       peer_u.astype(F32), batch, seq, block_l=1024)

    head_rows = lambda a: jnp.broadcast_to(a.astype(F32).reshape(SSD_HEADS, 1), (SSD_HEADS, LANES))
    d_skip = jnp.repeat(p["ssd_d"].astype(F32), SSD_P).reshape(1, SSD_WIDTH)
    o_ssd, v_bf = _ssd(z, xbc, dt, p["ssd_conv_w"].astype(F32), row(p["ssd_conv_b"]),
                       head_rows(p["ssd_dt_bias"]), head_rows(p["ssd_a_log"]), d_skip,
                       row(p["ssd_norm_w"]), peer_v.astype(F32), batch, seq, block_l=1024)

    keys = p["peer_sub_keys"].reshape(2 * PEER_HEADS, PEER_KEYS, PEER_HALF).astype(BF16)
    h1, n2, scores_t = _out_query(
        o_gla, o_ssd, x2, p["w_out"].astype(BF16), row(p["norm_ffn_w"]),
        p["peer_w_query"].astype(BF16), keys, block_t=512)

    return n2, _topk(scores_t), h1, u_bf, v_bf


def kernel(x, norm_mix_w, w_in, gla_w_gate2, gla_b_gate, gla_norm_w, ssd_conv_w, ssd_conv_b,
           ssd_dt_bias, ssd_a_log, ssd_d, ssd_norm_w, w_out, norm_ffn_w, peer_w_query,
           peer_sub_keys, peer_u, peer_v, norm_final_w):
    batch, seq, d = x.shape
    assert w_in.shape[0] == 1, "single-layer trunk"
    params = dict(norm_mix_w=norm_mix_w, w_in=w_in, gla_w_gate2=gla_w_gate2, gla_b_gate=gla_b_gate,
                  gla_norm_w=gla_norm_w, ssd_conv_w=ssd_conv_w, ssd_conv_b=ssd_conv_b,
                  ssd_dt_bias=ssd_dt_bias, ssd_a_log=ssd_a_log, ssd_d=ssd_d, ssd_norm_w=ssd_norm_w,
                  w_out=w_out, norm_ffn_w=norm_ffn_w, peer_w_query=peer_w_query,
                  peer_sub_keys=peer_sub_keys)
    p = {name: value[0] for name, value in params.items()}
    x2 = x.reshape(batch * seq, d).astype(F32)
    n2, w_packed, h1, u_bf, v_bf = _layer(x2, batch, seq, p, peer_u[0], peer_v[0])
    y = _peer(n2, w_packed, u_bf, v_bf, h1,
              norm_final_w.reshape(1, d).astype(F32), block_t=PEER_BLOCK_T, block_e=PEER_BLOCK_E)
    return y.reshape(batch, seq, d).astype(x.dtype)
```

```python
import functools

import jax
import jax.numpy as jnp
from jax import lax
from jax.experimental import pallas as pl
from jax.experimental.pallas import tpu as pltpu

F32 = jnp.float32
BF16 = jnp.bfloat16
HIGHEST = lax.Precision.HIGHEST

EPS = 1e-6
D_MODEL = 1024

GLA_HEADS = 4
GLA_DK = 64
GLA_DV = 128
GLA_QK = GLA_HEADS * GLA_DK
GLA_V = GLA_HEADS * GLA_DV
GLA_RANK = 16
GLA_NORMALIZER = 16.0
GLA_CHUNK = 64
GLA_CUMSUM_ROWS = 256

SSD_HEADS = 8
SSD_P = 64
SSD_WIDTH = SSD_HEADS * SSD_P
SSD_GROUPS = 2
SSD_N = 64
SSD_CONV = 4
SSD_CHUNK = 128
SSD_BC = SSD_GROUPS * SSD_N
SSD_CONV_DIM = SSD_WIDTH + 2 * SSD_BC

PEER_HEADS = 8
PEER_KEYS = 128
PEER_TOPK = 16
PEER_HALF = 128
PEER_ENTRIES = PEER_HEADS * PEER_TOPK
PEER_BLOCK_T = 1024
PEER_BLOCK_E = 2048
PEER_GRID_ROWS = PEER_BLOCK_E // PEER_KEYS

LANES = 128
SUBLANES = 8
VMEM_LIMIT = 56 * 1024 * 1024

_IN_SIZES = (GLA_QK, GLA_QK, GLA_V, GLA_RANK, GLA_V, SSD_WIDTH, SSD_CONV_DIM, SSD_HEADS)
_SLAB_WIDTHS = (GLA_QK, GLA_QK, GLA_V, GLA_V, SSD_WIDTH, SSD_CONV_DIM, LANES, LANES)


def _dot(a, b, dims=((1,), (0,)), precision=None):
    return lax.dot_general(a, b, (dims, ((), ())), precision=precision,
                           preferred_element_type=F32)


def _dot_nt(a, b, precision=None):
    return _dot(a, b, ((1,), (1,)), precision)


def _dot_tn(a, b, precision=None):
    return _dot(a, b, ((0,), (0,)), precision)


def _silu(x):
    return x * (1.0 / (1.0 + jnp.exp(-x)))


def _softplus(x):
    return jnp.maximum(x, 0.0) + jnp.log(1.0 + jnp.exp(-jnp.abs(x)))


def _rms_scale(x):
    return lax.rsqrt(jnp.mean(x * x, axis=-1, keepdims=True) + EPS)


def _in_proj_kernel(x_ref, nw_ref, w_ref, *out_refs):
    x = x_ref[...]
    n = (x * _rms_scale(x) * nw_ref[...]).astype(BF16)
    start = 0
    for o_ref, width in zip(out_refs, _SLAB_WIDTHS):
        o_ref[...] = _dot(n, w_ref[:, start:start + width])
        start += width


def _in_proj(x2, norm_w, w_slabs, block_t):
    t = x2.shape[0]
    total = sum(_SLAB_WIDTHS)
    out_shape = [jax.ShapeDtypeStruct((t, w), F32) for w in _SLAB_WIDTHS]
    out_specs = [pl.BlockSpec((block_t, w), lambda i: (i, 0)) for w in _SLAB_WIDTHS]
    return pl.pallas_call(
        _in_proj_kernel,
        grid=(t // block_t,),
        in_specs=[pl.BlockSpec((block_t, D_MODEL), lambda i: (i, 0)),
                  pl.BlockSpec((1, D_MODEL), lambda i: (0, 0)),
                  pl.BlockSpec((D_MODEL, total), lambda i: (0, 0))],
        out_specs=out_specs,
        out_shape=out_shape,
        compiler_params=pltpu.CompilerParams(
            dimension_semantics=("parallel",), vmem_limit_bytes=VMEM_LIMIT),
        name="in_proj",
    )(x2, norm_w, w_slabs)


def _gla_kernel(q_ref, k_ref, v_ref, glr_ref, gout_ref, w2_ref, bg_ref, nw_ref, table_ref,
                o_ref, table_bf_ref, st_ref, *, n_chunks):
    c_len = GLA_CHUNK
    table_bf_ref[...] = table_ref[...].astype(BF16)

    @pl.when(pl.program_id(1) == 0)
    def _():
        st_ref[...] = jnp.zeros_like(st_ref)

    def iota(shape, dim):
        return lax.broadcasted_iota(jnp.int32, shape, dim)

    block_l = n_chunks * c_len
    gate = _dot(glr_ref[...], w2_ref[...], precision=HIGHEST) + bg_ref[...]
    log_a = -_softplus(-gate) * (1.0 / GLA_NORMALIZER)
    group = GLA_CUMSUM_ROWS
    tri = ((iota((group, group), 0) // c_len == iota((group, group), 1) // c_len)
           & (iota((group, group), 0) >= iota((group, group), 1))).astype(BF16)

    def cumsum_group(x):
        hi = x.astype(BF16)
        rest = x - hi.astype(F32)
        mid = rest.astype(BF16)
        lo = (rest - mid.astype(F32)).astype(BF16)
        return _dot(tri, hi) + _dot(tri, mid) + _dot(tri, lo)

    g_cum = jnp.concatenate(
        [cumsum_group(log_a[g * group:(g + 1) * group]) for g in range(block_l // group)], axis=0)
    g_end = [g_cum[(c + 1) * c_len - 1:(c + 1) * c_len, :] for c in range(n_chunks)]
    g_last = jnp.concatenate([jnp.broadcast_to(g, (c_len, GLA_QK)) for g in g_end], axis=0)
    k = k_ref[...]
    q_in = (q_ref[...] * (GLA_DK ** -0.5) * jnp.exp(g_cum)).astype(BF16)
    k_in = k * jnp.exp(-g_cum)
    k_end = (k * jnp.exp(g_last - g_cum)).astype(BF16)
    v = v_ref[...]

    k_bd_mask = iota((GLA_QK, GLA_QK), 0) // c_len == iota((GLA_QK, GLA_QK), 1) // GLA_DK
    v_bd_mask = iota((GLA_QK, GLA_V), 0) // c_len == iota((GLA_QK, GLA_V), 1) // GLA_DV
    causal = iota((c_len, GLA_QK), 0) >= iota((c_len, GLA_QK), 1) % c_len
    state_mask = iota((GLA_V, GLA_QK), 0) // GLA_DV == iota((GLA_V, GLA_QK), 1) // GLA_DK

    att = []
    for c in range(n_chunks):
        rows = slice(c * c_len, (c + 1) * c_len)
        k_bd = jnp.where(k_bd_mask, jnp.concatenate([k_in[rows]] * GLA_HEADS, axis=0), 0.0)
        scores = _dot_nt(q_in[rows], k_bd.astype(BF16))
        att.append(jnp.where(causal, scores, 0.0).astype(BF16))

    state_t = st_ref[...]
    nw = jnp.concatenate([nw_ref[...]] * GLA_HEADS, axis=1)
    for c in range(n_chunks):
        rows = slice(c * c_len, (c + 1) * c_len)
        v_c = v[rows]
        new_state = _dot_tn(v_c.astype(BF16), k_end[rows])
        o_inter = _dot_nt(q_in[rows], state_t.astype(BF16))
        state_t = state_t * jnp.exp(g_end[c]) + jnp.where(state_mask, new_state, 0.0)
        v_bd = jnp.where(v_bd_mask, jnp.concatenate([v_c] * GLA_HEADS, axis=0), 0.0)
        o = _dot(att[c], v_bd.astype(BF16)) + o_inter
        outs = []
        for h in range(GLA_HEADS):
            o_h = o[:, h * GLA_DV:(h + 1) * GLA_DV]
            outs.append(o_h * _rms_scale(o_h))
        o = jnp.concatenate(outs, axis=1) * nw
        o_ref[rows, :] = (o * _silu(gout_ref[rows, :])).astype(o_ref.dtype)
    st_ref[...] = state_t


def _table_slab_spec(table, batch, nblk):
    rows, width = table.shape
    slab = rows // (batch * nblk)
    assert slab * batch * nblk == rows and slab % (2 * SUBLANES) == 0
    return pl.BlockSpec((slab, width), lambda b, j: (b * nblk + j, 0))


def _gla(q, k, v, glr, gout, w2p, bg, nw, table, batch, seq, block_l):
    t = batch * seq
    nblk = seq // block_l
    tok = lambda w: pl.BlockSpec((block_l, w), lambda b, j: (b * nblk + j, 0))
    const = lambda shape: pl.BlockSpec(shape, lambda b, j: (0, 0))
    table_spec = _table_slab_spec(table, batch, nblk)
    return pl.pallas_call(
        functools.partial(_gla_kernel, n_chunks=block_l // GLA_CHUNK),
        grid=(batch, nblk),
        in_specs=[tok(GLA_QK), tok(GLA_QK), tok(GLA_V), tok(LANES), tok(GLA_V),
                  const((LANES, GLA_QK)), const((1, GLA_QK)), const((1, GLA_DV)), table_spec],
        out_specs=[tok(GLA_V), table_spec],
        out_shape=[jax.ShapeDtypeStruct((t, GLA_V), BF16),
                   jax.ShapeDtypeStruct(table.shape, BF16)],
        scratch_shapes=[pltpu.VMEM((GLA_V, GLA_QK), F32)],
        compiler_params=pltpu.CompilerParams(
            dimension_semantics=("parallel", "arbitrary"), vmem_limit_bytes=VMEM_LIMIT),
        name="gla",
    )(q, k, v, glr, gout, w2p, bg, nw, table)


def _ssd_kernel(z_ref, xbc_ref, dt_ref, cw_ref, cb_ref, dtb_ref, alog_ref, dskip_ref, nw_ref,
                table_ref, o_ref, table_bf_ref, xs_ref, st_ref, *, block_l):
    c_len = SSD_CHUNK
    halo = SUBLANES
    table_bf_ref[...] = table_ref[...].astype(BF16)

    @pl.when(pl.program_id(1) == 0)
    def _():
        xs_ref[0:halo, :] = jnp.zeros((halo, SSD_CONV_DIM), F32)
        st_ref[...] = jnp.zeros_like(st_ref)

    xs_ref[halo:halo + block_l, :] = xbc_ref[...]

    row = lax.broadcasted_iota(jnp.int32, (c_len, c_len), 0)
    col = lax.broadcasted_iota(jnp.int32, (c_len, c_len), 1)
    causal = row >= col
    upper = (row <= col).astype(F32)
    first_half = col < SSD_N
    cw = cw_ref[...]
    cb = cb_ref[...]
    a_neg = -jnp.exp(alog_ref[...])
    dtb = dtb_ref[...]
    head_pad = jnp.zeros((LANES - SSD_HEADS, c_len), F32)

    def chunk(c, carry):
        r0 = pl.multiple_of(c * c_len, c_len)
        rows = pl.ds(r0, c_len)
        window = xs_ref[pl.ds(r0, c_len + halo), :]
        conv = cb
        for tap in reversed(range(SSD_CONV)):
            shift = halo - (SSD_CONV - 1) + tap
            conv = conv + cw[tap:tap + 1, :] * window[shift:shift + c_len, :]
        xc = _silu(conv)
        b_m = xc[:, SSD_WIDTH:SSD_WIDTH + SSD_BC]
        c_m = xc[:, SSD_WIDTH + SSD_BC:]

        dt_t = _softplus(dt_ref[rows, :].T[0:SSD_HEADS, :] + dtb)
        a_cum_t = _dot(dt_t * a_neg, upper, precision=HIGHEST)
        dt = jnp.concatenate([dt_t, head_pad], axis=0).T
        a_cum = jnp.concatenate([a_cum_t, head_pad], axis=0).T

        scores = []
        c_g = []
        for g in range(SSD_GROUPS):
            in_group = first_half if g == 0 else jnp.logical_not(first_half)
            c_g.append(jnp.where(in_group, c_m, 0.0).astype(BF16))
            scores.append(_dot_nt(c_g[g], b_m.astype(BF16)))

        for m in range(SSD_HEADS // 2):
            g = (2 * m) // (SSD_HEADS // SSD_GROUPS)
            lanes = slice(m * LANES, (m + 1) * LANES)
            x_pair = xc[:, lanes]
            halves = []
            dts = []
            for par in range(2):
                h = 2 * m + par
                dts.append(jnp.broadcast_to(dt[:, h:h + 1], (c_len, LANES)))
            xdt = (x_pair * jnp.where(first_half, dts[0], dts[1])).astype(BF16)
            for par in range(2):
                h = 2 * m + par
                a_col = jnp.broadcast_to(a_cum[:, h:h + 1], (c_len, c_len))
                a_row = jnp.broadcast_to(a_cum_t[h:h + 1, :], (c_len, c_len))
                a_end = a_col[c_len - 1:c_len, :]
                decay = jnp.exp(jnp.where(causal, a_col - a_row, -jnp.inf))
                y_diag = _dot((scores[g] * decay).astype(BF16), xdt)
                state = st_ref[h]
                y_off = _dot(c_g[g], state.astype(BF16)) * jnp.exp(a_col)
                halves.append(y_diag + y_off)
                b_dec = (b_m * jnp.exp(a_end - a_col)).astype(BF16)
                st_ref[h] = state * jnp.exp(a_end) + _dot_tn(b_dec, xdt)
            y = jnp.where(first_half, halves[0], halves[1]) + x_pair * dskip_ref[:, lanes]
            y = y * _silu(z_ref[rows, lanes])
            xs_pair_sq = jnp.sum(y * y, axis=-1, keepdims=True)
            if m % 2 == 0:
                y_prev, sq_prev = y, xs_pair_sq
            else:
                scale = lax.rsqrt((sq_prev + xs_pair_sq) * (1.0 / (2 * LANES)) + EPS)
                lo = slice((m - 1) * LANES, m * LANES)
                o_ref[rows, lo] = (y_prev * scale * nw_ref[:, lo]).astype(o_ref.dtype)
                o_ref[rows, lanes] = (y * scale * nw_ref[:, lanes]).astype(o_ref.dtype)
        return carry

    lax.fori_loop(0, block_l // c_len, chunk, 0)
    xs_ref[0:halo, :] = xs_ref[block_l:block_l + halo, :]


def _ssd(z, xbc, dt, cw, cb, dtb, alog, dskip, nw, table, batch, seq, block_l):
    t = batch * seq
    nblk = seq // block_l
    tok = lambda w: pl.BlockSpec((block_l, w), lambda b, j: (b * nblk + j, 0))
    const = lambda shape: pl.BlockSpec(shape, lambda b, j: (0, 0))
    table_spec = _table_slab_spec(table, batch, nblk)
    return pl.pallas_call(
        functools.partial(_ssd_kernel, block_l=block_l),
        grid=(batch, nblk),
        in_specs=[tok(SSD_WIDTH), tok(SSD_CONV_DIM), tok(LANES),
                  const((SSD_CONV, SSD_CONV_DIM)), const((1, SSD_CONV_DIM)),
                  const((SSD_HEADS, LANES)), const((SSD_HEADS, LANES)), const((1, SSD_WIDTH)),
                  const((1, SSD_WIDTH)), table_spec],
        out_specs=[tok(SSD_WIDTH), table_spec],
        out_shape=[jax.ShapeDtypeStruct((t, SSD_WIDTH), BF16),
                   jax.ShapeDtypeStruct(table.shape, BF16)],
        scratch_shapes=[pltpu.VMEM((block_l + SUBLANES, SSD_CONV_DIM), F32),
                        pltpu.VMEM((SSD_HEADS, SSD_BC, LANES), F32)],
        compiler_params=pltpu.CompilerParams(
            dimension_semantics=("parallel", "arbitrary"), vmem_limit_bytes=VMEM_LIMIT),
        name="ssd",
    )(z, xbc, dt, cw, cb, dtb, alog, dskip, nw, table)


def _out_query_kernel(og_ref, os_ref, x_ref, wo_ref, nw_ref, wq_ref, keys_ref,
                      h_ref, n_ref, s_ref, q_ref):
    mixed = _dot(og_ref[...], wo_ref[0:GLA_V, :]) + _dot(os_ref[...], wo_ref[GLA_V:, :])
    h = x_ref[...] + mixed
    h_ref[...] = h
    n = (h * _rms_scale(h) * nw_ref[...]).astype(BF16)
    n_ref[...] = n
    q_ref[...] = _dot(n, wq_ref[...]).astype(BF16)
    for hp in range(2 * PEER_HEADS):
        lanes = slice(hp * PEER_HALF, (hp + 1) * PEER_HALF)
        s_ref[hp] = _dot_nt(keys_ref[hp], q_ref[:, lanes])


def _out_query(o_gla, o_ssd, x2, w_out, norm_w, w_query, keys, block_t):
    t = x2.shape[0]
    n_hp = 2 * PEER_HEADS
    return pl.pallas_call(
        _out_query_kernel,
        grid=(t // block_t,),
        in_specs=[pl.BlockSpec((block_t, GLA_V), lambda i: (i, 0)),
                  pl.BlockSpec((block_t, SSD_WIDTH), lambda i: (i, 0)),
                  pl.BlockSpec((block_t, D_MODEL), lambda i: (i, 0)),
                  pl.BlockSpec((D_MODEL, D_MODEL), lambda i: (0, 0)),
                  pl.BlockSpec((1, D_MODEL), lambda i: (0, 0)),
                  pl.BlockSpec((D_MODEL, n_hp * PEER_HALF), lambda i: (0, 0)),
                  pl.BlockSpec((n_hp, PEER_KEYS, PEER_HALF), lambda i: (0, 0, 0))],
        out_specs=[pl.BlockSpec((block_t, D_MODEL), lambda i: (i, 0)),
                   pl.BlockSpec((block_t, D_MODEL), lambda i: (i, 0)),
                   pl.BlockSpec((n_hp, PEER_KEYS, block_t), lambda i: (0, 0, i))],
        out_shape=[jax.ShapeDtypeStruct((t, D_MODEL), F32),
                   jax.ShapeDtypeStruct((t, D_MODEL), BF16),
                   jax.ShapeDtypeStruct((n_hp, PEER_KEYS, t), F32)],
        scratch_shapes=[pltpu.VMEM((block_t, n_hp * PEER_HALF), BF16)],
        compiler_params=pltpu.CompilerParams(
            dimension_semantics=("parallel",), vmem_limit_bytes=VMEM_LIMIT),
        name="out_query",
    )(o_gla, o_ssd, x2, w_out, norm_w, w_query, keys)


def _extract_top(values, ids, count):
    lanes = values.shape[1]
    slot = lax.broadcasted_iota(jnp.int32, (count, lanes), 0)
    top_v = jnp.zeros((count, lanes), F32)
    top_i = jnp.zeros((count, lanes), F32)
    for r in range(count):
        m = jnp.max(values, axis=0, keepdims=True)
        sel = jnp.min(jnp.where(values == m, ids, jnp.inf), axis=0, keepdims=True)
        values = jnp.where(ids == sel, -jnp.inf, values)
        top_v = jnp.where(slot == r, m, top_v)
        top_i = jnp.where(slot == r, sel, top_i)
    return top_v, top_i, jnp.zeros((1, lanes), F32)


def _sorting_network(n):
    size = 1 << (n - 1).bit_length()
    pairs = []
    p = 1
    while p < size:
        k = p
        while k >= 1:
            for j in range(k % p, size - k, 2 * k):
                for i in range(min(k, size - j - k)):
                    if (i + j) // (2 * p) == (i + j + k) // (2 * p):
                        pairs.append((i + j, i + j + k))
            k //= 2
        p *= 2
    return [(a, b) for a, b in pairs if b < n]


def _pop_top(values, ids, count):
    rows, lanes = values.shape
    n = rows // SUBLANES
    vals = [values[v * SUBLANES:(v + 1) * SUBLANES] for v in range(n)]
    idl = [ids[v * SUBLANES:(v + 1) * SUBLANES] for v in range(n)]
    for a, b in _sorting_network(n):
        swap = vals[b] > vals[a]
        vals[a], vals[b] = jnp.where(swap, vals[b], vals[a]), jnp.where(swap, vals[a], vals[b])
        idl[a], idl[b] = jnp.where(swap, idl[b], idl[a]), jnp.where(swap, idl[a], idl[b])

    slot = lax.broadcasted_iota(jnp.int32, (count, lanes), 0)
    top_v = jnp.zeros((count, lanes), F32)
    top_i = jnp.zeros((count, lanes), F32)
    tie = jnp.zeros((1, lanes), F32)
    prev = None
    for r in range(count + 1):
        m = jnp.max(vals[0], axis=0, keepdims=True)
        if prev is not None:
            tie = jnp.where(m == prev, 1.0, tie)
        prev = m
        if r == count:
            break
        sel = jnp.min(jnp.where(vals[0] == m, idl[0], jnp.inf), axis=0, keepdims=True)
        top_v = jnp.where(slot == r, m, top_v)
        top_i = jnp.where(slot == r, sel, top_i)
        hit = idl[0] == sel
        depth = min(n - 1, count - r)
        for v in range(depth):
            vals[v] = jnp.where(hit, vals[v + 1], vals[v])
            idl[v] = jnp.where(hit, idl[v + 1], idl[v])
        vals[depth] = jnp.where(hit, -jnp.inf, vals[depth])
    return top_v, top_i, tie


def _gather_rows(table, index):
    row = lax.broadcasted_iota(jnp.int32, table.shape, 0).astype(F32)
    return jnp.sum(jnp.where(row == index, table, 0.0), axis=0, keepdims=True)


def _head_entries(s_ref, h, top_fn):
    k = PEER_TOPK
    lanes = LANES
    key_id = lax.broadcasted_iota(jnp.int32, (PEER_KEYS, lanes), 0).astype(F32)
    sub = lax.broadcasted_iota(jnp.int32, (SUBLANES, lanes), 0).astype(F32)
    slot = lax.broadcasted_iota(jnp.int32, (k, lanes), 0)

    s1, i1, tie1 = top_fn(s_ref[2 * h], key_id, k)
    s2, i2, tie2 = top_fn(s_ref[2 * h + 1], key_id, k)
    pieces, flats = [], []
    for b0 in (0, SUBLANES):
        pieces.append(s1[0:1, :] + s2[b0:b0 + SUBLANES, :])
        flats.append(sub + float(b0))
    for a in range(1, SUBLANES):
        limit = k // (a + 1)
        pieces.append(jnp.where(sub < float(limit), s1[a:a + 1, :] + s2[0:SUBLANES, :], -jnp.inf))
        flats.append(sub + float(a * k))
    pieces.append(s1[SUBLANES:k, :] + s2[0:1, :])
    flats.append((sub + float(SUBLANES)) * float(k))
    cand = jnp.concatenate(pieces, axis=0)
    flat = jnp.concatenate(flats, axis=0)
    best_s, best_flat, tie3 = top_fn(cand, flat, k)

    ent_i = jnp.zeros((k, lanes), F32)
    ent_j = jnp.zeros((k, lanes), F32)
    for r in range(k):
        pos = best_flat[r:r + 1, :]
        a_idx = jnp.floor(pos * (1.0 / k))
        b_idx = pos - a_idx * float(k)
        ent_i = jnp.where(slot == r, _gather_rows(i1, a_idx), ent_i)
        ent_j = jnp.where(slot == r, _gather_rows(i2, b_idx), ent_j)
    e = jnp.exp(best_s - best_s[0:1, :])
    gate = e / jnp.sum(e, axis=0, keepdims=True)
    return ent_i, ent_j, gate, jnp.maximum(jnp.maximum(tie1, tie2), tie3)


def _topk_kernel(s_ref, w_ref, ei_ref, ej_ref, eg_ref, tie_ref, pi_ref, pj_ref, pg_ref,
                 redo_ref):
    k = PEER_TOPK
    half_tile = LANES // 2

    @pl.when(pl.program_id(0) == 0)
    def _():
        pi_ref[...] = jnp.zeros_like(pi_ref)
        pj_ref[...] = jnp.zeros_like(pj_ref)
        pg_ref[...] = jnp.zeros_like(pg_ref)

    grid_row = lax.broadcasted_iota(jnp.int32, (PEER_KEYS, PEER_ENTRIES), 0).astype(F32)

    def gate_grid(t):
        i_row = pi_ref[pl.ds(t, 1), :]
        j_row = pj_ref[pl.ds(t, 1), :]
        g_row = pg_ref[pl.ds(t, 1), :]
        a_t = jnp.where(grid_row == i_row, g_row, 0.0).astype(BF16)
        b_t = jnp.where(grid_row == j_row, 1.0, 0.0).astype(BF16)
        return _dot_nt(a_t, b_t)

    def store(h, ent_i, ent_j, gate):
        rows = pl.ds(pl.multiple_of(h * k, k), k)
        ei_ref[rows, :] = ent_i
        ej_ref[rows, :] = ent_j
        eg_ref[rows, :] = gate

    def fast_head(h, carry):
        ent_i, ent_j, gate, tie = _head_entries(s_ref, h, _pop_top)
        store(h, ent_i, ent_j, gate)
        tie_ref[pl.ds(h, 1), :] = tie
        for n in range(half_tile // PEER_HEADS):
            q = h * (half_tile // PEER_HEADS) + n
            packed = pltpu.pack_elementwise([gate_grid(q), gate_grid(q + half_tile)],
                                            packed_dtype=BF16)
            words = lax.bitcast_convert_type(packed, jnp.uint32)
            rows = pl.ds(pl.multiple_of(q * PEER_GRID_ROWS, PEER_GRID_ROWS), PEER_GRID_ROWS)
            for e in range(PEER_KEYS // PEER_GRID_ROWS):
                w_ref[e, rows, :] = words[e * PEER_GRID_ROWS:(e + 1) * PEER_GRID_ROWS, :]
        return carry

    def exact_head(h, carry):
        @pl.when(redo_ref[h] > 0)
        def _():
            ent_i, ent_j, gate, _ = _head_entries(s_ref, h, _extract_top)
            store(h, ent_i, ent_j, gate)
        return carry

    lax.fori_loop(0, PEER_HEADS, fast_head, 0, unroll=2)

    @pl.when(jnp.max(tie_ref[...]) > 0.0)
    def _():
        for h in range(PEER_HEADS):
            redo_ref[h] = (jnp.max(tie_ref[h:h + 1, :]) > 0.0).astype(jnp.int32)
        lax.fori_loop(0, PEER_HEADS, exact_head, 0)

    pi_ref[...] = ei_ref[...].T
    pj_ref[...] = ej_ref[...].T
    pg_ref[...] = eg_ref[...].T


def _topk(scores_t):
    n_hp, n_keys, t = scores_t.shape
    n_tiles = t // LANES
    tiles_per_block = PEER_BLOCK_T // LANES
    e_steps = PEER_KEYS // PEER_GRID_ROWS
    tile_rows = LANES // 2 * PEER_GRID_ROWS

    def out_index(g):
        tile = jnp.maximum(g - 1, 0)
        return (tile // tiles_per_block, 0, tile % tiles_per_block, 0)

    return pl.pallas_call(
        _topk_kernel,
        grid=(n_tiles + 1,),
        in_specs=[pl.BlockSpec((n_hp, n_keys, LANES),
                               lambda g: (0, 0, jnp.minimum(g, n_tiles - 1)))],
        out_specs=pl.BlockSpec((None, e_steps, tile_rows, PEER_KEYS), out_index),
        out_shape=jax.ShapeDtypeStruct(
            (t // PEER_BLOCK_T, e_steps, tiles_per_block * tile_rows, PEER_KEYS), jnp.uint32),
        scratch_shapes=[pltpu.VMEM((PEER_ENTRIES, LANES), F32)] * 3
        + [pltpu.VMEM((PEER_HEADS, LANES), F32)]
        + [pltpu.VMEM((LANES, PEER_ENTRIES), F32)] * 3
        + [pltpu.SMEM((PEER_HEADS,), jnp.int32)],
        compiler_params=pltpu.CompilerParams(
            dimension_semantics=("arbitrary",), vmem_limit_bytes=VMEM_LIMIT),
        name="topk",
    )(scores_t)


PAIR = 2 * PEER_KEYS


def _gelu(x):
    return 0.5 * x * (1.0 + lax.erf(x * (2.0 ** -0.5)))


def _peer_kernel(n_ref, w_ref, u_ref, v_ref, h_ref, nw_ref, o_ref, hid_ref,
                 *, block_t, block_e):
    e_step = pl.program_id(1)
    n_pairs = block_e // PAIR
    grid_rows = block_e // PEER_KEYS
    half_tile = LANES // 2

    @pl.when(e_step == 0)
    def _():
        o_ref[...] = jnp.zeros_like(o_ref)

    def gate_rows(i):
        words = w_ref[pl.ds(i, block_t // 2, stride=grid_rows), :]
        lo, hi = [pltpu.unpack_elementwise(words, index=k, packed_dtype=BF16, unpacked_dtype=F32)
                  for k in range(2)]
        pieces = []
        for tile in range(block_t // LANES):
            rows = slice(tile * half_tile, (tile + 1) * half_tile)
            pieces += [lo[rows], hi[rows]]
        return jnp.concatenate(pieces, axis=0)

    n = n_ref[...]
    for p in range(n_pairs):
        experts = slice(p * PAIR, (p + 1) * PAIR)
        act = _dot_nt(n, u_ref[experts, :]).astype(BF16)
        w_pair = jnp.concatenate([gate_rows(2 * p), gate_rows(2 * p + 1)], axis=-1)
        hid_ref[:, experts] = _gelu(act) * w_pair.astype(BF16)
    o_ref[...] += _dot(hid_ref[...], v_ref[...])

    @pl.when(e_step == pl.num_programs(1) - 1)
    def _():
        h = h_ref[...] + o_ref[...]
        o_ref[...] = h * _rms_scale(h) * nw_ref[...]


def _peer(n2, w_packed, u, v, h1, norm_w, block_t, block_e):
    t = n2.shape[0]
    n_exp = u.shape[0]
    tok = lambda w: pl.BlockSpec((block_t, w), lambda i, e: (i, 0))
    return pl.pallas_call(
        functools.partial(_peer_kernel, block_t=block_t, block_e=block_e),
        grid=(t // block_t, n_exp // block_e),
        in_specs=[tok(D_MODEL),
                  pl.BlockSpec((None, None, block_t // 2 * (block_e // PEER_KEYS), PEER_KEYS),
                               lambda i, e: (i, e, 0, 0)),
                  pl.BlockSpec((block_e, D_MODEL), lambda i, e: (e, 0)),
                  pl.BlockSpec((block_e, D_MODEL), lambda i, e: (e, 0)),
                  tok(D_MODEL),
                  pl.BlockSpec((1, D_MODEL), lambda i, e: (0, 0))],
        out_specs=tok(D_MODEL),
        out_shape=jax.ShapeDtypeStruct((t, D_MODEL), F32),
        scratch_shapes=[pltpu.VMEM((block_t, block_e), BF16)],
        compiler_params=pltpu.CompilerParams(
            dimension_semantics=("parallel", "arbitrary"), vmem_limit_bytes=VMEM_LIMIT),
        name="peer",
    )(n2, w_packed, u, v, h1, norm_w)


def _pad_cols(w, width):
    return jnp.pad(w, ((0, 0), (0, width - w.shape[1])))


def _in_proj_slabs(w_in):
    parts, start = [], 0
    for size in _IN_SIZES:
        parts.append(w_in[:, start:start + size])
        start += size
    q, k, v, gate_lr, g_out, z, xbc, dt = parts
    return jnp.concatenate(
        [q, k, v, g_out, z, xbc, _pad_cols(gate_lr, LANES), _pad_cols(dt, LANES)],
        axis=1).astype(BF16)


def _layer(x2, batch, seq, p, peer_u, peer_v):
    row = lambda a: a.reshape(1, -1).astype(F32)
    q, k, v, g_out, z, xbc, gate_lr, dt = _in_proj(
        x2, row(p["norm_mix_w"]), _in_proj_slabs(p["w_in"]), block_t=1024)

    w2p = jnp.pad(p["gla_w_gate2"].astype(F32), ((0, LANES - GLA_RANK), (0, 0)))
    o_gla, u_bf = _gla(q, k, v, gate_lr, g_out, w2p, row(p["gla_b_gate"]), row(p["gla_norm_w"]),
                       peer_u.astype(F32), batch, seq, block_l=1024)

    head_rows = lambda a: jnp.broadcast_to(a.astype(F32).reshape(SSD_HEADS, 1), (SSD_HEADS, LANES))
    d_skip = jnp.repeat(p["ssd_d"].astype(F32), SSD_P).reshape(1, SSD_WIDTH)
    o_ssd, v_bf = _ssd(z, xbc, dt, p["ssd_conv_w"].astype(F32), row(p["ssd_conv_b"]),
                       head_rows(p["ssd_dt_bias"]), head_rows(p["ssd_a_log"]), d_skip,
                       row(p["ssd_norm_w"]), peer_v.astype(F32), batch, seq, block_l=1024)

    keys = p["peer_sub_keys"].reshape(2 * PEER_HEADS, PEER_KEYS, PEER_HALF).astype(BF16)
    h1, n2, scores_t = _out_query(
        o_gla, o_ssd, x2, p["w_out"].astype(BF16), row(p["norm_ffn_w"]),
        p["peer_w_query"].astype(BF16), keys, block_t=512)

    return n2, _topk(scores_t), h1, u_bf, v_bf


def kernel(x, norm_mix_w, w_in, gla_w_gate2, gla_b_gate, gla_norm_w, ssd_conv_w, ssd_conv_b,
           ssd_dt_bias, ssd_a_log, ssd_d, ssd_norm_w, w_out, norm_ffn_w, peer_w_query,
           peer_sub_keys, peer_u, peer_v, norm_final_w):
    batch, seq, d = x.shape
    assert w_in.shape[0] == 1, "single-layer trunk"
    params = dict(norm_mix_w=norm_mix_w, w_in=w_in, gla_w_gate2=gla_w_gate2, gla_b_gate=gla_b_gate,
                  gla_norm_w=gla_norm_w, ssd_conv_w=ssd_conv_w, ssd_conv_b=ssd_conv_b,
                  ssd_dt_bias=ssd_dt_bias, ssd_a_log=ssd_a_log, ssd_d=ssd_d, ssd_norm_w=ssd_norm_w,
                  w_out=w_out, norm_ffn_w=norm_ffn_w, peer_w_query=peer_w_query,
                  peer_sub_keys=peer_sub_keys)
    p = {name: value[0] for name, value in params.items()}
    x2 = x.reshape(batch * seq, d).astype(F32)
    n2, w_packed, h1, u_bf, v_bf = _layer(x2, batch, seq, p, peer_u[0], peer_v[0])
    y = _peer(n2, w_packed, u_bf, v_bf, h1,
              norm_final_w.reshape(1, d).astype(F32), block_t=PEER_BLOCK_T, block_e=PEER_BLOCK_E)
    return y.reshape(batch, seq, d).astype(x.dtype)
```
